```python
import jax, jax.numpy as jnp
from jax import lax
import numpy as np

D_MODEL = 1024
BATCH = 16
SEQ = 2048
DEPTH = 2

N_EVEN = (DEPTH + 1) // 2
N_ODD = DEPTH // 2
CHUNK = 128

M_HEAD_DIM = 64
M_HEADS = D_MODEL // M_HEAD_DIM
M_INNER = M_HEADS * M_HEAD_DIM
M_GROUPS = 2
M_STATE = 128
M_CONV = 4
M_CONV_DIM = M_INNER + 2 * M_GROUPS * M_STATE

R_HEAD_DIM = 64
R_HEADS = D_MODEL // R_HEAD_DIM
R_INNER = R_HEADS * R_HEAD_DIM
R_DECAY_LORA = 64
R_AAA_LORA = 64
R_GATE_LORA = 160
R_SHIFT_DIM = 3 * R_INNER + R_DECAY_LORA + R_AAA_LORA + R_GATE_LORA
R_LN_EPS = 64e-5

EVEN_IN = M_INNER + M_CONV_DIM + M_HEADS + R_SHIFT_DIM
EVEN_MIX = M_INNER + R_INNER

RET_QK_HEAD = 256
RET_V_HEAD = 512
RET_HEADS = D_MODEL // RET_QK_HEAD
RET_QK = RET_HEADS * RET_QK_HEAD
RET_V = RET_HEADS * RET_V_HEAD
ODD_IN = 2 * RET_QK + 2 * RET_V

D_FF = 7 * D_MODEL // 2
N_EXPERTS = 8
TOP_K = 2

ALPHA = (2.0 * DEPTH) ** 0.25
BETA = (8.0 * DEPTH) ** -0.25
LN_EPS = 1e-5

kernel_name = "hybrid_ssd_rwkv7_retention_moe_deepnorm"


def layer_norm(x, g, b, eps=LN_EPS):
    xf = x.astype(jnp.float32)
    mu = jnp.mean(xf, -1, keepdims=True)
    var = jnp.mean(jnp.square(xf - mu), -1, keepdims=True)
    return (xf - mu) * lax.rsqrt(var + eps) * g + b


def deepnorm(x, sub, g, b):
    return layer_norm(ALPHA * x + sub, g, b).astype(x.dtype)


def head_norm(y, eps):
    yf = y.astype(jnp.float32)
    mu = jnp.mean(yf, -1, keepdims=True)
    var = jnp.mean(jnp.square(yf - mu), -1, keepdims=True)
    return (yf - mu) * lax.rsqrt(var + eps)


def causal_depthwise_conv(u, w, bias):
    k = w.shape[0]
    out = lax.conv_general_dilated(u, w[:, None, :].astype(u.dtype), window_strides=(1,),
                                   padding=[(k - 1, 0)], dimension_numbers=('NWC', 'WIO', 'NWC'),
                                   feature_group_count=u.shape[-1])
    return out + bias


def to_chunks(a):
    b, t = a.shape[:2]
    return jnp.moveaxis(a.reshape(b, t // CHUNK, CHUNK, *a.shape[2:]), 1, 0)


def from_chunks(a):
    nc, b, l = a.shape[:3]
    return jnp.moveaxis(a, 0, 1).reshape(b, nc * l, *a.shape[3:])


def ssd_chunked(xs, dt, a, bm, cm):
    bsz, t, h, p = xs.shape
    g, n = bm.shape[2], bm.shape[3]
    hpg = h // g
    f32 = jnp.float32
    x_g = xs.astype(f32).reshape(bsz, t, g, hpg, p)
    dt_g = dt.astype(f32).reshape(bsz, t, g, hpg)
    da_g = (dt * a).astype(f32).reshape(bsz, t, g, hpg)
    causal = jnp.tril(jnp.ones((CHUNK, CHUNK), bool))[None, :, :, None, None]

    def step(state, inp):
        x_c, dt_c, da_c, b_c, c_c = inp
        acs = jnp.cumsum(da_c, axis=1)
        seg = acs[:, :, None] - acs[:, None, :]
        decay = jnp.exp(jnp.where(causal, seg, -jnp.inf))
        cb = jnp.einsum('bign,bjgn->bijg', c_c, b_c)
        m = cb[..., None] * decay * dt_c[:, None]
        y = jnp.einsum('bijgh,bjghp->bighp', m, x_c)
        y = y + jnp.einsum('bign,bghpn->bighp', c_c, state) * jnp.exp(acs)[..., None]
        w_end = jnp.exp(acs[:, -1:] - acs) * dt_c
        state = (state * jnp.exp(acs[:, -1])[..., None, None]
                 + jnp.einsum('bjgn,bjgh,bjghp->bghpn', b_c, w_end, x_c))
        return state, y

    inputs = (to_chunks(x_g), to_chunks(dt_g), to_chunks(da_g),
              to_chunks(bm.astype(f32)), to_chunks(cm.astype(f32)))
    _, ys = lax.scan(step, jnp.zeros((bsz, g, hpg, p, n), f32), inputs)
    return from_chunks(ys).reshape(bsz, t, h, p)


def rwkv7_scan(r, decay, k, v, a, b):
    bsz, t, h, n = r.shape

    def step(s, inp):
        r_t, w_t, k_t, v_t, a_t, b_t = inp
        sa = jnp.einsum('bhvk,bhk->bhv', s, a_t)
        s = s * w_t[:, :, None, :] + sa[..., None] * b_t[:, :, None, :] + v_t[..., None] * k_t[:, :, None, :]
        return s, jnp.einsum('bhvk,bhk->bhv', s, r_t)

    seq_first = tuple(jnp.moveaxis(u.astype(jnp.float32), 1, 0) for u in (r, decay, k, v, a, b))
    _, y = lax.scan(step, jnp.zeros((bsz, h, n, n), jnp.float32), seq_first)
    return jnp.moveaxis(y, 0, 1)


def xpos_rotate(u, positions):
    d = u.shape[-1]
    inv_freq = 1.0 / (10000.0 ** jnp.linspace(0.0, 1.0, d // 2, dtype=jnp.float32))
    ang = positions.astype(jnp.float32)[..., None] * inv_freq
    cos = jnp.repeat(jnp.cos(ang), 2, axis=-1)[:, :, None]
    sin = jnp.repeat(jnp.sin(ang), 2, axis=-1)[:, :, None]
    rot = jnp.stack([-u[..., 1::2], u[..., 0::2]], axis=-1).reshape(u.shape)
    return u * cos + rot * sin


def retention_chunkwise(q, k, v):
    f32 = jnp.float32
    log_gamma = jnp.log(1.0 - 2.0 ** (-5.0 - jnp.arange(RET_HEADS, dtype=f32)))
    idx = jnp.arange(CHUNK, dtype=f32)
    rel = idx[:, None] - idx[None, :]
    causal = rel >= 0
    intra = jnp.where(causal[None], jnp.exp(log_gamma[:, None, None] * jnp.where(causal, rel, 0.0)[None]), 0.0)
    q_decay = jnp.exp(log_gamma[None, :] * (idx + 1.0)[:, None])
    k_decay = jnp.exp(log_gamma[None, :] * (CHUNK - 1.0 - idx)[:, None])
    chunk_decay = jnp.exp(log_gamma * CHUNK)

    def step(state, inp):
        q_c, k_c, v_c = inp
        s = jnp.einsum('bihd,bjhd->bhij', q_c, k_c) * intra
        y = jnp.einsum('bhij,bjhv->bihv', s, v_c)
        y = y + jnp.einsum('bihd,bhdv->bihv', q_c, state) * q_decay[None, :, :, None]
        state = state * chunk_decay[None, :, None, None] + jnp.einsum('bjhd,jh,bjhv->bhdv', k_c, k_decay, v_c)
        return state, y

    bsz, _, h, dk = q.shape
    dv = v.shape[-1]
    inputs = (to_chunks(q.astype(f32)), to_chunks(k.astype(f32)), to_chunks(v.astype(f32)))
    _, ys = lax.scan(step, jnp.zeros((bsz, h, dk, dv), f32), inputs)
    return from_chunks(ys)


def swiglu(x, wg, wu, wd):
    return (jax.nn.silu(x @ wg) * (x @ wu)) @ wd


def moe_swiglu(x, w_router, wg, wu, wd):
    logits = (x @ w_router).astype(jnp.float32)
    top_val, top_idx = lax.top_k(logits, TOP_K)
    probs = jax.nn.softmax(top_val, axis=-1)
    gates = jnp.sum(jax.nn.one_hot(top_idx, N_EXPERTS, dtype=jnp.float32) * probs[..., None], axis=-2)
    out = gates[..., 0:1] * swiglu(x, wg[0], wu[0], wd[0])
    for e in range(1, N_EXPERTS):
        out = out + gates[..., e:e + 1] * swiglu(x, wg[e], wu[e], wd[e])
    return out


def even_mixer(x, w_in, conv_w, conv_b, dt_bias, a_log, d_skip, m_norm, mu_shift, w0, w_up, a0, a_up,
               g_up, k_k, k_a, r_k, rln_w, rln_b, w_out):
    bsz, t, _ = x.shape
    f32 = jnp.float32
    proj = x @ w_in
    z, xbc, dt_raw, rw = jnp.split(proj, [M_INNER, M_INNER + M_CONV_DIM, M_INNER + M_CONV_DIM + M_HEADS], axis=-1)

    xbc = jax.nn.silu(causal_depthwise_conv(xbc, conv_w, conv_b))
    xs, bm, cm = jnp.split(xbc, [M_INNER, M_INNER + M_GROUPS * M_STATE], axis=-1)
    xs = xs.reshape(bsz, t, M_HEADS, M_HEAD_DIM)
    dt = jax.nn.softplus((dt_raw + dt_bias).astype(f32))
    y = ssd_chunked(xs, dt, -jnp.exp(a_log.astype(f32)),
                    bm.reshape(bsz, t, M_GROUPS, M_STATE), cm.reshape(bsz, t, M_GROUPS, M_STATE))
    y = (y + d_skip[:, None] * xs) * jax.nn.silu(z).reshape(bsz, t, M_HEADS, M_HEAD_DIM)
    y = y.reshape(bsz, t, M_GROUPS, M_INNER // M_GROUPS).astype(f32)
    y_ssd = (y * lax.rsqrt(jnp.mean(jnp.square(y), -1, keepdims=True) + LN_EPS)).reshape(bsz, t, M_INNER) * m_norm

    prev = jnp.pad(rw, ((0, 0), (1, 0), (0, 0)))[:, :-1]
    rw = rw + (prev - rw) * mu_shift
    r, k, v, wd, ad, gd = jnp.split(rw, [R_INNER, 2 * R_INNER, 3 * R_INNER, 3 * R_INNER + R_DECAY_LORA,
                                         3 * R_INNER + R_DECAY_LORA + R_AAA_LORA], axis=-1)
    w = (w0 + jnp.tanh(wd) @ w_up).astype(f32)
    decay = jnp.exp(-jnp.exp(-jax.nn.softplus(-w) - 0.5))
    iclr = jax.nn.sigmoid(a0 + ad @ a_up)
    gate = jax.nn.sigmoid(gd) @ g_up

    def heads(u):
        return u.reshape(bsz, t, R_HEADS, R_HEAD_DIM).astype(f32)

    kk = heads(k * k_k)
    kk = kk / jnp.maximum(jnp.sqrt(jnp.sum(kk * kk, -1, keepdims=True)), 1e-12)
    k = k * (1.0 + (iclr - 1.0) * k_a)
    r_h, k_h, v_h, a_h = heads(r), heads(k), heads(v), heads(iclr)
    y = rwkv7_scan(r_h, heads(decay), k_h, v_h, -kk, kk * a_h)
    y = head_norm(y, R_LN_EPS).reshape(bsz, t, R_INNER) * rln_w + rln_b
    bonus = jnp.sum(r_h * k_h * r_k, -1, keepdims=True) * v_h
    y_rwkv = (y + bonus.reshape(bsz, t, R_INNER)) * gate

    return jnp.concatenate([y_ssd, y_rwkv.astype(y_ssd.dtype)], axis=-1) @ w_out


def odd_mixer(x, positions, w_in, w_out):
    bsz, t, _ = x.shape
    q, k, v, g = jnp.split(x @ w_in, [RET_QK, 2 * RET_QK, 2 * RET_QK + RET_V], axis=-1)
    q = xpos_rotate(q.reshape(bsz, t, RET_HEADS, RET_QK_HEAD), positions)
    k = xpos_rotate(k.reshape(bsz, t, RET_HEADS, RET_QK_HEAD), positions) * RET_QK_HEAD ** -0.5
    y = retention_chunkwise(q, k, v.reshape(bsz, t, RET_HEADS, RET_V_HEAD))
    y = head_norm(y, LN_EPS).reshape(bsz, t, RET_V)
    return (jax.nn.silu(g) * y) @ w_out


def setup_inputs(seed: int = 0) -> dict:
    key = jax.random.key(seed)
    ks = iter(jax.random.split(key, 64))
    f32 = jnp.float32
    E, O = N_EVEN, N_ODD

    def nrm(shape, scale):
        return jax.random.normal(next(ks), shape, f32) * scale

    def gain(shape):
        return 1.0 + nrm(shape, 0.02)

    x = nrm((BATCH, SEQ, D_MODEL), 1.0)
    offsets = jax.random.randint(next(ks), (BATCH, 1), 0, 1024, dtype=jnp.int32)
    positions = offsets + jnp.arange(SEQ, dtype=jnp.int32)[None, :]
    dt0 = jnp.exp(jax.random.uniform(next(ks), (E, M_HEADS), f32, float(np.log(1e-3)), float(np.log(1e-1))))
    dt_bias = dt0 + jnp.log(-jnp.expm1(-dt0))
    a_log = jnp.log(jax.random.uniform(next(ks), (E, M_HEADS), f32, 1.0, 16.0))
    ramp = jnp.arange(R_INNER, dtype=f32) / (R_INNER - 1)
    w0 = -7.0 + 5.0 * ramp ** 0.85 + 0.5 + nrm((E, R_INNER), 0.02)
    return {
        "x": x,
        "positions": positions,
        "ev_w_in": nrm((E, D_MODEL, EVEN_IN), D_MODEL ** -0.5),
        "ev_conv_w": nrm((E, M_CONV, M_CONV_DIM), M_CONV ** -0.5),
        "ev_conv_b": nrm((E, M_CONV_DIM), 0.02),
        "ev_dt_bias": dt_bias,
        "ev_a_log": a_log,
        "ev_d_skip": 1.0 + nrm((E, M_HEADS), 0.1),
        "ev_m_norm": gain((E, M_INNER)),
        "ev_mu_shift": jax.random.uniform(next(ks), (E, R_SHIFT_DIM), f32),
        "ev_w0": w0,
        "ev_w_up": nrm((E, R_DECAY_LORA, R_INNER), 0.1 * R_DECAY_LORA ** -0.5),
        "ev_a0": nrm((E, R_INNER), 0.1),
        "ev_a_up": nrm((E, R_AAA_LORA, R_INNER), R_AAA_LORA ** -0.5),
        "ev_g_up": nrm((E, R_GATE_LORA, R_INNER), R_GATE_LORA ** -0.5),
        "ev_k_k": 0.85 + nrm((E, R_INNER), 0.05),
        "ev_k_a": 1.0 + nrm((E, R_INNER), 0.05),
        "ev_r_k": nrm((E, R_HEADS, R_HEAD_DIM), 0.1),
        "ev_rln_w": gain((E, R_INNER)),
        "ev_rln_b": nrm((E, R_INNER), 0.02),
        "ev_w_out": nrm((E, EVEN_MIX, D_MODEL), BETA * EVEN_MIX ** -0.5),
        "ev_ln1_g": gain((E, D_MODEL)),
        "ev_ln1_b": nrm((E, D_MODEL), 0.02),
        "ev_ffn_wg": nrm((E, D_MODEL, D_FF), D_MODEL ** -0.5),
        "ev_ffn_wu": nrm((E, D_MODEL, D_FF), D_MODEL ** -0.5),
        "ev_ffn_wd": nrm((E, D_FF, D_MODEL), BETA * D_FF ** -0.5),
        "ev_ln2_g": gain((E, D_MODEL)),
        "ev_ln2_b": nrm((E, D_MODEL), 0.02),
        "od_w_in": nrm((O, D_MODEL, ODD_IN), D_MODEL ** -0.5),
        "od_w_out": nrm((O, RET_V, D_MODEL), BETA * RET_V ** -0.5),
        "od_ln1_g": gain((O, D_MODEL)),
        "od_ln1_b": nrm((O, D_MODEL), 0.02),
        "od_router": nrm((O, D_MODEL, N_EXPERTS), D_MODEL ** -0.5),
        "od_moe_wg": nrm((O, N_EXPERTS, D_MODEL, D_FF), D_MODEL ** -0.5),
        "od_moe_wu": nrm((O, N_EXPERTS, D_MODEL, D_FF), D_MODEL ** -0.5),
        "od_moe_wd": nrm((O, N_EXPERTS, D_FF, D_MODEL), BETA * D_FF ** -0.5),
        "od_ln2_g": gain((O, D_MODEL)),
        "od_ln2_b": nrm((O, D_MODEL), 0.02),
    }


def reference(x, positions, ev_w_in, ev_conv_w, ev_conv_b, ev_dt_bias, ev_a_log, ev_d_skip, ev_m_norm,
              ev_mu_shift, ev_w0, ev_w_up, ev_a0, ev_a_up, ev_g_up, ev_k_k, ev_k_a, ev_r_k, ev_rln_w,
              ev_rln_b, ev_w_out, ev_ln1_g, ev_ln1_b, ev_ffn_wg, ev_ffn_wu, ev_ffn_wd, ev_ln2_g, ev_ln2_b,
              od_w_in, od_w_out, od_ln1_g, od_ln1_b, od_router, od_moe_wg, od_moe_wu, od_moe_wd,
              od_ln2_g, od_ln2_b):
    for layer in range(DEPTH):
        i = layer // 2
        if layer % 2 == 0:
            h = even_mixer(x, ev_w_in[i], ev_conv_w[i], ev_conv_b[i], ev_dt_bias[i], ev_a_log[i], ev_d_skip[i],
                           ev_m_norm[i], ev_mu_shift[i], ev_w0[i], ev_w_up[i], ev_a0[i], ev_a_up[i], ev_g_up[i],
                           ev_k_k[i], ev_k_a[i], ev_r_k[i], ev_rln_w[i], ev_rln_b[i], ev_w_out[i])
            x = deepnorm(x, h, ev_ln1_g[i], ev_ln1_b[i])
            x = deepnorm(x, swiglu(x, ev_ffn_wg[i], ev_ffn_wu[i], ev_ffn_wd[i]), ev_ln2_g[i], ev_ln2_b[i])
        else:
            h = odd_mixer(x, positions, od_w_in[i], od_w_out[i])
            x = deepnorm(x, h, od_ln1_g[i], od_ln1_b[i])
            x = deepnorm(x, moe_swiglu(x, od_router[i], od_moe_wg[i], od_moe_wu[i], od_moe_wd[i]),
                         od_ln2_g[i], od_ln2_b[i])
    return x
```

```python
import functools

import jax
import jax.numpy as jnp
import numpy as np
from jax import lax
from jax.experimental import pallas as pl
from jax.experimental.pallas import tpu as pltpu

F32 = jnp.float32
BF16 = jnp.bfloat16

CHUNK = 128
M_HEAD_DIM = 64
M_GROUPS = 2
M_STATE = 128
R_HEAD_DIM = 64
R_DECAY_LORA = 64
R_AAA_LORA = 64
R_GATE_LORA = 160
R_LN_EPS = 64e-5
RET_HEADS = 4
RET_QK_HEAD = 256
RET_V_HEAD = 512
N_EXPERTS = 8
LN_EPS = 1e-5
DEPTH = 2
ALPHA = (2.0 * DEPTH) ** 0.25

VMEM_LIMIT_BYTES = 56 * 1024 * 1024


def _params(*sem):
    return pltpu.CompilerParams(dimension_semantics=sem, vmem_limit_bytes=VMEM_LIMIT_BYTES)


def _matmul_kernel(x_ref, w_ref, o_ref):
    o_ref[...] = jnp.dot(x_ref[...].astype(BF16), w_ref[...],
                         preferred_element_type=F32).astype(o_ref.dtype)


def matmul(x, w, *, tm=1024, tn=512, out_dtype=F32):
    n, k = x.shape
    m = w.shape[1]
    tm = min(tm, n)
    return pl.pallas_call(
        _matmul_kernel,
        grid=(n // tm, m // tn),
        in_specs=[pl.BlockSpec((tm, k), lambda i, j: (i, 0)),
                  pl.BlockSpec((k, tn), lambda i, j: (0, j))],
        out_specs=pl.BlockSpec((tm, tn), lambda i, j: (i, j)),
        out_shape=jax.ShapeDtypeStruct((n, m), out_dtype),
        compiler_params=_params("parallel", "arbitrary"),
        name="matmul",
    )(x, w)


def _deepnorm_rows(resid, sub, g, b):
    y = ALPHA * resid + sub
    mu = jnp.mean(y, axis=-1, keepdims=True)
    yc = y - mu
    var = jnp.mean(yc * yc, axis=-1, keepdims=True)
    return yc * lax.rsqrt(var + LN_EPS) * g + b


def _out_proj_kernel(a1_ref, a2_ref, w_ref, x_ref, g_ref, b_ref, o_ref):
    kh = a1_ref.shape[1]
    sub = jnp.dot(a1_ref[...].astype(BF16), w_ref[:kh, :], preferred_element_type=F32)
    sub = sub + jnp.dot(a2_ref[...].astype(BF16), w_ref[kh:, :], preferred_element_type=F32)
    o_ref[...] = _deepnorm_rows(x_ref[...], sub, g_ref[...], b_ref[...])


def out_proj_deepnorm(a1, a2, w, x, g, b, *, tm=512):
    n, d = x.shape
    kh = a1.shape[1]
    tm = min(tm, n)
    row = lambda i: (i, 0)
    fixed = lambda i: (0, 0)
    return pl.pallas_call(
        _out_proj_kernel,
        grid=(n // tm,),
        in_specs=[pl.BlockSpec((tm, kh), row), pl.BlockSpec((tm, kh), row),
                  pl.BlockSpec((2 * kh, d), fixed), pl.BlockSpec((tm, d), row),
                  pl.BlockSpec((1, d), fixed), pl.BlockSpec((1, d), fixed)],
        out_specs=pl.BlockSpec((tm, d), row),
        out_shape=jax.ShapeDtypeStruct((n, d), F32),
        compiler_params=_params("parallel"),
        name="out_proj_deepnorm",
    )(a1, a2, w, x, g, b)


def _moe_kernel(x_ref, gate_ref, wg_ref, wu_ref, wd_ref, g_ref, b_ref, o_ref, acc_ref, xb_ref):
    e = pl.program_id(1)
    f = pl.program_id(2)
    first = jnp.logical_and(e == 0, f == 0)
    last = jnp.logical_and(e == pl.num_programs(1) - 1, f == pl.num_programs(2) - 1)

    @pl.when(first)
    def _():
        acc_ref[...] = jnp.zeros_like(acc_ref)
        xb_ref[...] = x_ref[...].astype(BF16)

    xb = xb_ref[...]
    hg = jnp.dot(xb, wg_ref[...], preferred_element_type=F32)
    hu = jnp.dot(xb, wu_ref[...], preferred_element_type=F32)
    lane = lax.broadcasted_iota(jnp.int32, gate_ref.shape, 1)
    gate = jnp.sum(jnp.where(lane == e, gate_ref[...], 0.0), axis=-1, keepdims=True)
    h = (hg * jax.nn.sigmoid(hg)) * hu * gate
    acc_ref[...] += jnp.dot(h.astype(BF16), wd_ref[...], preferred_element_type=F32)

    @pl.when(last)
    def _():
        o_ref[...] = _deepnorm_rows(x_ref[...], acc_ref[...], g_ref[...], b_ref[...])


def moe_deepnorm(x, gates, wg, wu, wd, g, b, *, tm=1024, tf=512):
    n, d = x.shape
    ne, _, ff = wg.shape
    tm = min(tm, n)
    row = lambda i, e, f: (i, 0)
    fixed = lambda i, e, f: (0, 0)
    return pl.pallas_call(
        _moe_kernel,
        grid=(n // tm, ne, ff // tf),
        in_specs=[pl.BlockSpec((tm, d), row), pl.BlockSpec((tm, 128), row),
                  pl.BlockSpec((None, d, tf), lambda i, e, f: (e, 0, f)),
                  pl.BlockSpec((None, d, tf), lambda i, e, f: (e, 0, f)),
                  pl.BlockSpec((None, tf, d), lambda i, e, f: (e, f, 0)),
                  pl.BlockSpec((1, d), fixed), pl.BlockSpec((1, d), fixed)],
        out_specs=pl.BlockSpec((tm, d), row),
        out_shape=jax.ShapeDtypeStruct((n, d), F32),
        scratch_shapes=[pltpu.VMEM((tm, d), F32), pltpu.VMEM((tm, d), BF16)],
        compiler_params=_params("parallel", "arbitrary", "arbitrary"),
        name="moe_deepnorm",
    )(x, gates, wg, wu, wd, g, b)


def _ssd_kernel(x_ref, dt_ref, dtt_ref, a_ref, at_ref, bt_ref, c_ref, y_ref, state_ref):
    l = x_ref.shape[0]
    nh = dt_ref.shape[1]
    p = x_ref.shape[1] // nh
    ns = M_STATE
    hpg = nh // M_GROUPS

    @pl.when(pl.program_id(1) == 0)
    def _():
        state_ref[...] = jnp.zeros_like(state_ref)

    row = lax.broadcasted_iota(jnp.int32, (l, l), 0)
    col = lax.broadcasted_iota(jnp.int32, (l, l), 1)
    causal = row >= col
    tril = causal.astype(F32)
    triu = (row <= col).astype(F32)
    dt = dt_ref[...]
    dtt = dtt_ref[...]
    acs = jnp.dot(tril, dt * a_ref[...], preferred_element_type=F32, precision=lax.Precision.HIGHEST)
    acst = jnp.dot(dtt * at_ref[...], triu, preferred_element_type=F32, precision=lax.Precision.HIGHEST)
    acs_last = acs[l - 1:l, :]
    w_end = jnp.exp(acs_last - acs) * dt
    exp_acs = jnp.exp(acs)
    exp_last = jnp.exp(acs_last)

    for g in range(M_GROUPS):
        c_g = c_ref[:, g * ns:(g + 1) * ns].astype(BF16)
        bt_g = bt_ref[g * ns:(g + 1) * ns, :].astype(BF16)
        cb = jnp.dot(c_g, bt_g, preferred_element_type=F32)
        for hh in range(hpg):
            h = g * hpg + hh
            x_h = x_ref[:, h * p:(h + 1) * p]
            seg = acs[:, h:h + 1] - acst[h:h + 1, :]
            decay = jnp.where(causal, jnp.exp(seg), 0.0)
            m = cb * decay * dtt[h:h + 1, :]
            st = state_ref[h]
            y = jnp.dot(m.astype(BF16), x_h.astype(BF16), preferred_element_type=F32)
            y = y + jnp.dot(c_g, st.astype(BF16), preferred_element_type=F32) * exp_acs[:, h:h + 1]
            y_ref[:, h * p:(h + 1) * p] = y
            xw = (x_h * w_end[:, h:h + 1]).astype(BF16)
            state_ref[h] = st * exp_last[:, h:h + 1] + jnp.dot(bt_g, xw, preferred_element_type=F32)


def ssd_scan(xs, dt, dtt, a, bt, cm):
    bsz, t, hp = xs.shape
    nh = dt.shape[2]
    gn = cm.shape[2]
    l = CHUNK
    return pl.pallas_call(
        _ssd_kernel,
        grid=(bsz, t // l),
        in_specs=[pl.BlockSpec((None, l, hp), lambda b, c: (b, c, 0)),
                  pl.BlockSpec((None, l, nh), lambda b, c: (b, c, 0)),
                  pl.BlockSpec((None, nh, l), lambda b, c: (b, 0, c)),
                  pl.BlockSpec((1, nh), lambda b, c: (0, 0)),
                  pl.BlockSpec((nh, 1), lambda b, c: (0, 0)),
                  pl.BlockSpec((None, gn, l), lambda b, c: (b, 0, c)),
                  pl.BlockSpec((None, l, gn), lambda b, c: (b, c, 0))],
        out_specs=pl.BlockSpec((None, l, hp), lambda b, c: (b, c, 0)),
        out_shape=jax.ShapeDtypeStruct((bsz, t, hp), F32),
        scratch_shapes=[pltpu.VMEM((nh, M_STATE, hp // nh), F32)],
        compiler_params=_params("parallel", "arbitrary"),
        name="ssd_scan",
    )(xs, dt, dtt, a.reshape(1, nh), a.reshape(nh, 1), bt, cm)


RWKV_PARTIAL_SUMS = 8


def _tree_sum(parts):
    while len(parts) > 1:
        parts = [parts[i] + parts[i + 1] for i in range(0, len(parts), 2)]
    return parts[0]


def _rwkv_kernel(r_ref, w_ref, k_ref, v_ref, a_ref, b_ref, y_ref, s_ref):
    tb, c, bh = r_ref.shape
    lanes = 128
    sub = 8
    n_lane = bh // lanes
    n_vg = c // sub

    @pl.when(pl.program_id(0) == 0)
    def _():
        s_ref[...] = jnp.zeros_like(s_ref)

    def step(t, carry):
        def group(gi, carry2):
            half = gi // n_vg
            vg = gi % n_vg
            ls = pl.ds(pl.multiple_of(half * lanes, lanes), lanes)
            vs = pl.ds(pl.multiple_of(vg * sub, sub), sub)
            parts = [None] * RWKV_PARTIAL_SUMS
            for kk in range(c):
                term = s_ref[kk, vs, ls] * a_ref[t, pl.ds(kk, 1), ls]
                j = kk % RWKV_PARTIAL_SUMS
                parts[j] = term if parts[j] is None else parts[j] + term
            sa = _tree_sum(parts)
            vv = v_ref[t, vs, ls]
            parts = [None] * RWKV_PARTIAL_SUMS
            for kk in range(c):
                s_new = (s_ref[kk, vs, ls] * w_ref[t, pl.ds(kk, 1), ls]
                         + sa * b_ref[t, pl.ds(kk, 1), ls]
                         + vv * k_ref[t, pl.ds(kk, 1), ls])
                s_ref[kk, vs, ls] = s_new
                term = s_new * r_ref[t, pl.ds(kk, 1), ls]
                j = kk % RWKV_PARTIAL_SUMS
                parts[j] = term if parts[j] is None else parts[j] + term
            y_ref[t, vs, ls] = _tree_sum(parts)
            return carry2
        return lax.fori_loop(0, n_lane * n_vg, group, carry, unroll=2)

    lax.fori_loop(0, tb, step, 0)


def rwkv7_scan_t(r, w, k, v, a, b, *, tb=16):
    t, c, bh = r.shape
    spec = pl.BlockSpec((tb, c, bh), lambda i: (i, 0, 0))
    return pl.pallas_call(
        _rwkv_kernel,
        grid=(t // tb,),
        in_specs=[spec] * 6,
        out_specs=spec,
        out_shape=jax.ShapeDtypeStruct((t, c, bh), F32),
        scratch_shapes=[pltpu.VMEM((c, c, bh), F32)],
        compiler_params=_params("arbitrary"),
        name="rwkv7_scan",
    )(r, w, k, v, a, b)


def _retention_kernel(q_ref, k_ref, v_ref, y_ref, state_ref):
    l = q_ref.shape[0]
    dk = RET_QK_HEAD
    dv = RET_V_HEAD

    @pl.when(pl.program_id(1) == 0)
    def _():
        state_ref[...] = jnp.zeros_like(state_ref)

    row = lax.broadcasted_iota(jnp.int32, (l, l), 0)
    col = lax.broadcasted_iota(jnp.int32, (l, l), 1)
    rel = (row - col).astype(F32)
    causal = row >= col
    idx = lax.broadcasted_iota(jnp.int32, (l, 1), 0).astype(F32)
    for h in range(RET_HEADS):
        log_gamma = float(np.log(np.float32(1.0) - np.float32(2.0) ** np.float32(-5.0 - h)))
        intra = jnp.where(causal, jnp.exp(log_gamma * jnp.where(causal, rel, 0.0)), 0.0)
        q_decay = jnp.exp(log_gamma * (idx + 1.0))
        k_decay = jnp.exp(log_gamma * (l - 1.0 - idx))
        chunk_decay = float(np.exp(np.float32(log_gamma) * np.float32(l)))
        q_h = q_ref[:, h * dk:(h + 1) * dk].astype(BF16)
        k_h = k_ref[:, h * dk:(h + 1) * dk]
        v_h = v_ref[:, h * dv:(h + 1) * dv].astype(BF16)
        s = lax.dot_general(q_h, k_h.astype(BF16), (((1,), (1,)), ((), ())),
                            preferred_element_type=F32) * intra
        st = state_ref[h]
        y = jnp.dot(s.astype(BF16), v_h, preferred_element_type=F32)
        y = y + jnp.dot(q_h, st.astype(BF16), preferred_element_type=F32) * q_decay
        y_ref[:, h * dv:(h + 1) * dv] = y
        kd = (k_h * k_decay).astype(BF16)
        state_ref[h] = st * chunk_decay + lax.dot_general(
            kd, v_h, (((0,), (0,)), ((), ())), preferred_element_type=F32)


def retention_scan(q, k, v):
    bsz, t, hk = q.shape
    hv = v.shape[2]
    l = CHUNK
    return pl.pallas_call(
        _retention_kernel,
        grid=(bsz, t // l),
        in_specs=[pl.BlockSpec((None, l, hk), lambda b, c: (b, c, 0)),
                  pl.BlockSpec((None, l, hk), lambda b, c: (b, c, 0)),
                  pl.BlockSpec((None, l, hv), lambda b, c: (b, c, 0))],
        out_specs=pl.BlockSpec((None, l, hv), lambda b, c: (b, c, 0)),
        out_shape=jax.ShapeDtypeStruct((bsz, t, hv), F32),
        scratch_shapes=[pltpu.VMEM((RET_HEADS, RET_QK_HEAD, RET_V_HEAD), F32)],
        compiler_params=_params("parallel", "arbitrary"),
        name="retention_scan",
    )(q, k, v)


def _shift_mix(u, mu):
    prev = jnp.pad(u, ((0, 0), (1, 0), (0, 0)))[:, :-1]
    return u + (prev - u) * mu


def _head_norm(y, eps):
    mu = jnp.mean(y, -1, keepdims=True)
    var = jnp.mean(jnp.square(y - mu), -1, keepdims=True)
    return (y - mu) * lax.rsqrt(var + eps)


def _even_layer(x, w_in, conv_w, conv_b, dt_bias, a_log, d_skip, m_norm, mu_shift, w0, w_up, a0, a_up,
                g_up, k_k, k_a, r_k, rln_w, rln_b, w_out, ln1_g, ln1_b, wg, wu, wd, ln2_g, ln2_b):
    bsz, t, d = x.shape
    n = bsz * t
    m_inner = m_norm.shape[0]
    m_heads = dt_bias.shape[0]
    r_inner = w0.shape[0]
    r_heads = r_inner // R_HEAD_DIM
    gn = M_GROUPS * M_STATE
    conv_dim = m_inner + 2 * gn
    o_xbc = m_inner
    o_dt = o_xbc + conv_dim
    o_rw = o_dt + m_heads
    x2 = x.reshape(n, d)

    pad = (-w_in.shape[1]) % 512
    proj = matmul(x2, jnp.pad(w_in, ((0, 0), (0, pad))).astype(BF16)).reshape(bsz, t, -1)
    z = proj[..., :m_inner]
    xbc = proj[..., o_xbc:o_dt]
    dt_raw = proj[..., o_dt:o_rw]
    rw = proj[..., o_rw:w_in.shape[1]]

    k_conv = conv_w.shape[0]
    xpad = jnp.pad(xbc, ((0, 0), (k_conv - 1, 0), (0, 0)))
    conv = conv_b + sum(conv_w[i] * xpad[:, i:i + t] for i in range(k_conv))
    xbc = jax.nn.silu(conv)
    xs = xbc[..., :m_inner]
    bm = xbc[..., m_inner:m_inner + gn]
    cm = xbc[..., m_inner + gn:]
    dt = jax.nn.softplus(dt_raw + dt_bias)
    a = -jnp.exp(a_log)
    y = ssd_scan(xs, dt, jnp.swapaxes(dt, 1, 2), a, jnp.swapaxes(bm, 1, 2), cm)
    y = y.reshape(bsz, t, m_heads, M_HEAD_DIM)
    xs_h = xs.reshape(bsz, t, m_heads, M_HEAD_DIM)
    y = (y + d_skip[:, None] * xs_h) * jax.nn.silu(z).reshape(bsz, t, m_heads, M_HEAD_DIM)
    y = y.reshape(bsz, t, M_GROUPS, m_inner // M_GROUPS)
    y_ssd = (y * lax.rsqrt(jnp.mean(jnp.square(y), -1, keepdims=True) + LN_EPS)).reshape(bsz, t, m_inner) * m_norm

    rw = _shift_mix(rw, mu_shift)
    o1, o2, o3 = r_inner, 2 * r_inner, 3 * r_inner
    o4 = o3 + R_DECAY_LORA
    o5 = o4 + R_AAA_LORA
    r, k, v, wdl, adl, gdl = rw[..., :o1], rw[..., o1:o2], rw[..., o2:o3], rw[..., o3:o4], rw[..., o4:o5], rw[..., o5:]
    hi = lax.Precision.HIGHEST
    w = w0 + jnp.dot(jnp.tanh(wdl), w_up, precision=hi)
    decay = jnp.exp(-jnp.exp(-jax.nn.softplus(-w) - 0.5))
    iclr = jax.nn.sigmoid(a0 + jnp.dot(adl, a_up, precision=hi))
    gate = jnp.dot(jax.nn.sigmoid(gdl), g_up, precision=hi)

    def heads(u):
        return u.reshape(bsz, t, r_heads, R_HEAD_DIM)

    kk = heads(k * k_k)
    kk = kk / jnp.maximum(jnp.sqrt(jnp.sum(kk * kk, -1, keepdims=True)), 1e-12)
    k = k * (1.0 + (iclr - 1.0) * k_a)
    r_h, k_h, v_h, a_h = heads(r), heads(k), heads(v), heads(iclr)

    def to_t(u):
        return jnp.transpose(u, (1, 3, 0, 2)).reshape(t, R_HEAD_DIM, bsz * r_heads)

    y_t = rwkv7_scan_t(to_t(r_h), to_t(heads(decay)), to_t(k_h), to_t(v_h), to_t(-kk), to_t(kk * a_h))
    y = jnp.transpose(y_t.reshape(t, R_HEAD_DIM, bsz, r_heads), (2, 0, 3, 1))
    y = _head_norm(y, R_LN_EPS).reshape(bsz, t, r_inner) * rln_w + rln_b
    bonus = jnp.sum(r_h * k_h * r_k, -1, keepdims=True) * v_h
    y_rwkv = (y + bonus.reshape(bsz, t, r_inner)) * gate

    x2 = out_proj_deepnorm(y_ssd.reshape(n, m_inner), y_rwkv.reshape(n, r_inner), w_out.astype(BF16),
                           x2, ln1_g.reshape(1, d), ln1_b.reshape(1, d))
    gates = jnp.ones((n, 128), F32)
    x2 = moe_deepnorm(x2, gates, wg.astype(BF16)[None], wu.astype(BF16)[None], wd.astype(BF16)[None],
                      ln2_g.reshape(1, d), ln2_b.reshape(1, d))
    return x2.reshape(bsz, t, d)


def _xpos_rotate(u, positions):
    dh = u.shape[-1]
    inv_freq = 1.0 / (10000.0 ** jnp.linspace(0.0, 1.0, dh // 2, dtype=F32))
    ang = positions.astype(F32)[..., None] * inv_freq
    cos = jnp.repeat(jnp.cos(ang), 2, axis=-1)[:, :, None]
    sin = jnp.repeat(jnp.sin(ang), 2, axis=-1)[:, :, None]
    rot = jnp.stack([-u[..., 1::2], u[..., 0::2]], axis=-1).reshape(u.shape)
    return u * cos + rot * sin


def _odd_layer(x, positions, w_in, w_out, ln1_g, ln1_b, router, wg, wu, wd, ln2_g, ln2_b):
    bsz, t, d = x.shape
    n = bsz * t
    ret_qk = RET_HEADS * RET_QK_HEAD
    ret_v = RET_HEADS * RET_V_HEAD
    x2 = x.reshape(n, d)
    proj = matmul(x2, w_in.astype(BF16)).reshape(bsz, t, -1)
    q = proj[..., :ret_qk]
    k = proj[..., ret_qk:2 * ret_qk]
    v = proj[..., 2 * ret_qk:2 * ret_qk + ret_v]
    g = proj[..., 2 * ret_qk + ret_v:]
    q = _xpos_rotate(q.reshape(bsz, t, RET_HEADS, RET_QK_HEAD), positions).reshape(bsz, t, ret_qk)
    k = (_xpos_rotate(k.reshape(bsz, t, RET_HEADS, RET_QK_HEAD), positions)
         * RET_QK_HEAD ** -0.5).reshape(bsz, t, ret_qk)
    y = retention_scan(q, k, v)
    y = _head_norm(y.reshape(bsz, t, RET_HEADS, RET_V_HEAD), LN_EPS).reshape(bsz, t, ret_v)
    a = (jax.nn.silu(g) * y).reshape(n, ret_v)
    x2 = out_proj_deepnorm(a[:, :ret_v // 2], a[:, ret_v // 2:], w_out.astype(BF16), x2,
                           ln1_g.reshape(1, d), ln1_b.reshape(1, d))

    logits = jnp.dot(x2, router, precision=lax.Precision.HIGHEST)
    top_val, top_idx = lax.top_k(logits, 2)
    probs = jax.nn.softmax(top_val, axis=-1)
    gates = jnp.sum(jax.nn.one_hot(top_idx, 128, dtype=F32) * probs[..., None], axis=-2)
    x2 = moe_deepnorm(x2, gates, wg.astype(BF16), wu.astype(BF16), wd.astype(BF16),
                      ln2_g.reshape(1, d), ln2_b.reshape(1, d))
    return x2.reshape(bsz, t, d)


def kernel(x, positions, ev_w_in, ev_conv_w, ev_conv_b, ev_dt_bias, ev_a_log, ev_d_skip, ev_m_norm, ev_mu_shift, ev_w0, ev_w_up, ev_a0, ev_a_up, ev_g_up, ev_k_k, ev_k_a, ev_r_k, ev_rln_w, ev_rln_b, ev_w_out, ev_ln1_g, ev_ln1_b, ev_ffn_wg, ev_ffn_wu, ev_ffn_wd, ev_ln2_g, ev_ln2_b, od_w_in, od_w_out, od_ln1_g, od_ln1_b, od_router, od_moe_wg, od_moe_wu, od_moe_wd, od_ln2_g, od_ln2_b):
    for layer in range(DEPTH):
        i = layer // 2
        if layer % 2 == 0:
            x = _even_layer(x, ev_w_in[i], ev_conv_w[i], ev_conv_b[i], ev_dt_bias[i], ev_a_log[i], ev_d_skip[i],
                            ev_m_norm[i], ev_mu_shift[i], ev_w0[i], ev_w_up[i], ev_a0[i], ev_a_up[i], ev_g_up[i],
                            ev_k_k[i], ev_k_a[i], ev_r_k[i], ev_rln_w[i], ev_rln_b[i], ev_w_out[i],
                            ev_ln1_g[i], ev_ln1_b[i], ev_ffn_wg[i], ev_ffn_wu[i], ev_ffn_wd[i],
                            ev_ln2_g[i], ev_ln2_b[i])
        else:
            x = _odd_layer(x, positions, od_w_in[i], od_w_out[i], od_ln1_g[i], od_ln1_b[i], od_router[i],
                           od_moe_wg[i], od_moe_wu[i], od_moe_wd[i], od_ln2_g[i], od_ln2_b[i])
    return x
```

```python
import jax
import jax.numpy as jnp
import numpy as np
from jax import lax
from jax.experimental import pallas as pl
from jax.experimental.pallas import tpu as pltpu

F32 = jnp.float32
BF16 = jnp.bfloat16
HIGHEST = lax.Precision.HIGHEST

CHUNK = 128
M_HEAD_DIM = 64
M_GROUPS = 2
M_STATE = 128
R_HEAD_DIM = 64
R_DECAY_LORA = 64
R_AAA_LORA = 64
R_GATE_LORA = 160
R_LN_EPS = 64e-5
RET_HEADS = 4
RET_QK_HEAD = 256
RET_V_HEAD = 512
N_EXPERTS = 8
LN_EPS = 1e-5
DEPTH = 2
ALPHA = (2.0 * DEPTH) ** 0.25

LANES = 128
SUBLANES = 8
MXU_DIM = 256
VMEM_LIMIT_BYTES = 56 * 1024 * 1024

LORA_BLOCK = 512
LORA_GATE_OFF = 128
LORA_GATE_PAD = 256
LORA_DT_OFF = 384


def _params(*sem):
    return pltpu.CompilerParams(dimension_semantics=sem, vmem_limit_bytes=VMEM_LIMIT_BYTES)


def _sigmoid(x):
    return jax.nn.sigmoid(x)


def _matmul_kernel(x_ref, w_ref, o_ref):
    o_ref[...] = jnp.dot(x_ref[...].astype(BF16), w_ref[...],
                         preferred_element_type=F32).astype(o_ref.dtype)


def matmul(x, w, *, tm=1024, tn=512, out_dtype=F32):
    n, k = x.shape
    m = w.shape[1]
    tm = min(tm, n)
    return pl.pallas_call(
        _matmul_kernel,
        grid=(n // tm, m // tn),
        in_specs=[pl.BlockSpec((tm, k), lambda i, j: (i, 0)),
                  pl.BlockSpec((k, tn), lambda i, j: (0, j))],
        out_specs=pl.BlockSpec((tm, tn), lambda i, j: (i, j)),
        out_shape=jax.ShapeDtypeStruct((n, m), out_dtype),
        compiler_params=_params("parallel", "arbitrary"),
        name="matmul",
    )(x, w)


def _deepnorm_rows(resid, sub, g, b):
    y = ALPHA * resid + sub
    mu = jnp.mean(y, axis=-1, keepdims=True)
    yc = y - mu
    var = jnp.mean(yc * yc, axis=-1, keepdims=True)
    return yc * lax.rsqrt(var + LN_EPS) * g + b


def _out_proj_kernel(a1_ref, a2_ref, w_ref, x_ref, g_ref, b_ref, o_ref):
    kh = a1_ref.shape[1]
    sub = jnp.dot(a1_ref[...], w_ref[:kh, :], preferred_element_type=F32)
    sub = sub + jnp.dot(a2_ref[...], w_ref[kh:, :], preferred_element_type=F32)
    o_ref[...] = _deepnorm_rows(x_ref[...], sub, g_ref[...], b_ref[...])


def out_proj_deepnorm(a, w, x, g, b, *, tm=512):
    n, d = x.shape
    kh = a.shape[1] // 2
    tm = min(tm, n)
    row = lambda i: (i, 0)
    fixed = lambda i: (0, 0)
    return pl.pallas_call(
        _out_proj_kernel,
        grid=(n // tm,),
        in_specs=[pl.BlockSpec((tm, kh), row), pl.BlockSpec((tm, kh), lambda i: (i, 1)),
                  pl.BlockSpec((2 * kh, d), fixed), pl.BlockSpec((tm, d), row),
                  pl.BlockSpec((1, d), fixed), pl.BlockSpec((1, d), fixed)],
        out_specs=pl.BlockSpec((tm, d), row),
        out_shape=jax.ShapeDtypeStruct((n, d), F32),
        compiler_params=_params("parallel"),
        name="out_proj_deepnorm",
    )(a, a, w, x, g, b)


def _even_out_kernel(ys_ref, yn_ref, bonus_ref, gate_ref, rlnw_ref, rlnb_ref, w_ref, x_ref, g_ref, b_ref, o_ref):
    kh = ys_ref.shape[1]
    y_rwkv = ((yn_ref[...] * rlnw_ref[...] + rlnb_ref[...]) + bonus_ref[...]) * gate_ref[...]
    sub = jnp.dot(ys_ref[...], w_ref[:kh, :], preferred_element_type=F32)
    sub = sub + jnp.dot(y_rwkv.astype(BF16), w_ref[kh:, :], preferred_element_type=F32)
    o_ref[...] = _deepnorm_rows(x_ref[...], sub, g_ref[...], b_ref[...])


def even_out_deepnorm(y_ssd, y_norm, bonus, gate, rln_w, rln_b, w, x, g, b, *, tm=512):
    n, d = x.shape
    kh = y_ssd.shape[1]
    tm = min(tm, n)
    row = lambda i: (i, 0)
    fixed = lambda i: (0, 0)
    return pl.pallas_call(
        _even_out_kernel,
        grid=(n // tm,),
        in_specs=[pl.BlockSpec((tm, kh), row)] * 4
        + [pl.BlockSpec((1, kh), fixed), pl.BlockSpec((1, kh), fixed),
           pl.BlockSpec((2 * kh, d), fixed), pl.BlockSpec((tm, d), row),
           pl.BlockSpec((1, d), fixed), pl.BlockSpec((1, d), fixed)],
        out_specs=pl.BlockSpec((tm, d), row),
        out_shape=jax.ShapeDtypeStruct((n, d), F32),
        compiler_params=_params("parallel"),
        name="even_out_deepnorm",
    )(y_ssd, y_norm, bonus, gate, rln_w, rln_b, w, x, g, b)


def _router_kernel(x_ref, wr_ref, gates_ref):
    logits = jnp.dot(x_ref[...], wr_ref[...], preferred_element_type=F32, precision=HIGHEST)
    lane = lax.broadcasted_iota(jnp.int32, logits.shape, 1)
    neg = jnp.float32(-jnp.inf)
    lg = jnp.where(lane < N_EXPERTS, logits, neg)
    m1 = jnp.max(lg, axis=-1, keepdims=True)
    i1 = jnp.min(jnp.where(lg == m1, lane, LANES), axis=-1, keepdims=True)
    lg2 = jnp.where(lane == i1, neg, lg)
    m2 = jnp.max(lg2, axis=-1, keepdims=True)
    i2 = jnp.min(jnp.where(lg2 == m2, lane, LANES), axis=-1, keepdims=True)
    e2 = jnp.exp(m2 - m1)
    p1 = 1.0 / (1.0 + e2)
    p2 = e2 / (1.0 + e2)
    gates_ref[...] = jnp.where(lane == i1, p1, 0.0) + jnp.where(lane == i2, p2, 0.0)


def router_gates(x, w_router, *, tm=1024):
    n, d = x.shape
    tm = min(tm, n)
    return pl.pallas_call(
        _router_kernel,
        grid=(n // tm,),
        in_specs=[pl.BlockSpec((tm, d), lambda i: (i, 0)), pl.BlockSpec((d, LANES), lambda i: (0, 0))],
        out_specs=pl.BlockSpec((tm, LANES), lambda i: (i, 0)),
        out_shape=jax.ShapeDtypeStruct((n, LANES), F32),
        compiler_params=_params("parallel"),
        name="router_gates",
    )(x, w_router)


def _moe_kernel(x_ref, gate_ref, wg_ref, wu_ref, wd_ref, g_ref, b_ref, o_ref, acc_ref, xb_ref):
    e = pl.program_id(1)
    f = pl.program_id(2)
    first = jnp.logical_and(e == 0, f == 0)
    last = jnp.logical_and(e == pl.num_programs(1) - 1, f == pl.num_programs(2) - 1)

    @pl.when(first)
    def _():
        acc_ref[...] = jnp.zeros_like(acc_ref)
        xb_ref[...] = x_ref[...].astype(BF16)

    xb = xb_ref[...]
    hg = jnp.dot(xb, wg_ref[...], preferred_element_type=F32)
    hu = jnp.dot(xb, wu_ref[...], preferred_element_type=F32)
    lane = lax.broadcasted_iota(jnp.int32, gate_ref.shape, 1)
    gate = jnp.sum(jnp.where(lane == e, gate_ref[...], 0.0), axis=-1, keepdims=True)
    h = (hg * _sigmoid(hg)) * hu * gate
    acc_ref[...] += jnp.dot(h.astype(BF16), wd_ref[...], preferred_element_type=F32)

    @pl.when(last)
    def _():
        o_ref[...] = _deepnorm_rows(x_ref[...], acc_ref[...], g_ref[...], b_ref[...])


def moe_deepnorm(x, gates, wg, wu, wd, g, b, *, tm=1024, tf=512):
    n, d = x.shape
    ne, _, ff = wg.shape
    tm = min(tm, n)
    row = lambda i, e, f: (i, 0)
    fixed = lambda i, e, f: (0, 0)
    return pl.pallas_call(
        _moe_kernel,
        grid=(n // tm, ne, ff // tf),
        in_specs=[pl.BlockSpec((tm, d), row), pl.BlockSpec((tm, LANES), row),
                  pl.BlockSpec((None, d, tf), lambda i, e, f: (e, 0, f)),
                  pl.BlockSpec((None, d, tf), lambda i, e, f: (e, 0, f)),
                  pl.BlockSpec((None, tf, d), lambda i, e, f: (e, f, 0)),
                  pl.BlockSpec((1, d), fixed), pl.BlockSpec((1, d), fixed)],
        out_specs=pl.BlockSpec((tm, d), row),
        out_shape=jax.ShapeDtypeStruct((n, d), F32),
        scratch_shapes=[pltpu.VMEM((tm, d), F32), pltpu.VMEM((tm, d), BF16)],
        compiler_params=_params("parallel", "arbitrary", "arbitrary"),
        name="moe_deepnorm",
    )(x, gates, wg, wu, wd, g, b)


CONV_HISTORY = SUBLANES


def _ssd_kernel(xs_ref, b_ref, c_ref, z_ref, dt_ref, cw_ref, cb_ref, dtb_ref, alog_ref, dskip_ref, mnorm_ref,
                y_ref, state_ref, ext_ref, yacc_ref):
    l, hp = xs_ref.shape
    gn = b_ref.shape[1]
    p = M_HEAD_DIM
    ns = M_STATE
    nh = hp // p
    hpg = nh // M_GROUPS
    kc = cw_ref.shape[0]
    hist = CONV_HISTORY

    @pl.when(pl.program_id(1) == 0)
    def _():
        state_ref[...] = jnp.zeros_like(state_ref)
        ext_ref[0:hist, :] = jnp.zeros((hist, ext_ref.shape[1]), F32)

    ext_ref[hist:, 0:hp] = xs_ref[...]
    ext_ref[hist:, hp:hp + gn] = b_ref[...]
    ext_ref[hist:, hp + gn:] = c_ref[...]
    conv = cb_ref[...]
    for i in range(kc):
        conv = conv + cw_ref[i:i + 1, :] * ext_ref[pl.ds(hist - (kc - 1) + i, l), :]
    ext_ref[0:hist, :] = ext_ref[l:l + hist, :]
    xbc = conv * _sigmoid(conv)
    xs = xbc[:, :hp]

    dt_pre = dt_ref[...] + dtb_ref[...]
    dt = jnp.maximum(dt_pre, 0.0) + jnp.log(1.0 + jnp.exp(-jnp.abs(dt_pre)))
    a = -jnp.exp(alog_ref[...])

    row = lax.broadcasted_iota(jnp.int32, (l, l), 0)
    col = lax.broadcasted_iota(jnp.int32, (l, l), 1)
    causal = row >= col
    acs = jnp.dot(causal.astype(F32), dt * a, preferred_element_type=F32, precision=HIGHEST)
    acst = acs.T
    dtt = dt.T
    acs_last = acs[l - 1:l, :]
    w_end = jnp.exp(acs_last - acs) * dt
    exp_acs = jnp.exp(acs)
    exp_last = jnp.exp(acs_last)

    for g in range(M_GROUPS):
        b_g = xbc[:, hp + g * ns:hp + (g + 1) * ns]
        c_g = xbc[:, hp + gn + g * ns:hp + gn + (g + 1) * ns].astype(BF16)
        bt_g = b_g.T.astype(BF16)
        cb = jnp.dot(c_g, bt_g, preferred_element_type=F32)
        for hh in range(hpg):
            h = g * hpg + hh
            x_h = xs[:, h * p:(h + 1) * p]
            seg = acs[:, h:h + 1] - acst[h:h + 1, :]
            decay = jnp.where(causal, jnp.exp(seg), 0.0)
            m = cb * decay * dtt[h:h + 1, :]
            st = state_ref[h]
            y = jnp.dot(m.astype(BF16), x_h.astype(BF16), preferred_element_type=F32)
            y = y + jnp.dot(c_g, st.astype(BF16), preferred_element_type=F32) * exp_acs[:, h:h + 1]
            yacc_ref[:, h * p:(h + 1) * p] = y
            xw = (x_h * w_end[:, h:h + 1]).astype(BF16)
            state_ref[h] = st * exp_last[:, h:h + 1] + jnp.dot(bt_g, xw, preferred_element_type=F32)

    z = z_ref[...]
    y = (yacc_ref[...] + dskip_ref[...] * xs) * (z * _sigmoid(z))
    gw = hp // M_GROUPS
    for g in range(M_GROUPS):
        seg = y[:, g * gw:(g + 1) * gw]
        ms = jnp.mean(seg * seg, axis=-1, keepdims=True)
        y_ref[:, g * gw:(g + 1) * gw] = (seg * lax.rsqrt(ms + LN_EPS)
                                         * mnorm_ref[:, g * gw:(g + 1) * gw]).astype(y_ref.dtype)


def ssd_mixer(proj, cols, conv_w, conv_b, dt_bias, a_log, d_skip, m_norm):
    bsz, t, _ = proj.shape
    hp = m_norm.shape[1]
    gn = M_GROUPS * M_STATE
    conv_dim = hp + 2 * gn
    nh = hp // M_HEAD_DIM
    l = CHUNK

    def col_block(width, off):
        blk = off // width
        return pl.BlockSpec((None, l, width), lambda b, c: (b, c, blk))

    fixed = lambda b, c: (0, 0)
    return pl.pallas_call(
        _ssd_kernel,
        grid=(bsz, t // l),
        in_specs=[col_block(hp, cols["xs"]), col_block(gn, cols["B"]), col_block(gn, cols["C"]),
                  col_block(hp, cols["z"]), col_block(LANES, cols["dt"]),
                  pl.BlockSpec(conv_w.shape, fixed), pl.BlockSpec((1, conv_dim), fixed),
                  pl.BlockSpec((1, LANES), fixed), pl.BlockSpec((1, LANES), fixed),
                  pl.BlockSpec((1, hp), fixed), pl.BlockSpec((1, hp), fixed)],
        out_specs=pl.BlockSpec((None, l, hp), lambda b, c: (b, c, 0)),
        out_shape=jax.ShapeDtypeStruct((bsz, t, hp), BF16),
        scratch_shapes=[pltpu.VMEM((nh, M_STATE, M_HEAD_DIM), F32),
                        pltpu.VMEM((CONV_HISTORY + l, conv_dim), F32),
                        pltpu.VMEM((l, hp), F32)],
        compiler_params=_params("parallel", "arbitrary"),
        name="ssd_mixer",
    )(proj, proj, proj, proj, proj, conv_w, conv_b, dt_bias, a_log, d_skip, m_norm)


RWKV_PARTIAL_SUMS = 8
EXP_MINUS_HALF = float(np.exp(-0.5))


def _tree_sum(parts):
    while len(parts) > 1:
        parts = [parts[i] + parts[i + 1] for i in range(0, len(parts), 2)]
    return parts[0]


def _head_sum(x, ones):
    hi = x.astype(BF16)
    lo = (x - hi.astype(F32)).astype(BF16)
    outs = []
    for j in range(x.shape[1] // MXU_DIM):
        sl = slice(j * MXU_DIM, (j + 1) * MXU_DIM)
        outs.append(jnp.dot(hi[:, sl], ones, preferred_element_type=F32)
                    + jnp.dot(lo[:, sl], ones, preferred_element_type=F32))
    return jnp.concatenate(outs, axis=-1)


def _rwkv_prep_kernel(r_ref, k_ref, v_ref, lo_ref, mu_ref, w0_ref, a0_ref, kk_ref, ka_ref, rk_ref,
                      wup_ref, aup_ref, gup_ref, ones_ref,
                      ro_ref, wo_ref, ko_ref, vo_ref, ao_ref, bo_ref, bonus_ref, gate_ref, prev_ref):
    tl, c = r_ref.shape

    @pl.when(pl.program_id(1) == 0)
    def _():
        prev_ref[...] = jnp.zeros_like(prev_ref)

    first_row = lax.broadcasted_iota(jnp.int32, (tl, 1), 0) == 0

    def shift_mix(x_ref, off):
        x = x_ref[...]
        width = x.shape[1]
        prev = jnp.where(first_row, prev_ref[0:1, off:off + width], pltpu.roll(x, 1, 0))
        prev_ref[0:1, off:off + width] = x[tl - 1:tl, :]
        return x + (prev - x) * mu_ref[:, off:off + width]

    r = shift_mix(r_ref, 0)
    k = shift_mix(k_ref, c)
    v = shift_mix(v_ref, 2 * c)
    lo = shift_mix(lo_ref, 3 * c)
    lo_a = lo[:, 0:LANES]
    lo_g = lo[:, LORA_GATE_OFF:LORA_GATE_OFF + LORA_GATE_PAD]
    w = w0_ref[...] + jnp.dot(jnp.tanh(lo_a).astype(BF16), wup_ref[...], preferred_element_type=F32)
    decay = jnp.exp(-EXP_MINUS_HALF * _sigmoid(w))
    iclr = _sigmoid(a0_ref[...] + jnp.dot(lo_a.astype(BF16), aup_ref[...], preferred_element_type=F32))
    gate = jnp.dot(_sigmoid(lo_g).astype(BF16), gup_ref[...], preferred_element_type=F32)
    ones = ones_ref[...]
    kk = k * kk_ref[...]
    kk = kk / jnp.maximum(jnp.sqrt(_head_sum(kk * kk, ones)), 1e-12)
    k2 = k * (1.0 + (iclr - 1.0) * ka_ref[...])
    ro_ref[...] = r
    wo_ref[...] = decay
    ko_ref[...] = k2
    vo_ref[...] = v
    ao_ref[...] = -kk
    bo_ref[...] = kk * iclr
    bonus_ref[...] = _head_sum(r * k2 * rk_ref[...], ones) * v
    gate_ref[...] = gate


def rwkv_prep(proj, cols, mu, w0, a0, k_k, k_a, r_k, w_up, a_up, g_up, *, tl=256):
    bsz, t, _ = proj.shape
    c = w0.shape[1]
    tl = min(tl, t)
    ones = jnp.asarray(np.kron(np.eye(MXU_DIM // R_HEAD_DIM), np.ones((R_HEAD_DIM, R_HEAD_DIM))), BF16)

    def col_block(width, off):
        blk = off // width
        return pl.BlockSpec((None, tl, width), lambda b, i: (b, i, blk))

    def fixed(shape):
        return pl.BlockSpec(shape, lambda b, i: (0, 0))

    out_spec = pl.BlockSpec((None, tl, c), lambda b, i: (b, i, 0))
    return pl.pallas_call(
        _rwkv_prep_kernel,
        grid=(bsz, t // tl),
        in_specs=[col_block(c, cols["r"]), col_block(c, cols["k"]), col_block(c, cols["v"]),
                  col_block(LORA_BLOCK, cols["lora"]), fixed(mu.shape)]
        + [fixed((1, c))] * 5 + [fixed(w_up.shape), fixed(a_up.shape), fixed(g_up.shape), fixed(ones.shape)],
        out_specs=[out_spec] * 8,
        out_shape=[jax.ShapeDtypeStruct((bsz, t, c), F32)] * 8,
        scratch_shapes=[pltpu.VMEM((SUBLANES, mu.shape[1]), F32)],
        compiler_params=_params("parallel", "arbitrary"),
        name="rwkv_prep",
    )(proj, proj, proj, proj, mu, w0, a0, k_k, k_a, r_k, w_up, a_up, g_up, ones)


def _rwkv_kernel(r_ref, w_ref, k_ref, v_ref, a_ref, b_ref, y_ref, s_ref):
    tb, c, bh = r_ref.shape
    n_lane = bh // LANES
    n_vg = c // SUBLANES

    @pl.when(pl.program_id(0) == 0)
    def _():
        s_ref[...] = jnp.zeros_like(s_ref)

    def step(t, carry):
        def group(gi, carry2):
            half = gi // n_vg
            vg = gi % n_vg
            ls = pl.ds(pl.multiple_of(half * LANES, LANES), LANES)
            vs = pl.ds(pl.multiple_of(vg * SUBLANES, SUBLANES), SUBLANES)
            parts = [None] * RWKV_PARTIAL_SUMS
            for kk in range(c):
                term = s_ref[kk, vs, ls] * a_ref[t, pl.ds(kk, 1), ls]
                j = kk % RWKV_PARTIAL_SUMS
                parts[j] = term if parts[j] is None else parts[j] + term
            sa = _tree_sum(parts)
            vv = v_ref[t, vs, ls]
            parts = [None] * RWKV_PARTIAL_SUMS
            for kk in range(c):
                s_new = (s_ref[kk, vs, ls] * w_ref[t, pl.ds(kk, 1), ls]
                         + sa * b_ref[t, pl.ds(kk, 1), ls]
                         + vv * k_ref[t, pl.ds(kk, 1), ls])
                s_ref[kk, vs, ls] = s_new
                term = s_new * r_ref[t, pl.ds(kk, 1), ls]
                j = kk % RWKV_PARTIAL_SUMS
                parts[j] = term if parts[j] is None else parts[j] + term
            y_ref[t, vs, ls] = _tree_sum(parts)
            return carry2
        lax.fori_loop(0, n_lane * n_vg, group, 0, unroll=2)
        y = y_ref[t]
        mu = jnp.mean(y, axis=0, keepdims=True)
        yc = y - mu
        var = jnp.mean(yc * yc, axis=0, keepdims=True)
        y_ref[t] = yc * lax.rsqrt(var + R_LN_EPS)
        return carry

    lax.fori_loop(0, tb, step, 0)


def rwkv7_scan_t(r, w, k, v, a, b, *, tb=16):
    t, c, bh = r.shape
    spec = pl.BlockSpec((tb, c, bh), lambda i: (i, 0, 0))
    return pl.pallas_call(
        _rwkv_kernel,
        grid=(t // tb,),
        in_specs=[spec] * 6,
        out_specs=spec,
        out_shape=jax.ShapeDtypeStruct((t, c, bh), F32),
        scratch_shapes=[pltpu.VMEM((c, c, bh), F32)],
        compiler_params=_params("arbitrary"),
        name="rwkv7_scan",
    )(r, w, k, v, a, b)


def _retention_kernel(q_ref, k_ref, v_ref, g_ref, pos_ref, freq_ref, y_ref, state_ref):
    l = q_ref.shape[0]
    dk = RET_QK_HEAD
    dv = RET_V_HEAD
    half = dk // 2

    @pl.when(pl.program_id(1) == 0)
    def _():
        state_ref[...] = jnp.zeros_like(state_ref)

    ang = pos_ref[...].astype(F32) * freq_ref[...]
    cos = jnp.cos(ang)
    sin = jnp.sin(ang)

    def rotate(u):
        ue = u[:, :half]
        uo = u[:, half:]
        return jnp.concatenate([ue * cos - uo * sin, uo * cos + ue * sin], axis=-1)

    row = lax.broadcasted_iota(jnp.int32, (l, l), 0)
    col = lax.broadcasted_iota(jnp.int32, (l, l), 1)
    rel = (row - col).astype(F32)
    causal = row >= col
    idx = lax.broadcasted_iota(jnp.int32, (l, 1), 0).astype(F32)
    for h in range(RET_HEADS):
        log_gamma = float(np.log(np.float32(1.0) - np.float32(2.0) ** np.float32(-5.0 - h)))
        intra = jnp.where(causal, jnp.exp(log_gamma * jnp.where(causal, rel, 0.0)), 0.0)
        q_decay = jnp.exp(log_gamma * (idx + 1.0))
        k_decay = jnp.exp(log_gamma * (l - 1.0 - idx))
        chunk_decay = float(np.exp(np.float32(log_gamma) * np.float32(l)))
        q_h = rotate(q_ref[:, h * dk:(h + 1) * dk]).astype(BF16)
        k_h = rotate(k_ref[:, h * dk:(h + 1) * dk]) * (dk ** -0.5)
        v_h = v_ref[:, h * dv:(h + 1) * dv].astype(BF16)
        s = lax.dot_general(q_h, k_h.astype(BF16), (((1,), (1,)), ((), ())),
                            preferred_element_type=F32) * intra
        st = state_ref[h]
        y = jnp.dot(s.astype(BF16), v_h, preferred_element_type=F32)
        y = y + jnp.dot(q_h, st.astype(BF16), preferred_element_type=F32) * q_decay
        kd = (k_h * k_decay).astype(BF16)
        state_ref[h] = st * chunk_decay + lax.dot_general(
            kd, v_h, (((0,), (0,)), ((), ())), preferred_element_type=F32)
        mu = jnp.mean(y, axis=-1, keepdims=True)
        yc = y - mu
        var = jnp.mean(yc * yc, axis=-1, keepdims=True)
        g_h = g_ref[:, h * dv:(h + 1) * dv]
        y_ref[:, h * dv:(h + 1) * dv] = (g_h * _sigmoid(g_h) * (yc * lax.rsqrt(var + LN_EPS))).astype(y_ref.dtype)


def retention_mixer(proj, positions, inv_freq):
    bsz, t, _ = proj.shape
    hk = RET_HEADS * RET_QK_HEAD
    hv = RET_HEADS * RET_V_HEAD
    l = CHUNK
    return pl.pallas_call(
        _retention_kernel,
        grid=(bsz, t // l),
        in_specs=[pl.BlockSpec((None, l, hk), lambda b, c: (b, c, 0)),
                  pl.BlockSpec((None, l, hk), lambda b, c: (b, c, 1)),
                  pl.BlockSpec((None, l, hv), lambda b, c: (b, c, 1)),
                  pl.BlockSpec((None, l, hv), lambda b, c: (b, c, 2)),
                  pl.BlockSpec((None, l, 1), lambda b, c: (b, c, 0)),
                  pl.BlockSpec((1, RET_QK_HEAD // 2), lambda b, c: (0, 0))],
        out_specs=pl.BlockSpec((None, l, hv), lambda b, c: (b, c, 0)),
        out_shape=jax.ShapeDtypeStruct((bsz, t, hv), BF16),
        scratch_shapes=[pltpu.VMEM((RET_HEADS, RET_QK_HEAD, RET_V_HEAD), F32)],
        compiler_params=_params("parallel", "arbitrary"),
        name="retention_mixer",
    )(proj, proj, proj, proj, positions, inv_freq)


def _even_layer(x, w_in, conv_w, conv_b, dt_bias, a_log, d_skip, m_norm, mu_shift, w0, w_up, a0, a_up,
                g_up, k_k, k_a, r_k, rln_w, rln_b, w_out, ln1_g, ln1_b, wg, wu, wd, ln2_g, ln2_b):
    bsz, t, d = x.shape
    n = bsz * t
    m_inner = m_norm.shape[0]
    m_heads = dt_bias.shape[0]
    c = w0.shape[0]
    r_heads = c // R_HEAD_DIM
    gn = M_GROUPS * M_STATE
    o_xbc = m_inner
    o_dt = o_xbc + m_inner + 2 * gn
    o_rw = o_dt + m_heads
    o_lora = o_rw + 3 * c
    n_lora = R_DECAY_LORA + R_AAA_LORA + R_GATE_LORA
    x2 = x.reshape(n, d)

    zeros = lambda k: jnp.zeros((d, k), w_in.dtype)
    w_in_p = jnp.concatenate([
        w_in[:, o_rw:o_lora],
        w_in[:, :m_inner],
        w_in[:, o_xbc:o_dt],
        w_in[:, o_lora:o_lora + n_lora], zeros(LORA_DT_OFF - n_lora),
        w_in[:, o_dt:o_rw], zeros(LORA_BLOCK - LORA_DT_OFF - m_heads)], axis=1).astype(BF16)
    cols = {"r": 0, "k": c, "v": 2 * c, "z": 3 * c, "xs": 3 * c + m_inner, "B": 3 * c + 2 * m_inner,
            "C": 3 * c + 2 * m_inner + gn, "lora": 3 * c + 2 * m_inner + 2 * gn}
    cols["dt"] = cols["lora"] + LORA_DT_OFF
    proj = matmul(x2, w_in_p).reshape(bsz, t, -1)

    pad_lane = lambda u: jnp.pad(u.reshape(1, -1), ((0, 0), (0, LANES - u.shape[0])))
    y_ssd = ssd_mixer(proj, cols, conv_w, conv_b.reshape(1, -1), pad_lane(dt_bias), pad_lane(a_log),
                      jnp.repeat(d_skip, M_HEAD_DIM).reshape(1, -1), m_norm.reshape(1, -1))

    mu_p = jnp.pad(mu_shift, (0, LORA_BLOCK - n_lora)).reshape(1, -1)
    w_up_p = jnp.pad(w_up, ((0, LANES - R_DECAY_LORA), (0, 0))).astype(BF16)
    a_up_p = jnp.pad(a_up, ((R_DECAY_LORA, LANES - R_DECAY_LORA - R_AAA_LORA), (0, 0))).astype(BF16)
    g_up_p = jnp.pad(g_up, ((0, LORA_GATE_PAD - R_GATE_LORA), (0, 0))).astype(BF16)
    row = lambda u: u.reshape(1, -1)
    r, w, k, v, a, b, bonus, gate = rwkv_prep(proj, cols, mu_p, row(w0), row(a0), row(k_k), row(k_a), row(r_k),
                                              w_up_p, a_up_p, g_up_p)

    def to_t(u):
        return jnp.transpose(u.reshape(bsz, t, r_heads, R_HEAD_DIM), (1, 3, 0, 2)).reshape(t, R_HEAD_DIM, bsz * r_heads)

    y_t = rwkv7_scan_t(to_t(r), to_t(w), to_t(k), to_t(v), to_t(a), to_t(b))
    y_norm = jnp.transpose(y_t.reshape(t, R_HEAD_DIM, bsz, r_heads), (2, 0, 3, 1)).reshape(n, c)

    x2 = even_out_deepnorm(y_ssd.reshape(n, m_inner), y_norm, bonus.reshape(n, c), gate.reshape(n, c),
                           row(rln_w), row(rln_b), w_out.astype(BF16), x2, row(ln1_g), row(ln1_b))
    gates = jnp.ones((n, LANES), F32)
    x2 = moe_deepnorm(x2, gates, wg.astype(BF16)[None], wu.astype(BF16)[None], wd.astype(BF16)[None],
                      row(ln2_g), row(ln2_b))
    return x2.reshape(bsz, t, d)


def _odd_layer(x, positions, w_in, w_out, ln1_g, ln1_b, router, wg, wu, wd, ln2_g, ln2_b):
    bsz, t, d = x.shape
    n = bsz * t
    ret_qk = RET_HEADS * RET_QK_HEAD
    x2 = x.reshape(n, d)
    head_perm = np.concatenate([np.arange(0, RET_QK_HEAD, 2), np.arange(1, RET_QK_HEAD, 2)])
    qk_perm = (np.arange(2 * RET_HEADS)[:, None] * RET_QK_HEAD + head_perm[None, :]).reshape(-1)
    w_in_p = jnp.concatenate([w_in[:, qk_perm], w_in[:, 2 * ret_qk:]], axis=1).astype(BF16)
    proj = matmul(x2, w_in_p).reshape(bsz, t, -1)
    inv_freq = (1.0 / (10000.0 ** jnp.linspace(0.0, 1.0, RET_QK_HEAD // 2, dtype=F32))).reshape(1, -1)
    a = retention_mixer(proj, positions.reshape(bsz, t, 1), inv_freq)
    row = lambda u: u.reshape(1, -1)
    x2 = out_proj_deepnorm(a.reshape(n, -1), w_out.astype(BF16), x2, row(ln1_g), row(ln1_b))
    gates = router_gates(x2, jnp.pad(router, ((0, 0), (0, LANES - router.shape[1]))))
    x2 = moe_deepnorm(x2, gates, wg.astype(BF16), wu.astype(BF16), wd.astype(BF16), row(ln2_g), row(ln2_b))
    return x2.reshape(bsz, t, d)


def kernel(x, positions, ev_w_in, ev_conv_w, ev_conv_b, ev_dt_bias, ev_a_log, ev_d_skip, ev_m_norm, ev_mu_shift, ev_w0, ev_w_up, ev_a0, ev_a_up, ev_g_up, ev_k_k, ev_k_a, ev_r_k, ev_rln_w, ev_rln_b, ev_w_out, ev_ln1_g, ev_ln1_b, ev_ffn_wg, ev_ffn_wu, ev_ffn_wd, ev_ln2_g, ev_ln2_b, od_w_in, od_w_out, od_ln1_g, od_ln1_b, od_router, od_moe_wg, od_moe_wu, od_moe_wd, od_ln2_g, od_ln2_b):
    for layer in range(DEPTH):
        i = layer // 2
        if layer % 2 == 0:
            x = _even_layer(x, ev_w_in[i], ev_conv_w[i], ev_conv_b[i], ev_dt_bias[i], ev_a_log[i], ev_d_skip[i],
                            ev_m_norm[i], ev_mu_shift[i], ev_w0[i], ev_w_up[i], ev_a0[i], ev_a_up[i], ev_g_up[i],
                            ev_k_k[i], ev_k_a[i], ev_r_k[i], ev_rln_w[i], ev_rln_b[i], ev_w_out[i],
                            ev_ln1_g[i], ev_ln1_b[i], ev_ffn_wg[i], ev_ffn_wu[i], ev_ffn_wd[i],
                            ev_ln2_g[i], ev_ln2_b[i])
        else:
            x = _odd_layer(x, positions, od_w_in[i], od_w_out[i], od_ln1_g[i], od_ln1_b[i], od_router[i],
                           od_moe_wg[i], od_moe_wu[i], od_moe_wd[i], od_ln2_g[i], od_ln2_b[i])
    return x
```

```python
import jax
import jax.numpy as jnp
import numpy as np
from jax import lax
from jax.experimental import pallas as pl
from jax.experimental.pallas import tpu as pltpu

F32 = jnp.float32
BF16 = jnp.bfloat16
HIGHEST = lax.Precision.HIGHEST

CHUNK = 128
M_HEAD_DIM = 64
M_GROUPS = 2
M_STATE = 128
R_HEAD_DIM = 64
R_DECAY_LORA = 64
R_AAA_LORA = 64
R_GATE_LORA = 160
R_LN_EPS = 64e-5
RET_HEADS = 4
RET_QK_HEAD = 256
RET_V_HEAD = 512
N_EXPERTS = 8
LN_EPS = 1e-5
DEPTH = 2
ALPHA = (2.0 * DEPTH) ** 0.25

LANES = 128
SUBLANES = 8
MXU_DIM = 256
VMEM_LIMIT_BYTES = 56 * 1024 * 1024

LORA_BLOCK = 512
LORA_GATE_OFF = 128
LORA_GATE_PAD = 256
LORA_DT_OFF = 384


def _params(*sem):
    return pltpu.CompilerParams(dimension_semantics=sem, vmem_limit_bytes=VMEM_LIMIT_BYTES)


def _sigmoid(x):
    return jax.nn.sigmoid(x)


def _matmul_kernel(x_ref, w_ref, o_ref):
    o_ref[...] = jnp.dot(x_ref[...].astype(BF16), w_ref[...],
                         preferred_element_type=F32).astype(o_ref.dtype)


def matmul(x, w, *, tm=1024, tn=512, out_dtype=F32):
    n, k = x.shape
    m = w.shape[1]
    tm = min(tm, n)
    return pl.pallas_call(
        _matmul_kernel,
        grid=(n // tm, m // tn),
        in_specs=[pl.BlockSpec((tm, k), lambda i, j: (i, 0)),
                  pl.BlockSpec((k, tn), lambda i, j: (0, j))],
        out_specs=pl.BlockSpec((tm, tn), lambda i, j: (i, j)),
        out_shape=jax.ShapeDtypeStruct((n, m), out_dtype),
        compiler_params=_params("parallel", "arbitrary"),
        name="matmul",
    )(x, w)


def _deepnorm_rows(resid, sub, g, b):
    y = ALPHA * resid + sub
    mu = jnp.mean(y, axis=-1, keepdims=True)
    yc = y - mu
    var = jnp.mean(yc * yc, axis=-1, keepdims=True)
    return yc * lax.rsqrt(var + LN_EPS) * g + b


def _out_proj_kernel(a1_ref, a2_ref, w_ref, x_ref, g_ref, b_ref, o_ref):
    kh = a1_ref.shape[1]
    sub = jnp.dot(a1_ref[...], w_ref[:kh, :], preferred_element_type=F32)
    sub = sub + jnp.dot(a2_ref[...], w_ref[kh:, :], preferred_element_type=F32)
    o_ref[...] = _deepnorm_rows(x_ref[...], sub, g_ref[...], b_ref[...])


def out_proj_deepnorm(a, w, x, g, b, *, tm=512):
    n, d = x.shape
    kh = a.shape[1] // 2
    tm = min(tm, n)
    row = lambda i: (i, 0)
    fixed = lambda i: (0, 0)
    return pl.pallas_call(
        _out_proj_kernel,
        grid=(n // tm,),
        in_specs=[pl.BlockSpec((tm, kh), row), pl.BlockSpec((tm, kh), lambda i: (i, 1)),
                  pl.BlockSpec((2 * kh, d), fixed), pl.BlockSpec((tm, d), row),
                  pl.BlockSpec((1, d), fixed), pl.BlockSpec((1, d), fixed)],
        out_specs=pl.BlockSpec((tm, d), row),
        out_shape=jax.ShapeDtypeStruct((n, d), F32),
        compiler_params=_params("parallel"),
        name="out_proj_deepnorm",
    )(a, a, w, x, g, b)


def _even_out_kernel(ys_ref, yn_ref, bonus_ref, gate_ref, rlnw_ref, rlnb_ref, w_ref, x_ref, g_ref, b_ref, o_ref):
    kh = ys_ref.shape[1]
    y_rwkv = ((yn_ref[...] * rlnw_ref[...] + rlnb_ref[...]) + bonus_ref[...]) * gate_ref[...]
    sub = jnp.dot(ys_ref[...], w_ref[:kh, :], preferred_element_type=F32)
    sub = sub + jnp.dot(y_rwkv.astype(BF16), w_ref[kh:, :], preferred_element_type=F32)
    o_ref[...] = _deepnorm_rows(x_ref[...], sub, g_ref[...], b_ref[...])


def even_out_deepnorm(y_ssd, y_norm, bonus, gate, rln_w, rln_b, w, x, g, b, *, tm=512):
    n, d = x.shape
    kh = y_ssd.shape[1]
    tm = min(tm, n)
    row = lambda i: (i, 0)
    fixed = lambda i: (0, 0)
    return pl.pallas_call(
        _even_out_kernel,
        grid=(n // tm,),
        in_specs=[pl.BlockSpec((tm, kh), row)] * 4
        + [pl.BlockSpec((1, kh), fixed), pl.BlockSpec((1, kh), fixed),
           pl.BlockSpec((2 * kh, d), fixed), pl.BlockSpec((tm, d), row),
           pl.BlockSpec((1, d), fixed), pl.BlockSpec((1, d), fixed)],
        out_specs=pl.BlockSpec((tm, d), row),
        out_shape=jax.ShapeDtypeStruct((n, d), F32),
        compiler_params=_params("parallel"),
        name="even_out_deepnorm",
    )(y_ssd, y_norm, bonus, gate, rln_w, rln_b, w, x, g, b)


TOP_K = 2
R_IDX = 0
R_PROB = TOP_K


def _router_kernel(x_ref, wr_ref, info_ref):
    logits = jnp.dot(x_ref[...], wr_ref[...], preferred_element_type=F32, precision=HIGHEST)
    lane = lax.broadcasted_iota(jnp.int32, logits.shape, 1)
    neg = jnp.float32(-jnp.inf)
    lg = jnp.where(lane < N_EXPERTS, logits, neg)
    m1 = jnp.max(lg, axis=-1, keepdims=True)
    i1 = jnp.min(jnp.where(lg == m1, lane, LANES), axis=-1, keepdims=True)
    lg2 = jnp.where(lane == i1, neg, lg)
    m2 = jnp.max(lg2, axis=-1, keepdims=True)
    i2 = jnp.min(jnp.where(lg2 == m2, lane, LANES), axis=-1, keepdims=True)
    e2 = jnp.exp(m2 - m1)
    p1 = 1.0 / (1.0 + e2)
    p2 = e2 / (1.0 + e2)
    info = jnp.where(lane == R_IDX, i1.astype(F32), 0.0)
    info = jnp.where(lane == R_IDX + 1, i2.astype(F32), info)
    info = jnp.where(lane == R_PROB, p1, info)
    info = jnp.where(lane == R_PROB + 1, p2, info)
    info_ref[...] = info


def router_top2(x, w_router, *, tm=1024):
    n, d = x.shape
    tm = min(tm, n)
    return pl.pallas_call(
        _router_kernel,
        grid=(n // tm,),
        in_specs=[pl.BlockSpec((tm, d), lambda i: (i, 0)), pl.BlockSpec((d, LANES), lambda i: (0, 0))],
        out_specs=pl.BlockSpec((tm, LANES), lambda i: (i, 0)),
        out_shape=jax.ShapeDtypeStruct((n, LANES), F32),
        compiler_params=_params("parallel"),
        name="router_top2",
    )(x, w_router)


def _gather_rows_kernel(idx_ref, src_ref, out_ref, sem):
    tg = out_ref.shape[0]

    def issue(r, carry):
        pltpu.make_async_copy(src_ref.at[idx_ref[0, r]], out_ref.at[r], sem).start()
        return carry

    lax.fori_loop(0, tg, issue, 0)
    pltpu.make_async_copy(src_ref.at[pl.ds(0, tg)], out_ref, sem).wait()


def gather_rows(src, idx, *, tg=512):
    r = idx.shape[0]
    _, s, lanes = src.shape
    return pl.pallas_call(
        _gather_rows_kernel,
        grid=(r // tg,),
        in_specs=[pl.BlockSpec((None, 1, tg), lambda i: (i, 0, 0), memory_space=pltpu.SMEM),
                  pl.BlockSpec(memory_space=pl.ANY)],
        out_specs=pl.BlockSpec((tg, s, lanes), lambda i: (i, 0, 0)),
        out_shape=jax.ShapeDtypeStruct((r, s, lanes), src.dtype),
        scratch_shapes=[pltpu.SemaphoreType.DMA(())],
        compiler_params=_params("arbitrary"),
        name="gather_rows",
    )(idx.reshape(r // tg, 1, tg), src)


def _gather_pair_sum_kernel(idx_ref, src_ref, out_ref, buf_ref, sem):
    tc = out_ref.shape[0]

    def issue(t, carry):
        for j in range(TOP_K):
            pltpu.make_async_copy(src_ref.at[idx_ref[j, t]], buf_ref.at[j, t], sem.at[j]).start()
        return carry

    lax.fori_loop(0, tc, issue, 0)
    for j in range(TOP_K):
        pltpu.make_async_copy(src_ref.at[pl.ds(0, tc)], buf_ref.at[j], sem.at[j]).wait()
    out_ref[...] = buf_ref[0] + buf_ref[1]


def gather_pair_sum(src, idx):
    nt, _, tc = idx.shape
    _, s, lanes = src.shape
    return pl.pallas_call(
        _gather_pair_sum_kernel,
        grid=(nt,),
        in_specs=[pl.BlockSpec((None, TOP_K, tc), lambda i: (i, 0, 0), memory_space=pltpu.SMEM),
                  pl.BlockSpec(memory_space=pl.ANY)],
        out_specs=pl.BlockSpec((tc, s, lanes), lambda i: (i, 0, 0)),
        out_shape=jax.ShapeDtypeStruct((nt * tc, s, lanes), src.dtype),
        scratch_shapes=[pltpu.VMEM((TOP_K, tc, s, lanes), src.dtype), pltpu.SemaphoreType.DMA((TOP_K,))],
        compiler_params=_params("arbitrary"),
        name="gather_pair_sum",
    )(idx, src)


def _expert_kernel(te_ref, tv_ref, x_ref, gate_ref, wg_ref, wu_ref, wd_ref, y_ref, acc_ref, xb_ref):
    i = pl.program_id(0)
    f = pl.program_id(1)

    @pl.when(f == 0)
    def _():
        acc_ref[...] = jnp.zeros_like(acc_ref)
        xb_ref[...] = x_ref[...].astype(BF16)

    @pl.when(tv_ref[i] != 0)
    def _():
        xb = xb_ref[...]
        hg = jnp.dot(xb, wg_ref[...], preferred_element_type=F32)
        hu = jnp.dot(xb, wu_ref[...], preferred_element_type=F32)
        h = (hg * _sigmoid(hg)) * hu * gate_ref[:, 0:1]
        acc_ref[...] += jnp.dot(h.astype(BF16), wd_ref[...], preferred_element_type=F32)

    @pl.when(f == pl.num_programs(1) - 1)
    def _():
        y_ref[...] = acc_ref[...]


def expert_swiglu(xs, row_gate, tile_expert, tile_valid, wg, wu, wd, *, tm, tf=512):
    r, d = xs.shape
    ff = wg.shape[2]
    grid_spec = pltpu.PrefetchScalarGridSpec(
        num_scalar_prefetch=2,
        grid=(r // tm, ff // tf),
        in_specs=[pl.BlockSpec((tm, d), lambda i, f, te, tv: (i, 0)),
                  pl.BlockSpec((tm, LANES), lambda i, f, te, tv: (i, 0)),
                  pl.BlockSpec((None, d, tf), lambda i, f, te, tv: (te[i], 0, f)),
                  pl.BlockSpec((None, d, tf), lambda i, f, te, tv: (te[i], 0, f)),
                  pl.BlockSpec((None, tf, d), lambda i, f, te, tv: (te[i], f, 0))],
        out_specs=pl.BlockSpec((tm, d), lambda i, f, te, tv: (i, 0)),
        scratch_shapes=[pltpu.VMEM((tm, d), F32), pltpu.VMEM((tm, d), BF16)],
    )
    return pl.pallas_call(
        _expert_kernel,
        grid_spec=grid_spec,
        out_shape=jax.ShapeDtypeStruct((r, d), F32),
        compiler_params=_params("arbitrary", "arbitrary"),
        name="expert_swiglu",
    )(tile_expert, tile_valid, xs, row_gate, wg, wu, wd)


def _deepnorm_kernel(x_ref, sub_ref, g_ref, b_ref, o_ref):
    o_ref[...] = _deepnorm_rows(x_ref[...], sub_ref[...], g_ref[...], b_ref[...])


def deepnorm(x, sub, g, b, *, tm=1024):
    n, d = x.shape
    tm = min(tm, n)
    row = lambda i: (i, 0)
    fixed = lambda i: (0, 0)
    return pl.pallas_call(
        _deepnorm_kernel,
        grid=(n // tm,),
        in_specs=[pl.BlockSpec((tm, d), row), pl.BlockSpec((tm, d), row),
                  pl.BlockSpec((1, d), fixed), pl.BlockSpec((1, d), fixed)],
        out_specs=pl.BlockSpec((tm, d), row),
        out_shape=jax.ShapeDtypeStruct((n, d), F32),
        compiler_params=_params("parallel"),
        name="deepnorm",
    )(x, sub, g, b)


def moe_top2_deepnorm(x, w_router, wg, wu, wd, g, b, *, tm=512, tc=256):
    n, d = x.shape
    ne = wg.shape[0]
    tc = min(tc, n)
    info = router_top2(x, w_router)
    e_flat = info[:, R_IDX:R_IDX + TOP_K].astype(jnp.int32).reshape(-1)
    p_flat = info[:, R_PROB:R_PROB + TOP_K].reshape(-1)

    onehot = (e_flat[:, None] == jnp.arange(ne, dtype=jnp.int32)[None, :]).astype(jnp.int32)
    csum = jnp.cumsum(onehot, axis=0)
    rank = jnp.sum((csum - 1) * onehot, axis=1)
    padded = ((csum[-1] + tm - 1) // tm) * tm
    ends = jnp.cumsum(padded)
    dest = (ends - padded)[e_flat] + rank
    n_rows = n * TOP_K + ne * tm
    n_tiles = n_rows // tm
    src_tok = jnp.zeros((n_rows,), jnp.int32).at[dest].set(jnp.arange(n * TOP_K, dtype=jnp.int32) // TOP_K)
    row_gate = jnp.zeros((n_rows,), F32).at[dest].set(p_flat)
    tile_start = jnp.arange(n_tiles, dtype=jnp.int32) * tm
    tile_expert = jnp.minimum(jnp.searchsorted(ends, tile_start, side="right"), ne - 1).astype(jnp.int32)
    tile_valid = (tile_start < ends[-1]).astype(jnp.int32)

    xs = gather_rows(x.reshape(n, d // LANES, LANES), src_tok, tg=tm).reshape(n_rows, d)
    ys = expert_swiglu(xs, jnp.broadcast_to(row_gate[:, None], (n_rows, LANES)), tile_expert, tile_valid,
                       wg, wu, wd, tm=tm)
    pair_idx = jnp.transpose(dest.reshape(n // tc, tc, TOP_K), (0, 2, 1))
    sub = gather_pair_sum(ys.reshape(n_rows, d // LANES, LANES), pair_idx).reshape(n, d)
    return deepnorm(x, sub, g, b)


def _ffn_kernel(x_ref, wg_ref, wu_ref, wd_ref, g_ref, b_ref, o_ref, acc_ref, xb_ref):
    f = pl.program_id(1)

    @pl.when(f == 0)
    def _():
        acc_ref[...] = jnp.zeros_like(acc_ref)
        xb_ref[...] = x_ref[...].astype(BF16)

    xb = xb_ref[...]
    hg = jnp.dot(xb, wg_ref[...], preferred_element_type=F32)
    hu = jnp.dot(xb, wu_ref[...], preferred_element_type=F32)
    h = (hg * _sigmoid(hg)) * hu
    acc_ref[...] += jnp.dot(h.astype(BF16), wd_ref[...], preferred_element_type=F32)

    @pl.when(f == pl.num_programs(1) - 1)
    def _():
        o_ref[...] = _deepnorm_rows(x_ref[...], acc_ref[...], g_ref[...], b_ref[...])


def ffn_deepnorm(x, wg, wu, wd, g, b, *, tm=1024, tf=512):
    n, d = x.shape
    ff = wg.shape[1]
    tm = min(tm, n)
    row = lambda i, f: (i, 0)
    fixed = lambda i, f: (0, 0)
    return pl.pallas_call(
        _ffn_kernel,
        grid=(n // tm, ff // tf),
        in_specs=[pl.BlockSpec((tm, d), row),
                  pl.BlockSpec((d, tf), lambda i, f: (0, f)),
                  pl.BlockSpec((d, tf), lambda i, f: (0, f)),
                  pl.BlockSpec((tf, d), lambda i, f: (f, 0)),
                  pl.BlockSpec((1, d), fixed), pl.BlockSpec((1, d), fixed)],
        out_specs=pl.BlockSpec((tm, d), row),
        out_shape=jax.ShapeDtypeStruct((n, d), F32),
        scratch_shapes=[pltpu.VMEM((tm, d), F32), pltpu.VMEM((tm, d), BF16)],
        compiler_params=_params("parallel", "arbitrary"),
        name="ffn_deepnorm",
    )(x, wg, wu, wd, g, b)


CONV_HISTORY = SUBLANES


def _ssd_kernel(xs_ref, b_ref, c_ref, z_ref, dt_ref, cw_ref, cb_ref, dtb_ref, alog_ref, dskip_ref, mnorm_ref,
                y_ref, state_ref, ext_ref, yacc_ref):
    l, hp = xs_ref.shape
    gn = b_ref.shape[1]
    p = M_HEAD_DIM
    ns = M_STATE
    nh = hp // p
    hpg = nh // M_GROUPS
    kc = cw_ref.shape[0]
    hist = CONV_HISTORY

    @pl.when(pl.program_id(1) == 0)
    def _():
        state_ref[...] = jnp.zeros_like(state_ref)
        ext_ref[0:hist, :] = jnp.zeros((hist, ext_ref.shape[1]), F32)

    ext_ref[hist:, 0:hp] = xs_ref[...]
    ext_ref[hist:, hp:hp + gn] = b_ref[...]
    ext_ref[hist:, hp + gn:] = c_ref[...]
    conv = cb_ref[...]
    for i in range(kc):
        conv = conv + cw_ref[i:i + 1, :] * ext_ref[pl.ds(hist - (kc - 1) + i, l), :]
    ext_ref[0:hist, :] = ext_ref[l:l + hist, :]
    xbc = conv * _sigmoid(conv)
    xs = xbc[:, :hp]

    dt_pre = dt_ref[...] + dtb_ref[...]
    dt = jnp.maximum(dt_pre, 0.0) + jnp.log(1.0 + jnp.exp(-jnp.abs(dt_pre)))
    a = -jnp.exp(alog_ref[...])

    row = lax.broadcasted_iota(jnp.int32, (l, l), 0)
    col = lax.broadcasted_iota(jnp.int32, (l, l), 1)
    causal = row >= col
    acs = jnp.dot(causal.astype(F32), dt * a, preferred_element_type=F32, precision=HIGHEST)
    acst = acs.T
    dtt = dt.T
    acs_last = acs[l - 1:l, :]
    w_end = jnp.exp(acs_last - acs) * dt
    exp_acs = jnp.exp(acs)
    exp_last = jnp.exp(acs_last)

    for g in range(M_GROUPS):
        b_g = xbc[:, hp + g * ns:hp + (g + 1) * ns]
        c_g = xbc[:, hp + gn + g * ns:hp + gn + (g + 1) * ns].astype(BF16)
        bt_g = b_g.T.astype(BF16)
        cb = jnp.dot(c_g, bt_g, preferred_element_type=F32)
        for hh in range(hpg):
            h = g * hpg + hh
            x_h = xs[:, h * p:(h + 1) * p]
            seg = acs[:, h:h + 1] - acst[h:h + 1, :]
            decay = jnp.where(causal, jnp.exp(seg), 0.0)
            m = cb * decay * dtt[h:h + 1, :]
            st = state_ref[h]
            y = jnp.dot(m.astype(BF16), x_h.astype(BF16), preferred_element_type=F32)
            y = y + jnp.dot(c_g, st.astype(BF16), preferred_element_type=F32) * exp_acs[:, h:h + 1]
            yacc_ref[:, h * p:(h + 1) * p] = y
            xw = (x_h * w_end[:, h:h + 1]).astype(BF16)
            state_ref[h] = st * exp_last[:, h:h + 1] + jnp.dot(bt_g, xw, preferred_element_type=F32)

    z = z_ref[...]
    y = (yacc_ref[...] + dskip_ref[...] * xs) * (z * _sigmoid(z))
    gw = hp // M_GROUPS
    for g in range(M_GROUPS):
        seg = y[:, g * gw:(g + 1) * gw]
        ms = jnp.mean(seg * seg, axis=-1, keepdims=True)
        y_ref[:, g * gw:(g + 1) * gw] = (seg * lax.rsqrt(ms + LN_EPS)
                                         * mnorm_ref[:, g * gw:(g + 1) * gw]).astype(y_ref.dtype)


def ssd_mixer(proj, cols, conv_w, conv_b, dt_bias, a_log, d_skip, m_norm):
    bsz, t, _ = proj.shape
    hp = m_norm.shape[1]
    gn = M_GROUPS * M_STATE
    conv_dim = hp + 2 * gn
    nh = hp // M_HEAD_DIM
    l = CHUNK

    def col_block(width, off):
        blk = off // width
        return pl.BlockSpec((None, l, width), lambda b, c: (b, c, blk))

    fixed = lambda b, c: (0, 0)
    return pl.pallas_call(
        _ssd_kernel,
        grid=(bsz, t // l),
        in_specs=[col_block(hp, cols["xs"]), col_block(gn, cols["B"]), col_block(gn, cols["C"]),
                  col_block(hp, cols["z"]), col_block(LANES, cols["dt"]),
                  pl.BlockSpec(conv_w.shape, fixed), pl.BlockSpec((1, conv_dim), fixed),
                  pl.BlockSpec((1, LANES), fixed), pl.BlockSpec((1, LANES), fixed),
                  pl.BlockSpec((1, hp), fixed), pl.BlockSpec((1, hp), fixed)],
        out_specs=pl.BlockSpec((None, l, hp), lambda b, c: (b, c, 0)),
        out_shape=jax.ShapeDtypeStruct((bsz, t, hp), BF16),
        scratch_shapes=[pltpu.VMEM((nh, M_STATE, M_HEAD_DIM), F32),
                        pltpu.VMEM((CONV_HISTORY + l, conv_dim), F32),
                        pltpu.VMEM((l, hp), F32)],
        compiler_params=_params("parallel", "arbitrary"),
        name="ssd_mixer",
    )(proj, proj, proj, proj, proj, conv_w, conv_b, dt_bias, a_log, d_skip, m_norm)


RWKV_PARTIAL_SUMS = 8
EXP_MINUS_HALF = float(np.exp(-0.5))


def _tree_sum(parts):
    while len(parts) > 1:
        parts = [parts[i] + parts[i + 1] for i in range(0, len(parts), 2)]
    return parts[0]


def _head_sum(x, ones):
    hi = x.astype(BF16)
    lo = (x - hi.astype(F32)).astype(BF16)
    outs = []
    for j in range(x.shape[1] // MXU_DIM):
        sl = slice(j * MXU_DIM, (j + 1) * MXU_DIM)
        outs.append(jnp.dot(hi[:, sl], ones, preferred_element_type=F32)
                    + jnp.dot(lo[:, sl], ones, preferred_element_type=F32))
    return jnp.concatenate(outs, axis=-1)


def _rwkv_prep_kernel(r_ref, k_ref, v_ref, lo_ref, mu_ref, w0_ref, a0_ref, kk_ref, ka_ref, rk_ref,
                      wup_ref, aup_ref, gup_ref, ones_ref,
                      ro_ref, wo_ref, ko_ref, vo_ref, ao_ref, bo_ref, bonus_ref, gate_ref, prev_ref):
    tl, c = r_ref.shape

    @pl.when(pl.program_id(1) == 0)
    def _():
        prev_ref[...] = jnp.zeros_like(prev_ref)

    first_row = lax.broadcasted_iota(jnp.int32, (tl, 1), 0) == 0

    def shift_mix(x_ref, off):
        x = x_ref[...]
        width = x.shape[1]
        prev = jnp.where(first_row, prev_ref[0:1, off:off + width], pltpu.roll(x, 1, 0))
        prev_ref[0:1, off:off + width] = x[tl - 1:tl, :]
        return x + (prev - x) * mu_ref[:, off:off + width]

    r = shift_mix(r_ref, 0)
    k = shift_mix(k_ref, c)
    v = shift_mix(v_ref, 2 * c)
    lo = shift_mix(lo_ref, 3 * c)
    lo_a = lo[:, 0:LANES]
    lo_g = lo[:, LORA_GATE_OFF:LORA_GATE_OFF + LORA_GATE_PAD]
    w = w0_ref[...] + jnp.dot(jnp.tanh(lo_a).astype(BF16), wup_ref[...], preferred_element_type=F32)
    decay = jnp.exp(-EXP_MINUS_HALF * _sigmoid(w))
    iclr = _sigmoid(a0_ref[...] + jnp.dot(lo_a.astype(BF16), aup_ref[...], preferred_element_type=F32))
    gate = jnp.dot(_sigmoid(lo_g).astype(BF16), gup_ref[...], preferred_element_type=F32)
    ones = ones_ref[...]
    kk = k * kk_ref[...]
    kk = kk / jnp.maximum(jnp.sqrt(_head_sum(kk * kk, ones)), 1e-12)
    k2 = k * (1.0 + (iclr - 1.0) * ka_ref[...])
    ro_ref[...] = r
    wo_ref[...] = decay
    ko_ref[...] = k2
    vo_ref[...] = v
    ao_ref[...] = -kk
    bo_ref[...] = kk * iclr
    bonus_ref[...] = _head_sum(r * k2 * rk_ref[...], ones) * v
    gate_ref[...] = gate


def rwkv_prep(proj, cols, mu, w0, a0, k_k, k_a, r_k, w_up, a_up, g_up, *, tl=256):
    bsz, t, _ = proj.shape
    c = w0.shape[1]
    tl = min(tl, t)
    ones = jnp.asarray(np.kron(np.eye(MXU_DIM // R_HEAD_DIM), np.ones((R_HEAD_DIM, R_HEAD_DIM))), BF16)

    def col_block(width, off):
        blk = off // width
        return pl.BlockSpec((None, tl, width), lambda b, i: (b, i, blk))

    def fixed(shape):
        return pl.BlockSpec(shape, lambda b, i: (0, 0))

    out_spec = pl.BlockSpec((None, tl, c), lambda b, i: (b, i, 0))
    return pl.pallas_call(
        _rwkv_prep_kernel,
        grid=(bsz, t // tl),
        in_specs=[col_block(c, cols["r"]), col_block(c, cols["k"]), col_block(c, cols["v"]),
                  col_block(LORA_BLOCK, cols["lora"]), fixed(mu.shape)]
        + [fixed((1, c))] * 5 + [fixed(w_up.shape), fixed(a_up.shape), fixed(g_up.shape), fixed(ones.shape)],
        out_specs=[out_spec] * 8,
        out_shape=[jax.ShapeDtypeStruct((bsz, t, c), F32)] * 8,
        scratch_shapes=[pltpu.VMEM((SUBLANES, mu.shape[1]), F32)],
        compiler_params=_params("parallel", "arbitrary"),
        name="rwkv_prep",
    )(proj, proj, proj, proj, mu, w0, a0, k_k, k_a, r_k, w_up, a_up, g_up, ones)


def _rwkv_kernel(r_ref, w_ref, k_ref, v_ref, a_ref, b_ref, y_ref, s_ref):
    tb, c, bh = r_ref.shape
    n_lane = bh // LANES
    n_vg = c // SUBLANES

    @pl.when(pl.program_id(0) == 0)
    def _():
        s_ref[...] = jnp.zeros_like(s_ref)

    def step(t, carry):
        def group(gi, carry2):
            half = gi // n_vg
            vg = gi % n_vg
            ls = pl.ds(pl.multiple_of(half * LANES, LANES), LANES)
            vs = pl.ds(pl.multiple_of(vg * SUBLANES, SUBLANES), SUBLANES)
            parts = [None] * RWKV_PARTIAL_SUMS
            for kk in range(c):
                term = s_ref[kk, vs, ls] * a_ref[t, pl.ds(kk, 1), ls]
                j = kk % RWKV_PARTIAL_SUMS
                parts[j] = term if parts[j] is None else parts[j] + term
            sa = _tree_sum(parts)
            vv = v_ref[t, vs, ls]
            parts = [None] * RWKV_PARTIAL_SUMS
            for kk in range(c):
                s_new = (s_ref[kk, vs, ls] * w_ref[t, pl.ds(kk, 1), ls]
                         + sa * b_ref[t, pl.ds(kk, 1), ls]
                         + vv * k_ref[t, pl.ds(kk, 1), ls])
                s_ref[kk, vs, ls] = s_new
                term = s_new * r_ref[t, pl.ds(kk, 1), ls]
                j = kk % RWKV_PARTIAL_SUMS
                parts[j] = term if parts[j] is None else parts[j] + term
            y_ref[t, vs, ls] = _tree_sum(parts)
            return carry2
        lax.fori_loop(0, n_lane * n_vg, group, 0, unroll=2)
        y = y_ref[t]
        mu = jnp.mean(y, axis=0, keepdims=True)
        yc = y - mu
        var = jnp.mean(yc * yc, axis=0, keepdims=True)
        y_ref[t] = yc * lax.rsqrt(var + R_LN_EPS)
        return carry

    lax.fori_loop(0, tb, step, 0)


def rwkv7_scan_t(r, w, k, v, a, b, *, tb=16):
    t, c, bh = r.shape
    spec = pl.BlockSpec((tb, c, bh), lambda i: (i, 0, 0))
    return pl.pallas_call(
        _rwkv_kernel,
        grid=(t // tb,),
        in_specs=[spec] * 6,
        out_specs=spec,
        out_shape=jax.ShapeDtypeStruct((t, c, bh), F32),
        scratch_shapes=[pltpu.VMEM((c, c, bh), F32)],
        compiler_params=_params("arbitrary"),
        name="rwkv7_scan",
    )(r, w, k, v, a, b)


def _retention_kernel(q_ref, k_ref, v_ref, g_ref, pos_ref, freq_ref, y_ref, state_ref):
    l = q_ref.shape[0]
    dk = RET_QK_HEAD
    dv = RET_V_HEAD
    half = dk // 2

    @pl.when(pl.program_id(1) == 0)
    def _():
        state_ref[...] = jnp.zeros_like(state_ref)

    ang = pos_ref[...].astype(F32) * freq_ref[...]
    cos = jnp.cos(ang)
    sin = jnp.sin(ang)

    def rotate(u):
        ue = u[:, :half]
        uo = u[:, half:]
        return jnp.concatenate([ue * cos - uo * sin, uo * cos + ue * sin], axis=-1)

    row = lax.broadcasted_iota(jnp.int32, (l, l), 0)
    col = lax.broadcasted_iota(jnp.int32, (l, l), 1)
    rel = (row - col).astype(F32)
    causal = row >= col
    idx = lax.broadcasted_iota(jnp.int32, (l, 1), 0).astype(F32)
    for h in range(RET_HEADS):
        log_gamma = float(np.log(np.float32(1.0) - np.float32(2.0) ** np.float32(-5.0 - h)))
        intra = jnp.where(causal, jnp.exp(log_gamma * jnp.where(causal, rel, 0.0)), 0.0)
        q_decay = jnp.exp(log_gamma * (idx + 1.0))
        k_decay = jnp.exp(log_gamma * (l - 1.0 - idx))
        chunk_decay = float(np.exp(np.float32(log_gamma) * np.float32(l)))
        q_h = rotate(q_ref[:, h * dk:(h + 1) * dk]).astype(BF16)
        k_h = rotate(k_ref[:, h * dk:(h + 1) * dk]) * (dk ** -0.5)
        v_h = v_ref[:, h * dv:(h + 1) * dv].astype(BF16)
        s = lax.dot_general(q_h, k_h.astype(BF16), (((1,), (1,)), ((), ())),
                            preferred_element_type=F32) * intra
        st = state_ref[h]
        y = jnp.dot(s.astype(BF16), v_h, preferred_element_type=F32)
        y = y + jnp.dot(q_h, st.astype(BF16), preferred_element_type=F32) * q_decay
        kd = (k_h * k_decay).astype(BF16)
        state_ref[h] = st * chunk_decay + lax.dot_general(
            kd, v_h, (((0,), (0,)), ((), ())), preferred_element_type=F32)
        mu = jnp.mean(y, axis=-1, keepdims=True)
        yc = y - mu
        var = jnp.mean(yc * yc, axis=-1, keepdims=True)
        g_h = g_ref[:, h * dv:(h + 1) * dv]
        y_ref[:, h * dv:(h + 1) * dv] = (g_h * _sigmoid(g_h) * (yc * lax.rsqrt(var + LN_EPS))).astype(y_ref.dtype)


def retention_mixer(proj, positions, inv_freq):
    bsz, t, _ = proj.shape
    hk = RET_HEADS * RET_QK_HEAD
    hv = RET_HEADS * RET_V_HEAD
    l = CHUNK
    return pl.pallas_call(
        _retention_kernel,
        grid=(bsz, t // l),
        in_specs=[pl.BlockSpec((None, l, hk), lambda b, c: (b, c, 0)),
                  pl.BlockSpec((None, l, hk), lambda b, c: (b, c, 1)),
                  pl.BlockSpec((None, l, hv), lambda b, c: (b, c, 1)),
                  pl.BlockSpec((None, l, hv), lambda b, c: (b, c, 2)),
                  pl.BlockSpec((None, l, 1), lambda b, c: (b, c, 0)),
                  pl.BlockSpec((1, RET_QK_HEAD // 2), lambda b, c: (0, 0))],
        out_specs=pl.BlockSpec((None, l, hv), lambda b, c: (b, c, 0)),
        out_shape=jax.ShapeDtypeStruct((bsz, t, hv), BF16),
        scratch_shapes=[pltpu.VMEM((RET_HEADS, RET_QK_HEAD, RET_V_HEAD), F32)],
        compiler_params=_params("parallel", "arbitrary"),
        name="retention_mixer",
    )(proj, proj, proj, proj, positions, inv_freq)


def _even_layer(x, w_in, conv_w, conv_b, dt_bias, a_log, d_skip, m_norm, mu_shift, w0, w_up, a0, a_up,
                g_up, k_k, k_a, r_k, rln_w, rln_b, w_out, ln1_g, ln1_b, wg, wu, wd, ln2_g, ln2_b):
    bsz, t, d = x.shape
    n = bsz * t
    m_inner = m_norm.shape[0]
    m_heads = dt_bias.shape[0]
    c = w0.shape[0]
    r_heads = c // R_HEAD_DIM
    gn = M_GROUPS * M_STATE
    o_xbc = m_inner
    o_dt = o_xbc + m_inner + 2 * gn
    o_rw = o_dt + m_heads
    o_lora = o_rw + 3 * c
    n_lora = R_DECAY_LORA + R_AAA_LORA + R_GATE_LORA
    x2 = x.reshape(n, d)

    zeros = lambda k: jnp.zeros((d, k), w_in.dtype)
    w_in_p = jnp.concatenate([
        w_in[:, o_rw:o_lora],
        w_in[:, :m_inner],
        w_in[:, o_xbc:o_dt],
        w_in[:, o_lora:o_lora + n_lora], zeros(LORA_DT_OFF - n_lora),
        w_in[:, o_dt:o_rw], zeros(LORA_BLOCK - LORA_DT_OFF - m_heads)], axis=1).astype(BF16)
    cols = {"r": 0, "k": c, "v": 2 * c, "z": 3 * c, "xs": 3 * c + m_inner, "B": 3 * c + 2 * m_inner,
            "C": 3 * c + 2 * m_inner + gn, "lora": 3 * c + 2 * m_inner + 2 * gn}
    cols["dt"] = cols["lora"] + LORA_DT_OFF
    proj = matmul(x2, w_in_p).reshape(bsz, t, -1)

    pad_lane = lambda u: jnp.pad(u.reshape(1, -1), ((0, 0), (0, LANES - u.shape[0])))
    y_ssd = ssd_mixer(proj, cols, conv_w, conv_b.reshape(1, -1), pad_lane(dt_bias), pad_lane(a_log),
                      jnp.repeat(d_skip, M_HEAD_DIM).reshape(1, -1), m_norm.reshape(1, -1))

    mu_p = jnp.pad(mu_shift, (0, LORA_BLOCK - n_lora)).reshape(1, -1)
    w_up_p = jnp.pad(w_up, ((0, LANES - R_DECAY_LORA), (0, 0))).astype(BF16)
    a_up_p = jnp.pad(a_up, ((R_DECAY_LORA, LANES - R_DECAY_LORA - R_AAA_LORA), (0, 0))).astype(BF16)
    g_up_p = jnp.pad(g_up, ((0, LORA_GATE_PAD - R_GATE_LORA), (0, 0))).astype(BF16)
    row = lambda u: u.reshape(1, -1)
    r, w, k, v, a, b, bonus, gate = rwkv_prep(proj, cols, mu_p, row(w0), row(a0), row(k_k), row(k_a), row(r_k),
                                              w_up_p, a_up_p, g_up_p)

    def to_t(u):
        return jnp.transpose(u.reshape(bsz, t, r_heads, R_HEAD_DIM), (1, 3, 0, 2)).reshape(t, R_HEAD_DIM, bsz * r_heads)

    y_t = rwkv7_scan_t(to_t(r), to_t(w), to_t(k), to_t(v), to_t(a), to_t(b))
    y_norm = jnp.transpose(y_t.reshape(t, R_HEAD_DIM, bsz, r_heads), (2, 0, 3, 1)).reshape(n, c)

    x2 = even_out_deepnorm(y_ssd.reshape(n, m_inner), y_norm, bonus.reshape(n, c), gate.reshape(n, c),
                           row(rln_w), row(rln_b), w_out.astype(BF16), x2, row(ln1_g), row(ln1_b))
    x2 = ffn_deepnorm(x2, wg.astype(BF16), wu.astype(BF16), wd.astype(BF16), row(ln2_g), row(ln2_b))
    return x2.reshape(bsz, t, d)


def _odd_layer(x, positions, w_in, w_out, ln1_g, ln1_b, router, wg, wu, wd, ln2_g, ln2_b):
    bsz, t, d = x.shape
    n = bsz * t
    ret_qk = RET_HEADS * RET_QK_HEAD
    x2 = x.reshape(n, d)
    head_perm = np.concatenate([np.arange(0, RET_QK_HEAD, 2), np.arange(1, RET_QK_HEAD, 2)])
    qk_perm = (np.arange(2 * RET_HEADS)[:, None] * RET_QK_HEAD + head_perm[None, :]).reshape(-1)
    w_in_p = jnp.concatenate([w_in[:, qk_perm], w_in[:, 2 * ret_qk:]], axis=1).astype(BF16)
    proj = matmul(x2, w_in_p).reshape(bsz, t, -1)
    inv_freq = (1.0 / (10000.0 ** jnp.linspace(0.0, 1.0, RET_QK_HEAD // 2, dtype=F32))).reshape(1, -1)
    a = retention_mixer(proj, positions.reshape(bsz, t, 1), inv_freq)
    row = lambda u: u.reshape(1, -1)
    x2 = out_proj_deepnorm(a.reshape(n, -1), w_out.astype(BF16), x2, row(ln1_g), row(ln1_b))
    x2 = moe_top2_deepnorm(x2, jnp.pad(router, ((0, 0), (0, LANES - router.shape[1]))),
                           wg.astype(BF16), wu.astype(BF16), wd.astype(BF16), row(ln2_g), row(ln2_b))
    return x2.reshape(bsz, t, d)


def kernel(x, positions, ev_w_in, ev_conv_w, ev_conv_b, ev_dt_bias, ev_a_log, ev_d_skip, ev_m_norm, ev_mu_shift, ev_w0, ev_w_up, ev_a0, ev_a_up, ev_g_up, ev_k_k, ev_k_a, ev_r_k, ev_rln_w, ev_rln_b, ev_w_out, ev_ln1_g, ev_ln1_b, ev_ffn_wg, ev_ffn_wu, ev_ffn_wd, ev_ln2_g, ev_ln2_b, od_w_in, od_w_out, od_ln1_g, od_ln1_b, od_router, od_moe_wg, od_moe_wu, od_moe_wd, od_ln2_g, od_ln2_b):
    for layer in range(DEPTH):
        i = layer // 2
        if layer % 2 == 0:
            x = _even_layer(x, ev_w_in[i], ev_conv_w[i], ev_conv_b[i], ev_dt_bias[i], ev_a_log[i], ev_d_skip[i],
                            ev_m_norm[i], ev_mu_shift[i], ev_w0[i], ev_w_up[i], ev_a0[i], ev_a_up[i], ev_g_up[i],
                            ev_k_k[i], ev_k_a[i], ev_r_k[i], ev_rln_w[i], ev_rln_b[i], ev_w_out[i],
                            ev_ln1_g[i], ev_ln1_b[i], ev_ffn_wg[i], ev_ffn_wu[i], ev_ffn_wd[i],
                            ev_ln2_g[i], ev_ln2_b[i])
        else:
            x = _odd_layer(x, positions, od_w_in[i], od_w_out[i], od_ln1_g[i], od_ln1_b[i], od_router[i],
                           od_moe_wg[i], od_moe_wu[i], od_moe_wd[i], od_ln2_g[i], od_ln2_b[i])
    return x
```

```python
import jax
import jax.numpy as jnp
import numpy as np
from jax import lax
from jax.experimental import pallas as pl
from jax.experimental.pallas import tpu as pltpu

F32 = jnp.float32
BF16 = jnp.bfloat16
HIGHEST = lax.Precision.HIGHEST

CHUNK = 128
M_HEAD_DIM = 64
M_GROUPS = 2
M_STATE = 128
R_HEAD_DIM = 64
R_DECAY_LORA = 64
R_AAA_LORA = 64
R_GATE_LORA = 160
R_LN_EPS = 64e-5
RET_HEADS = 4
RET_QK_HEAD = 256
RET_V_HEAD = 512
N_EXPERTS = 8
LN_EPS = 1e-5
DEPTH = 2
ALPHA = (2.0 * DEPTH) ** 0.25

LANES = 128
SUBLANES = 8
MXU_DIM = 256
VMEM_LIMIT_BYTES = 56 * 1024 * 1024

LORA_BLOCK = 512
LORA_GATE_OFF = 128
LORA_GATE_PAD = 256
LORA_DT_OFF = 384


def _params(*sem):
    return pltpu.CompilerParams(dimension_semantics=sem, vmem_limit_bytes=VMEM_LIMIT_BYTES)


def _sigmoid(x):
    return jax.nn.sigmoid(x)


def _matmul_kernel(x_ref, w_ref, o_ref, xb_ref):
    @pl.when(pl.program_id(1) == 0)
    def _():
        xb_ref[...] = x_ref[...].astype(BF16)

    o_ref[...] = jnp.dot(xb_ref[...], w_ref[...], preferred_element_type=F32).astype(o_ref.dtype)


def matmul(x, w, *, tm=1024, tn=1024, out_dtype=F32):
    n, k = x.shape
    m = w.shape[1]
    tm = min(tm, n)
    return pl.pallas_call(
        _matmul_kernel,
        grid=(n // tm, m // tn),
        in_specs=[pl.BlockSpec((tm, k), lambda i, j: (i, 0)),
                  pl.BlockSpec((k, tn), lambda i, j: (0, j))],
        out_specs=pl.BlockSpec((tm, tn), lambda i, j: (i, j)),
        out_shape=jax.ShapeDtypeStruct((n, m), out_dtype),
        scratch_shapes=[pltpu.VMEM((tm, k), BF16)],
        compiler_params=_params("parallel", "arbitrary"),
        name="matmul",
    )(x, w)


def _deepnorm_rows(resid, sub, g, b):
    y = ALPHA * resid + sub
    mu = jnp.mean(y, axis=-1, keepdims=True)
    yc = y - mu
    var = jnp.mean(yc * yc, axis=-1, keepdims=True)
    return yc * lax.rsqrt(var + LN_EPS) * g + b


def _out_proj_kernel(a1_ref, a2_ref, w_ref, x_ref, g_ref, b_ref, o_ref):
    kh = a1_ref.shape[1]
    sub = jnp.dot(a1_ref[...], w_ref[:kh, :], preferred_element_type=F32)
    sub = sub + jnp.dot(a2_ref[...], w_ref[kh:, :], preferred_element_type=F32)
    o_ref[...] = _deepnorm_rows(x_ref[...], sub, g_ref[...], b_ref[...])


def out_proj_deepnorm(a1, a2, w, x, g, b, *, tm=512):
    n, d = x.shape
    kh = w.shape[0] // 2
    tm = min(tm, n)
    row = lambda i: (i, 0)
    fixed = lambda i: (0, 0)
    if a2 is None:
        a2, second = a1, pl.BlockSpec((tm, kh), lambda i: (i, 1))
    else:
        second = pl.BlockSpec((tm, kh), row)
    return pl.pallas_call(
        _out_proj_kernel,
        grid=(n // tm,),
        in_specs=[pl.BlockSpec((tm, kh), row), second,
                  pl.BlockSpec((2 * kh, d), fixed), pl.BlockSpec((tm, d), row),
                  pl.BlockSpec((1, d), fixed), pl.BlockSpec((1, d), fixed)],
        out_specs=pl.BlockSpec((tm, d), row),
        out_shape=jax.ShapeDtypeStruct((n, d), F32),
        compiler_params=_params("parallel"),
        name="out_proj_deepnorm",
    )(a1, a2, w, x, g, b)


TOP_K = 2
R_IDX = 0
R_PROB = TOP_K


def _router_kernel(x_ref, wr_ref, info_ref):
    logits = jnp.dot(x_ref[...], wr_ref[...], preferred_element_type=F32, precision=HIGHEST)
    lane = lax.broadcasted_iota(jnp.int32, logits.shape, 1)
    neg = jnp.float32(-jnp.inf)
    lg = jnp.where(lane < N_EXPERTS, logits, neg)
    m1 = jnp.max(lg, axis=-1, keepdims=True)
    i1 = jnp.min(jnp.where(lg == m1, lane, LANES), axis=-1, keepdims=True)
    lg2 = jnp.where(lane == i1, neg, lg)
    m2 = jnp.max(lg2, axis=-1, keepdims=True)
    i2 = jnp.min(jnp.where(lg2 == m2, lane, LANES), axis=-1, keepdims=True)
    e2 = jnp.exp(m2 - m1)
    p1 = 1.0 / (1.0 + e2)
    p2 = e2 / (1.0 + e2)
    info = jnp.where(lane == R_IDX, i1.astype(F32), 0.0)
    info = jnp.where(lane == R_IDX + 1, i2.astype(F32), info)
    info = jnp.where(lane == R_PROB, p1, info)
    info = jnp.where(lane == R_PROB + 1, p2, info)
    info_ref[...] = info


def router_top2(x, w_router, *, tm=1024):
    n, d = x.shape
    tm = min(tm, n)
    return pl.pallas_call(
        _router_kernel,
        grid=(n // tm,),
        in_specs=[pl.BlockSpec((tm, d), lambda i: (i, 0)), pl.BlockSpec((d, LANES), lambda i: (0, 0))],
        out_specs=pl.BlockSpec((tm, LANES), lambda i: (i, 0)),
        out_shape=jax.ShapeDtypeStruct((n, LANES), F32),
        compiler_params=_params("parallel"),
        name="router_top2",
    )(x, w_router)


def _gather_rows_kernel(idx_ref, src_ref, out_ref, sem):
    tg = out_ref.shape[0]

    def issue(r, carry):
        pltpu.make_async_copy(src_ref.at[idx_ref[0, r]], out_ref.at[r], sem).start()
        return carry

    lax.fori_loop(0, tg, issue, 0)
    pltpu.make_async_copy(src_ref.at[pl.ds(0, tg)], out_ref, sem).wait()


def gather_rows(src, idx, *, tg=512):
    r = idx.shape[0]
    _, s, lanes = src.shape
    return pl.pallas_call(
        _gather_rows_kernel,
        grid=(r // tg,),
        in_specs=[pl.BlockSpec((None, 1, tg), lambda i: (i, 0, 0), memory_space=pltpu.SMEM),
                  pl.BlockSpec(memory_space=pl.ANY)],
        out_specs=pl.BlockSpec((tg, s, lanes), lambda i: (i, 0, 0)),
        out_shape=jax.ShapeDtypeStruct((r, s, lanes), src.dtype),
        scratch_shapes=[pltpu.SemaphoreType.DMA(())],
        compiler_params=_params("arbitrary"),
        name="gather_rows",
    )(idx.reshape(r // tg, 1, tg), src)


def _gather_pair_sum_kernel(idx_ref, p_ref, src_ref, out_ref, buf_ref, sem):
    tc = out_ref.shape[0]

    def issue(t, carry):
        for j in range(TOP_K):
            pltpu.make_async_copy(src_ref.at[idx_ref[j, t]], buf_ref.at[j, t], sem.at[j]).start()
        return carry

    lax.fori_loop(0, tc, issue, 0)
    for j in range(TOP_K):
        pltpu.make_async_copy(src_ref.at[pl.ds(0, tc)], buf_ref.at[j], sem.at[j]).wait()

    def combine(t, carry):
        out_ref[t] = buf_ref[0, t] * p_ref[0, t] + buf_ref[1, t] * p_ref[1, t]
        return carry

    lax.fori_loop(0, tc, combine, 0, unroll=8)


def gather_pair_sum(src, idx, probs):
    nt, _, tc = idx.shape
    _, s, lanes = src.shape
    smem_tile = pl.BlockSpec((None, TOP_K, tc), lambda i: (i, 0, 0), memory_space=pltpu.SMEM)
    return pl.pallas_call(
        _gather_pair_sum_kernel,
        grid=(nt,),
        in_specs=[smem_tile, smem_tile, pl.BlockSpec(memory_space=pl.ANY)],
        out_specs=pl.BlockSpec((tc, s, lanes), lambda i: (i, 0, 0)),
        out_shape=jax.ShapeDtypeStruct((nt * tc, s, lanes), src.dtype),
        scratch_shapes=[pltpu.VMEM((TOP_K, tc, s, lanes), src.dtype), pltpu.SemaphoreType.DMA((TOP_K,))],
        compiler_params=_params("arbitrary"),
        name="gather_pair_sum",
    )(idx, probs, src)


def _expert_kernel(te_ref, tv_ref, x_ref, wg_ref, wu_ref, wd_ref, y_ref, acc_ref, xb_ref):
    i = pl.program_id(0)
    f = pl.program_id(1)

    @pl.when(f == 0)
    def _():
        acc_ref[...] = jnp.zeros_like(acc_ref)
        xb_ref[...] = x_ref[...].astype(BF16)

    @pl.when(tv_ref[i] != 0)
    def _():
        xb = xb_ref[...]
        hg = jnp.dot(xb, wg_ref[...], preferred_element_type=F32)
        hu = jnp.dot(xb, wu_ref[...], preferred_element_type=F32)
        h = (hg * _sigmoid(hg)) * hu
        acc_ref[...] += jnp.dot(h.astype(BF16), wd_ref[...], preferred_element_type=F32)

    @pl.when(f == pl.num_programs(1) - 1)
    def _():
        y_ref[...] = acc_ref[...]


def expert_swiglu(xs, tile_expert, tile_valid, wg, wu, wd, *, tm, tf=512):
    r, d = xs.shape
    ff = wg.shape[2]
    grid_spec = pltpu.PrefetchScalarGridSpec(
        num_scalar_prefetch=2,
        grid=(r // tm, ff // tf),
        in_specs=[pl.BlockSpec((tm, d), lambda i, f, te, tv: (i, 0)),
                  pl.BlockSpec((None, d, tf), lambda i, f, te, tv: (te[i], 0, f)),
                  pl.BlockSpec((None, d, tf), lambda i, f, te, tv: (te[i], 0, f)),
                  pl.BlockSpec((None, tf, d), lambda i, f, te, tv: (te[i], f, 0))],
        out_specs=pl.BlockSpec((tm, d), lambda i, f, te, tv: (i, 0)),
        scratch_shapes=[pltpu.VMEM((tm, d), F32), pltpu.VMEM((tm, d), BF16)],
    )
    return pl.pallas_call(
        _expert_kernel,
        grid_spec=grid_spec,
        out_shape=jax.ShapeDtypeStruct((r, d), F32),
        compiler_params=_params("arbitrary", "arbitrary"),
        name="expert_swiglu",
    )(tile_expert, tile_valid, xs, wg, wu, wd)


def _deepnorm_kernel(x_ref, sub_ref, g_ref, b_ref, o_ref):
    o_ref[...] = _deepnorm_rows(x_ref[...], sub_ref[...], g_ref[...], b_ref[...])


def deepnorm(x, sub, g, b, *, tm=1024):
    n, d = x.shape
    tm = min(tm, n)
    row = lambda i: (i, 0)
    fixed = lambda i: (0, 0)
    return pl.pallas_call(
        _deepnorm_kernel,
        grid=(n // tm,),
        in_specs=[pl.BlockSpec((tm, d), row), pl.BlockSpec((tm, d), row),
                  pl.BlockSpec((1, d), fixed), pl.BlockSpec((1, d), fixed)],
        out_specs=pl.BlockSpec((tm, d), row),
        out_shape=jax.ShapeDtypeStruct((n, d), F32),
        compiler_params=_params("parallel"),
        name="deepnorm",
    )(x, sub, g, b)


def moe_top2_deepnorm(x, w_router, wg, wu, wd, g, b, *, tm=512, tc=256):
    n, d = x.shape
    ne = wg.shape[0]
    tc = min(tc, n)
    info = router_top2(x, w_router)
    e_flat = info[:, R_IDX:R_IDX + TOP_K].astype(jnp.int32).reshape(-1)
    p_flat = info[:, R_PROB:R_PROB + TOP_K].reshape(-1)

    onehot = (e_flat[:, None] == jnp.arange(ne, dtype=jnp.int32)[None, :]).astype(jnp.int32)
    csum = jnp.cumsum(onehot, axis=0)
    rank = jnp.sum((csum - 1) * onehot, axis=1)
    padded = ((csum[-1] + tm - 1) // tm) * tm
    ends = jnp.cumsum(padded)
    dest = (ends - padded)[e_flat] + rank
    n_rows = n * TOP_K + ne * tm
    n_tiles = n_rows // tm
    src_tok = jnp.zeros((n_rows,), jnp.int32).at[dest].set(jnp.arange(n * TOP_K, dtype=jnp.int32) // TOP_K)
    tile_start = jnp.arange(n_tiles, dtype=jnp.int32) * tm
    tile_expert = jnp.minimum(jnp.searchsorted(ends, tile_start, side="right"), ne - 1).astype(jnp.int32)
    tile_valid = (tile_start < ends[-1]).astype(jnp.int32)

    xs = gather_rows(x.reshape(n, d // LANES, LANES), src_tok, tg=tm).reshape(n_rows, d)
    ys = expert_swiglu(xs, tile_expert, tile_valid, wg, wu, wd, tm=tm)
    per_tile = lambda u: jnp.transpose(u.reshape(n // tc, tc, TOP_K), (0, 2, 1))
    sub = gather_pair_sum(ys.reshape(n_rows, d // LANES, LANES), per_tile(dest), per_tile(p_flat)).reshape(n, d)
    return deepnorm(x, sub, g, b)


def _ffn_kernel(x_ref, wg_ref, wu_ref, wd_ref, g_ref, b_ref, o_ref, acc_ref, xb_ref):
    f = pl.program_id(1)

    @pl.when(f == 0)
    def _():
        acc_ref[...] = jnp.zeros_like(acc_ref)
        xb_ref[...] = x_ref[...].astype(BF16)

    xb = xb_ref[...]
    hg = jnp.dot(xb, wg_ref[...], preferred_element_type=F32)
    hu = jnp.dot(xb, wu_ref[...], preferred_element_type=F32)
    h = (hg * _sigmoid(hg)) * hu
    acc_ref[...] += jnp.dot(h.astype(BF16), wd_ref[...], preferred_element_type=F32)

    @pl.when(f == pl.num_programs(1) - 1)
    def _():
        o_ref[...] = _deepnorm_rows(x_ref[...], acc_ref[...], g_ref[...], b_ref[...])


def ffn_deepnorm(x, wg, wu, wd, g, b, *, tm=1024, tf=512):
    n, d = x.shape
    ff = wg.shape[1]
    tm = min(tm, n)
    row = lambda i, f: (i, 0)
    fixed = lambda i, f: (0, 0)
    return pl.pallas_call(
        _ffn_kernel,
        grid=(n // tm, ff // tf),
        in_specs=[pl.BlockSpec((tm, d), row),
                  pl.BlockSpec((d, tf), lambda i, f: (0, f)),
                  pl.BlockSpec((d, tf), lambda i, f: (0, f)),
                  pl.BlockSpec((tf, d), lambda i, f: (f, 0)),
                  pl.BlockSpec((1, d), fixed), pl.BlockSpec((1, d), fixed)],
        out_specs=pl.BlockSpec((tm, d), row),
        out_shape=jax.ShapeDtypeStruct((n, d), F32),
        scratch_shapes=[pltpu.VMEM((tm, d), F32), pltpu.VMEM((tm, d), BF16)],
        compiler_params=_params("parallel", "arbitrary"),
        name="ffn_deepnorm",
    )(x, wg, wu, wd, g, b)


CONV_HISTORY = SUBLANES


def _ssd_kernel(xs_ref, b_ref, c_ref, z_ref, dt_ref, cw_ref, cb_ref, dtb_ref, alog_ref, dskip_ref, mnorm_ref,
                y_ref, state_ref, ext_ref, yacc_ref):
    l, hp = xs_ref.shape
    gn = b_ref.shape[1]
    p = M_HEAD_DIM
    ns = M_STATE
    nh = hp // p
    hpg = nh // M_GROUPS
    kc = cw_ref.shape[0]
    hist = CONV_HISTORY

    @pl.when(pl.program_id(1) == 0)
    def _():
        state_ref[...] = jnp.zeros_like(state_ref)
        ext_ref[0:hist, :] = jnp.zeros((hist, ext_ref.shape[1]), F32)

    ext_ref[hist:, 0:hp] = xs_ref[...]
    ext_ref[hist:, hp:hp + gn] = b_ref[...]
    ext_ref[hist:, hp + gn:] = c_ref[...]
    conv = cb_ref[...]
    for i in range(kc):
        conv = conv + cw_ref[i:i + 1, :] * ext_ref[pl.ds(hist - (kc - 1) + i, l), :]
    ext_ref[0:hist, :] = ext_ref[l:l + hist, :]
    xbc = conv * _sigmoid(conv)
    xs = xbc[:, :hp]

    dt_pre = dt_ref[...] + dtb_ref[...]
    dt = jnp.maximum(dt_pre, 0.0) + jnp.log(1.0 + jnp.exp(-jnp.abs(dt_pre)))
    a = -jnp.exp(alog_ref[...])

    row = lax.broadcasted_iota(jnp.int32, (l, l), 0)
    col = lax.broadcasted_iota(jnp.int32, (l, l), 1)
    causal = row >= col
    acs = jnp.dot(causal.astype(F32), dt * a, preferred_element_type=F32, precision=HIGHEST)
    acst = acs.T
    dtt = dt.T
    acs_last = acs[l - 1:l, :]
    w_end = jnp.exp(acs_last - acs) * dt
    exp_acs = jnp.exp(acs)
    exp_last = jnp.exp(acs_last)

    for g in range(M_GROUPS):
        b_g = xbc[:, hp + g * ns:hp + (g + 1) * ns]
        c_g = xbc[:, hp + gn + g * ns:hp + gn + (g + 1) * ns].astype(BF16)
        bt_g = b_g.T.astype(BF16)
        cb = jnp.dot(c_g, bt_g, preferred_element_type=F32)
        for hh in range(hpg):
            h = g * hpg + hh
            x_h = xs[:, h * p:(h + 1) * p]
            seg = acs[:, h:h + 1] - acst[h:h + 1, :]
            decay = jnp.where(causal, jnp.exp(seg), 0.0)
            m = cb * decay * dtt[h:h + 1, :]
            st = state_ref[h]
            y = jnp.dot(m.astype(BF16), x_h.astype(BF16), preferred_element_type=F32)
            y = y + jnp.dot(c_g, st.astype(BF16), preferred_element_type=F32) * exp_acs[:, h:h + 1]
            yacc_ref[:, h * p:(h + 1) * p] = y
            xw = (x_h * w_end[:, h:h + 1]).astype(BF16)
            state_ref[h] = st * exp_last[:, h:h + 1] + jnp.dot(bt_g, xw, preferred_element_type=F32)

    z = z_ref[...]
    y = (yacc_ref[...] + dskip_ref[...] * xs) * (z * _sigmoid(z))
    gw = hp // M_GROUPS
    for g in range(M_GROUPS):
        seg = y[:, g * gw:(g + 1) * gw]
        ms = jnp.mean(seg * seg, axis=-1, keepdims=True)
        y_ref[:, g * gw:(g + 1) * gw] = (seg * lax.rsqrt(ms + LN_EPS)
                                         * mnorm_ref[:, g * gw:(g + 1) * gw]).astype(y_ref.dtype)


def ssd_mixer(proj, cols, conv_w, conv_b, dt_bias, a_log, d_skip, m_norm):
    bsz, t, _ = proj.shape
    hp = m_norm.shape[1]
    gn = M_GROUPS * M_STATE
    conv_dim = hp + 2 * gn
    nh = hp // M_HEAD_DIM
    l = CHUNK

    def col_block(width, off):
        blk = off // width
        return pl.BlockSpec((None, l, width), lambda b, c: (b, c, blk))

    fixed = lambda b, c: (0, 0)
    return pl.pallas_call(
        _ssd_kernel,
        grid=(bsz, t // l),
        in_specs=[col_block(hp, cols["xs"]), col_block(gn, cols["B"]), col_block(gn, cols["C"]),
                  col_block(hp, cols["z"]), col_block(LANES, cols["dt"]),
                  pl.BlockSpec(conv_w.shape, fixed), pl.BlockSpec((1, conv_dim), fixed),
                  pl.BlockSpec((1, LANES), fixed), pl.BlockSpec((1, LANES), fixed),
                  pl.BlockSpec((1, hp), fixed), pl.BlockSpec((1, hp), fixed)],
        out_specs=pl.BlockSpec((None, l, hp), lambda b, c: (b, c, 0)),
        out_shape=jax.ShapeDtypeStruct((bsz, t, hp), BF16),
        scratch_shapes=[pltpu.VMEM((nh, M_STATE, M_HEAD_DIM), F32),
                        pltpu.VMEM((CONV_HISTORY + l, conv_dim), F32),
                        pltpu.VMEM((l, hp), F32)],
        compiler_params=_params("parallel", "arbitrary"),
        name="ssd_mixer",
    )(proj, proj, proj, proj, proj, conv_w, conv_b, dt_bias, a_log, d_skip, m_norm)


EXP_MINUS_HALF = float(np.exp(-0.5))


def _head_sum(x, ones):
    hi = x.astype(BF16)
    lo = (x - hi.astype(F32)).astype(BF16)
    outs = []
    for j in range(x.shape[1] // MXU_DIM):
        sl = slice(j * MXU_DIM, (j + 1) * MXU_DIM)
        outs.append(jnp.dot(hi[:, sl], ones, preferred_element_type=F32)
                    + jnp.dot(lo[:, sl], ones, preferred_element_type=F32))
    return jnp.concatenate(outs, axis=-1)


def _rwkv_prep_kernel(r_ref, k_ref, v_ref, lo_ref, mu_ref, w0_ref, a0_ref, kk_ref, ka_ref, rk_ref,
                      wup_ref, aup_ref, gup_ref, ones_ref,
                      ro_ref, wo_ref, ko_ref, vo_ref, ao_ref, bo_ref, bonus_ref, gate_ref, prev_ref):
    tl, c = r_ref.shape

    @pl.when(pl.program_id(1) == 0)
    def _():
        prev_ref[...] = jnp.zeros_like(prev_ref)

    first_row = lax.broadcasted_iota(jnp.int32, (tl, 1), 0) == 0

    def shift_mix(x_ref, off):
        x = x_ref[...]
        width = x.shape[1]
        prev = jnp.where(first_row, prev_ref[0:1, off:off + width], pltpu.roll(x, 1, 0))
        prev_ref[0:1, off:off + width] = x[tl - 1:tl, :]
        return x + (prev - x) * mu_ref[:, off:off + width]

    r = shift_mix(r_ref, 0)
    k = shift_mix(k_ref, c)
    v = shift_mix(v_ref, 2 * c)
    lo = shift_mix(lo_ref, 3 * c)
    lo_a = lo[:, 0:LANES]
    lo_g = lo[:, LORA_GATE_OFF:LORA_GATE_OFF + LORA_GATE_PAD]
    w = w0_ref[...] + jnp.dot(jnp.tanh(lo_a).astype(BF16), wup_ref[...], preferred_element_type=F32)
    log_decay = -EXP_MINUS_HALF * _sigmoid(w)
    iclr = _sigmoid(a0_ref[...] + jnp.dot(lo_a.astype(BF16), aup_ref[...], preferred_element_type=F32))
    gate = jnp.dot(_sigmoid(lo_g).astype(BF16), gup_ref[...], preferred_element_type=F32)
    ones = ones_ref[...]
    kk = k * kk_ref[...]
    kk = kk / jnp.maximum(jnp.sqrt(_head_sum(kk * kk, ones)), 1e-12)
    k2 = k * (1.0 + (iclr - 1.0) * ka_ref[...])
    ro_ref[...] = r
    wo_ref[...] = log_decay
    ko_ref[...] = k2
    vo_ref[...] = v
    ao_ref[...] = -kk
    bo_ref[...] = kk * iclr
    bonus_ref[...] = _head_sum(r * k2 * rk_ref[...], ones) * v
    gate_ref[...] = gate


def rwkv_prep(proj, cols, mu, w0, a0, k_k, k_a, r_k, w_up, a_up, g_up, *, tl=256):
    bsz, t, _ = proj.shape
    c = w0.shape[1]
    tl = min(tl, t)
    ones = jnp.asarray(np.kron(np.eye(MXU_DIM // R_HEAD_DIM), np.ones((R_HEAD_DIM, R_HEAD_DIM))), BF16)

    def col_block(width, off):
        blk = off // width
        return pl.BlockSpec((None, tl, width), lambda b, i: (b, i, blk))

    def fixed(shape):
        return pl.BlockSpec(shape, lambda b, i: (0, 0))

    out_spec = pl.BlockSpec((None, tl, c), lambda b, i: (b, i, 0))
    return pl.pallas_call(
        _rwkv_prep_kernel,
        grid=(bsz, t // tl),
        in_specs=[col_block(c, cols["r"]), col_block(c, cols["k"]), col_block(c, cols["v"]),
                  col_block(LORA_BLOCK, cols["lora"]), fixed(mu.shape)]
        + [fixed((1, c))] * 5 + [fixed(w_up.shape), fixed(a_up.shape), fixed(g_up.shape), fixed(ones.shape)],
        out_specs=[out_spec] * 8,
        out_shape=[jax.ShapeDtypeStruct((bsz, t, c), F32)] * 8,
        scratch_shapes=[pltpu.VMEM((SUBLANES, mu.shape[1]), F32)],
        compiler_params=_params("parallel", "arbitrary"),
        name="rwkv_prep",
    )(proj, proj, proj, proj, mu, w0, a0, k_k, k_a, r_k, w_up, a_up, g_up, ones)


RWKV_CHUNK = 64


def _nt(a, b):
    return lax.dot_general(a, b, (((1,), (1,)), ((), ())), preferred_element_type=F32)


def _tn(a, b):
    return lax.dot_general(a, b, (((0,), (0,)), ((), ())), preferred_element_type=F32)


def _rwkv_chunk_kernel(r_ref, lw_ref, k_ref, v_ref, a_ref, b_ref, bonus_ref, gate_ref, rlnw_ref, rlnb_ref,
                       ones_ref, y_ref, state_ref, yacc_ref):
    l, c = r_ref.shape
    hd = R_HEAD_DIM
    n_pair = c // LANES

    @pl.when(pl.program_id(1) == 0)
    def _():
        state_ref[...] = jnp.zeros_like(state_ref)

    row = lax.broadcasted_iota(jnp.int32, (l, l), 0)
    col = lax.broadcasted_iota(jnp.int32, (l, l), 1)
    lw = lw_ref[...]
    g = jnp.dot((row >= col).astype(F32), lw, preferred_element_type=F32, precision=HIGHEST)
    g_last = g[l - 1:l, :]
    e_g = jnp.exp(g)
    e_ng = jnp.exp(-g)
    e_last = jnp.exp(g_last)
    at_all = a_ref[...] * jnp.exp(g - lw)
    rt_all = r_ref[...] * e_g
    bt_all = b_ref[...] * e_ng
    kt_all = k_ref[...] * e_ng
    v_all = v_ref[...]

    wi = lax.broadcasted_iota(jnp.int32, (l, 2 * l), 0)
    wl = lax.broadcasted_iota(jnp.int32, (l, 2 * l), 1)
    wj = jnp.where(wl >= l, wl - l, wl)
    strict = wj < wi
    incl = wj <= wi
    eye_w = (wj == wi).astype(F32)
    first_w = wl < l
    first_c = lax.broadcasted_iota(jnp.int32, (l, LANES), 1) < hd
    bi = lax.broadcasted_iota(jnp.int32, (LANES, LANES), 0)
    bj = lax.broadcasted_iota(jnp.int32, (LANES, LANES), 1)
    same_head = (bi < hd) == (bj < hd)

    def block_rows(x, first):
        return jnp.concatenate([jnp.where(first, x, 0.0), jnp.where(first, 0.0, x)], axis=0)

    pairs = range(n_pair)
    lanes_of = [slice(p * LANES, (p + 1) * LANES) for p in pairs]
    dot = lambda x, y: jnp.dot(x, y, preferred_element_type=F32)
    ht = [state_ref[p] for p in pairs]
    lhs_ar = [jnp.concatenate([at_all[:, s], rt_all[:, s]], axis=0).astype(BF16) for s in lanes_of]
    gram = [_nt(lhs_ar[p], jnp.concatenate([block_rows(bt_all[:, s], first_c),
                                            block_rows(kt_all[:, s], first_c)], axis=0).astype(BF16))
            for p, s in zip(pairs, lanes_of)]
    h_ar = [_nt(lhs_ar[p], ht[p].astype(BF16)) for p in pairs]
    n_w = [jnp.where(strict, gram[p][0:l, 0:2 * l], 0.0) for p in pairs]
    t_w = [eye_w + n_w[p] for p in pairs]
    p_w = [dot(n_w[p].astype(BF16), block_rows(n_w[p], first_w).astype(BF16)) for p in pairs]
    for _ in range(int(np.log2(l)) - 1):
        res = [dot(jnp.concatenate([t_w[p], p_w[p]], axis=0).astype(BF16),
                   block_rows(p_w[p], first_w).astype(BF16)) for p in pairs]
        t_w = [t_w[p] + res[p][0:l] for p in pairs]
        p_w = [res[p][l:2 * l] for p in pairs]
    v_bd = [block_rows(v_all[:, s], first_c).astype(BF16) for s in lanes_of]
    rhs_u = [h_ar[p][0:l] + dot(jnp.where(strict, gram[p][0:l, 2 * l:4 * l], 0.0).astype(BF16), v_bd[p])
             for p in pairs]
    u = [dot(t_w[p].astype(BF16), block_rows(rhs_u[p], first_c).astype(BF16)) for p in pairs]
    for p, s in zip(pairs, lanes_of):
        rbk_w = jnp.where(jnp.concatenate([incl, incl], axis=1), gram[p][l:2 * l, :], 0.0)
        yacc_ref[:, s] = h_ar[p][l:2 * l] + dot(
            rbk_w.astype(BF16), jnp.concatenate([block_rows(u[p], first_c).astype(BF16), v_bd[p]], axis=0))
    for p, s in zip(pairs, lanes_of):
        e_l = e_last[:, s]
        upd = _tn(jnp.concatenate([u[p], v_all[:, s]], axis=0).astype(BF16),
                  jnp.concatenate([bt_all[:, s] * e_l, kt_all[:, s] * e_l], axis=0).astype(BF16))
        state_ref[p] = ht[p] * e_l + jnp.where(same_head, upd, 0.0)

    y = yacc_ref[...]
    ones = ones_ref[...]
    inv = 1.0 / hd
    mu = _head_sum(y, ones) * inv
    yc = y - mu
    var = _head_sum(yc * yc, ones) * inv
    yn = yc * lax.rsqrt(var + R_LN_EPS)
    y_ref[...] = (((yn * rlnw_ref[...] + rlnb_ref[...]) + bonus_ref[...]) * gate_ref[...]).astype(y_ref.dtype)


def rwkv_chunk_mixer(r, lw, k, v, a, b, bonus, gate, rln_w, rln_b):
    bsz, t, c = r.shape
    l = RWKV_CHUNK
    ones = jnp.asarray(np.kron(np.eye(MXU_DIM // R_HEAD_DIM), np.ones((R_HEAD_DIM, R_HEAD_DIM))), BF16)
    blk = pl.BlockSpec((None, l, c), lambda i, j: (i, j, 0))
    fixed = lambda shape: pl.BlockSpec(shape, lambda i, j: (0, 0))
    return pl.pallas_call(
        _rwkv_chunk_kernel,
        grid=(bsz, t // l),
        in_specs=[blk] * 8 + [fixed((1, c)), fixed((1, c)), fixed(ones.shape)],
        out_specs=blk,
        out_shape=jax.ShapeDtypeStruct((bsz, t, c), BF16),
        scratch_shapes=[pltpu.VMEM((c // LANES, LANES, LANES), F32), pltpu.VMEM((l, c), F32)],
        compiler_params=_params("parallel", "arbitrary"),
        name="rwkv_chunk_mixer",
    )(r, lw, k, v, a, b, bonus, gate, rln_w, rln_b, ones)


def _retention_kernel(q_ref, k_ref, v_ref, g_ref, pos_ref, freq_ref, y_ref, state_ref):
    l = q_ref.shape[0]
    dk = RET_QK_HEAD
    dv = RET_V_HEAD
    half = dk // 2

    @pl.when(pl.program_id(1) == 0)
    def _():
        state_ref[...] = jnp.zeros_like(state_ref)

    ang = pos_ref[...].astype(F32) * freq_ref[...]
    cos = jnp.cos(ang)
    sin = jnp.sin(ang)

    def rotate(u):
        ue = u[:, :half]
        uo = u[:, half:]
        return jnp.concatenate([ue * cos - uo * sin, uo * cos + ue * sin], axis=-1)

    row = lax.broadcasted_iota(jnp.int32, (l, l), 0)
    col = lax.broadcasted_iota(jnp.int32, (l, l), 1)
    rel = (row - col).astype(F32)
    causal = row >= col
    idx = lax.broadcasted_iota(jnp.int32, (l, 1), 0).astype(F32)
    for h in range(RET_HEADS):
        log_gamma = float(np.log(np.float32(1.0) - np.float32(2.0) ** np.float32(-5.0 - h)))
        intra = jnp.where(causal, jnp.exp(log_gamma * jnp.where(causal, rel, 0.0)), 0.0)
        q_decay = jnp.exp(log_gamma * (idx + 1.0))
        k_decay = jnp.exp(log_gamma * (l - 1.0 - idx))
        chunk_decay = float(np.exp(np.float32(log_gamma) * np.float32(l)))
        q_h = rotate(q_ref[:, h * dk:(h + 1) * dk]).astype(BF16)
        k_h = rotate(k_ref[:, h * dk:(h + 1) * dk]) * (dk ** -0.5)
        v_h = v_ref[:, h * dv:(h + 1) * dv].astype(BF16)
        s = lax.dot_general(q_h, k_h.astype(BF16), (((1,), (1,)), ((), ())),
                            preferred_element_type=F32) * intra
        st = state_ref[h]
        y = jnp.dot(s.astype(BF16), v_h, preferred_element_type=F32)
        y = y + jnp.dot(q_h, st.astype(BF16), preferred_element_type=F32) * q_decay
        kd = (k_h * k_decay).astype(BF16)
        state_ref[h] = st * chunk_decay + lax.dot_general(
            kd, v_h, (((0,), (0,)), ((), ())), preferred_element_type=F32)
        mu = jnp.mean(y, axis=-1, keepdims=True)
        yc = y - mu
        var = jnp.mean(yc * yc, axis=-1, keepdims=True)
        g_h = g_ref[:, h * dv:(h + 1) * dv]
        y_ref[:, h * dv:(h + 1) * dv] = (g_h * _sigmoid(g_h) * (yc * lax.rsqrt(var + LN_EPS))).astype(y_ref.dtype)


def retention_mixer(proj, positions, inv_freq):
    bsz, t, _ = proj.shape
    hk = RET_HEADS * RET_QK_HEAD
    hv = RET_HEADS * RET_V_HEAD
    l = CHUNK
    return pl.pallas_call(
        _retention_kernel,
        grid=(bsz, t // l),
        in_specs=[pl.BlockSpec((None, l, hk), lambda b, c: (b, c, 0)),
                  pl.BlockSpec((None, l, hk), lambda b, c: (b, c, 1)),
                  pl.BlockSpec((None, l, hv), lambda b, c: (b, c, 1)),
                  pl.BlockSpec((None, l, hv), lambda b, c: (b, c, 2)),
                  pl.BlockSpec((None, l, 1), lambda b, c: (b, c, 0)),
                  pl.BlockSpec((1, RET_QK_HEAD // 2), lambda b, c: (0, 0))],
        out_specs=pl.BlockSpec((None, l, hv), lambda b, c: (b, c, 0)),
        out_shape=jax.ShapeDtypeStruct((bsz, t, hv), BF16),
        scratch_shapes=[pltpu.VMEM((RET_HEADS, RET_QK_HEAD, RET_V_HEAD), F32)],
        compiler_params=_params("parallel", "arbitrary"),
        name="retention_mixer",
    )(proj, proj, proj, proj, positions, inv_freq)


def _even_layer(x, w_in, conv_w, conv_b, dt_bias, a_log, d_skip, m_norm, mu_shift, w0, w_up, a0, a_up,
                g_up, k_k, k_a, r_k, rln_w, rln_b, w_out, ln1_g, ln1_b, wg, wu, wd, ln2_g, ln2_b):
    bsz, t, d = x.shape
    n = bsz * t
    m_inner = m_norm.shape[0]
    m_heads = dt_bias.shape[0]
    c = w0.shape[0]
    r_heads = c // R_HEAD_DIM
    gn = M_GROUPS * M_STATE
    o_xbc = m_inner
    o_dt = o_xbc + m_inner + 2 * gn
    o_rw = o_dt + m_heads
    o_lora = o_rw + 3 * c
    n_lora = R_DECAY_LORA + R_AAA_LORA + R_GATE_LORA
    x2 = x.reshape(n, d)

    zeros = lambda k: jnp.zeros((d, k), w_in.dtype)
    w_in_p = jnp.concatenate([
        w_in[:, o_rw:o_lora],
        w_in[:, :m_inner],
        w_in[:, o_xbc:o_dt],
        w_in[:, o_lora:o_lora + n_lora], zeros(LORA_DT_OFF - n_lora),
        w_in[:, o_dt:o_rw], zeros(LORA_BLOCK - LORA_DT_OFF - m_heads)], axis=1).astype(BF16)
    cols = {"r": 0, "k": c, "v": 2 * c, "z": 3 * c, "xs": 3 * c + m_inner, "B": 3 * c + 2 * m_inner,
            "C": 3 * c + 2 * m_inner + gn, "lora": 3 * c + 2 * m_inner + 2 * gn}
    cols["dt"] = cols["lora"] + LORA_DT_OFF
    proj = matmul(x2, w_in_p).reshape(bsz, t, -1)

    pad_lane = lambda u: jnp.pad(u.reshape(1, -1), ((0, 0), (0, LANES - u.shape[0])))
    y_ssd = ssd_mixer(proj, cols, conv_w, conv_b.reshape(1, -1), pad_lane(dt_bias), pad_lane(a_log),
                      jnp.repeat(d_skip, M_HEAD_DIM).reshape(1, -1), m_norm.reshape(1, -1))

    mu_p = jnp.pad(mu_shift, (0, LORA_BLOCK - n_lora)).reshape(1, -1)
    w_up_p = jnp.pad(w_up, ((0, LANES - R_DECAY_LORA), (0, 0))).astype(BF16)
    a_up_p = jnp.pad(a_up, ((R_DECAY_LORA, LANES - R_DECAY_LORA - R_AAA_LORA), (0, 0))).astype(BF16)
    g_up_p = jnp.pad(g_up, ((0, LORA_GATE_PAD - R_GATE_LORA), (0, 0))).astype(BF16)
    row = lambda u: u.reshape(1, -1)
    r, lw, k, v, a, b, bonus, gate = rwkv_prep(proj, cols, mu_p, row(w0), row(a0), row(k_k), row(k_a), row(r_k),
                                               w_up_p, a_up_p, g_up_p)
    y_rwkv = rwkv_chunk_mixer(r, lw, k, v, a, b, bonus, gate, row(rln_w), row(rln_b))
    x2 = out_proj_deepnorm(y_ssd.reshape(n, m_inner), y_rwkv.reshape(n, c), w_out.astype(BF16), x2,
                           row(ln1_g), row(ln1_b))
    x2 = ffn_deepnorm(x2, wg.astype(BF16), wu.astype(BF16), wd.astype(BF16), row(ln2_g), row(ln2_b))
    return x2.reshape(bsz, t, d)


def _odd_layer(x, positions, w_in, w_out, ln1_g, ln1_b, router, wg, wu, wd, ln2_g, ln2_b):
    bsz, t, d = x.shape
    n = bsz * t
    ret_qk = RET_HEADS * RET_QK_HEAD
    x2 = x.reshape(n, d)
    head_perm = np.concatenate([np.arange(0, RET_QK_HEAD, 2), np.arange(1, RET_QK_HEAD, 2)])
    qk_perm = (np.arange(2 * RET_HEADS)[:, None] * RET_QK_HEAD + head_perm[None, :]).reshape(-1)
    w_in_p = jnp.concatenate([w_in[:, qk_perm], w_in[:, 2 * ret_qk:]], axis=1).astype(BF16)
    proj = matmul(x2, w_in_p).reshape(bsz, t, -1)
    inv_freq = (1.0 / (10000.0 ** jnp.linspace(0.0, 1.0, RET_QK_HEAD // 2, dtype=F32))).reshape(1, -1)
    a = retention_mixer(proj, positions.reshape(bsz, t, 1), inv_freq)
    row = lambda u: u.reshape(1, -1)
    x2 = out_proj_deepnorm(a.reshape(n, -1), None, w_out.astype(BF16), x2, row(ln1_g), row(ln1_b))
    x2 = moe_top2_deepnorm(x2, jnp.pad(router, ((0, 0), (0, LANES - router.shape[1]))),
                           wg.astype(BF16), wu.astype(BF16), wd.astype(BF16), row(ln2_g), row(ln2_b))
    return x2.reshape(bsz, t, d)


def kernel(x, positions, ev_w_in, ev_conv_w, ev_conv_b, ev_dt_bias, ev_a_log, ev_d_skip, ev_m_norm, ev_mu_shift, ev_w0, ev_w_up, ev_a0, ev_a_up, ev_g_up, ev_k_k, ev_k_a, ev_r_k, ev_rln_w, ev_rln_b, ev_w_out, ev_ln1_g, ev_ln1_b, ev_ffn_wg, ev_ffn_wu, ev_ffn_wd, ev_ln2_g, ev_ln2_b, od_w_in, od_w_out, od_ln1_g, od_ln1_b, od_router, od_moe_wg, od_moe_wu, od_moe_wd, od_ln2_g, od_ln2_b):
    for layer in range(DEPTH):
        i = layer // 2
        if layer % 2 == 0:
            x = _even_layer(x, ev_w_in[i], ev_conv_w[i], ev_conv_b[i], ev_dt_bias[i], ev_a_log[i], ev_d_skip[i],
                            ev_m_norm[i], ev_mu_shift[i], ev_w0[i], ev_w_up[i], ev_a0[i], ev_a_up[i], ev_g_up[i],
                            ev_k_k[i], ev_k_a[i], ev_r_k[i], ev_rln_w[i], ev_rln_b[i], ev_w_out[i],
                            ev_ln1_g[i], ev_ln1_b[i], ev_ffn_wg[i], ev_ffn_wu[i], ev_ffn_wd[i],
                            ev_ln2_g[i], ev_ln2_b[i])
        else:
            x = _odd_layer(x, positions, od_w_in[i], od_w_out[i], od_ln1_g[i], od_ln1_b[i], od_router[i],
                           od_moe_wg[i], od_moe_wu[i], od_moe_wd[i], od_ln2_g[i], od_ln2_b[i])
    return x
```

```python
import jax
import jax.numpy as jnp
import numpy as np
from jax import lax
from jax.experimental import pallas as pl
from jax.experimental.pallas import tpu as pltpu

F32 = jnp.float32
BF16 = jnp.bfloat16
HIGHEST = lax.Precision.HIGHEST

CHUNK = 128
M_HEAD_DIM = 64
M_GROUPS = 2
M_STATE = 128
R_HEAD_DIM = 64
R_DECAY_LORA = 64
R_AAA_LORA = 64
R_GATE_LORA = 160
R_LN_EPS = 64e-5
RET_HEADS = 4
RET_QK_HEAD = 256
RET_V_HEAD = 512
N_EXPERTS = 8
LN_EPS = 1e-5
DEPTH = 2
ALPHA = (2.0 * DEPTH) ** 0.25

LANES = 128
SUBLANES = 8
MXU_DIM = 256
VMEM_LIMIT_BYTES = 56 * 1024 * 1024

LORA_BLOCK = 512
LORA_GATE_OFF = 128
LORA_GATE_PAD = 256
LORA_DT_OFF = 384


def _params(*sem):
    return pltpu.CompilerParams(dimension_semantics=sem, vmem_limit_bytes=VMEM_LIMIT_BYTES)


def _sigmoid(x):
    return jax.nn.sigmoid(x)


def _matmul_kernel(x_ref, w_ref, o_ref, xb_ref):
    @pl.when(pl.program_id(1) == 0)
    def _():
        xb_ref[...] = x_ref[...].astype(BF16)

    o_ref[...] = jnp.dot(xb_ref[...], w_ref[...], preferred_element_type=F32).astype(o_ref.dtype)


def matmul(x, w, *, tm=1024, tn=1024, out_dtype=F32):
    n, k = x.shape
    m = w.shape[1]
    tm = min(tm, n)
    return pl.pallas_call(
        _matmul_kernel,
        grid=(n // tm, m // tn),
        in_specs=[pl.BlockSpec((tm, k), lambda i, j: (i, 0)),
                  pl.BlockSpec((k, tn), lambda i, j: (0, j))],
        out_specs=pl.BlockSpec((tm, tn), lambda i, j: (i, j)),
        out_shape=jax.ShapeDtypeStruct((n, m), out_dtype),
        scratch_shapes=[pltpu.VMEM((tm, k), BF16)],
        compiler_params=_params("parallel", "arbitrary"),
        name="matmul",
    )(x, w)


def _deepnorm_rows(resid, sub, g, b):
    y = ALPHA * resid + sub
    mu = jnp.mean(y, axis=-1, keepdims=True)
    yc = y - mu
    var = jnp.mean(yc * yc, axis=-1, keepdims=True)
    return yc * lax.rsqrt(var + LN_EPS) * g + b


def _out_proj_kernel(a1_ref, a2_ref, w_ref, x_ref, g_ref, b_ref, o_ref):
    kh = a1_ref.shape[1]
    sub = jnp.dot(a1_ref[...], w_ref[:kh, :], preferred_element_type=F32)
    sub = sub + jnp.dot(a2_ref[...], w_ref[kh:, :], preferred_element_type=F32)
    o_ref[...] = _deepnorm_rows(x_ref[...], sub, g_ref[...], b_ref[...])


def out_proj_deepnorm(a1, a2, w, x, g, b, *, tm=512):
    n, d = x.shape
    kh = w.shape[0] // 2
    tm = min(tm, n)
    row = lambda i: (i, 0)
    fixed = lambda i: (0, 0)
    if a2 is None:
        a2, second = a1, pl.BlockSpec((tm, kh), lambda i: (i, 1))
    else:
        second = pl.BlockSpec((tm, kh), row)
    return pl.pallas_call(
        _out_proj_kernel,
        grid=(n // tm,),
        in_specs=[pl.BlockSpec((tm, kh), row), second,
                  pl.BlockSpec((2 * kh, d), fixed), pl.BlockSpec((tm, d), row),
                  pl.BlockSpec((1, d), fixed), pl.BlockSpec((1, d), fixed)],
        out_specs=pl.BlockSpec((tm, d), row),
        out_shape=jax.ShapeDtypeStruct((n, d), F32),
        compiler_params=_params("parallel"),
        name="out_proj_deepnorm",
    )(a1, a2, w, x, g, b)


TOP_K = 2
R_IDX = 0
R_PROB = TOP_K


def _router_kernel(x_ref, wr_ref, info_ref):
    logits = jnp.dot(x_ref[...], wr_ref[...], preferred_element_type=F32, precision=HIGHEST)
    lane = lax.broadcasted_iota(jnp.int32, logits.shape, 1)
    neg = jnp.float32(-jnp.inf)
    lg = jnp.where(lane < N_EXPERTS, logits, neg)
    m1 = jnp.max(lg, axis=-1, keepdims=True)
    i1 = jnp.min(jnp.where(lg == m1, lane, LANES), axis=-1, keepdims=True)
    lg2 = jnp.where(lane == i1, neg, lg)
    m2 = jnp.max(lg2, axis=-1, keepdims=True)
    i2 = jnp.min(jnp.where(lg2 == m2, lane, LANES), axis=-1, keepdims=True)
    e2 = jnp.exp(m2 - m1)
    p1 = 1.0 / (1.0 + e2)
    p2 = e2 / (1.0 + e2)
    info = jnp.where(lane == R_IDX, i1.astype(F32), 0.0)
    info = jnp.where(lane == R_IDX + 1, i2.astype(F32), info)
    info = jnp.where(lane == R_PROB, p1, info)
    info = jnp.where(lane == R_PROB + 1, p2, info)
    info_ref[...] = info


def router_top2(x, w_router, *, tm=1024):
    n, d = x.shape
    tm = min(tm, n)
    return pl.pallas_call(
        _router_kernel,
        grid=(n // tm,),
        in_specs=[pl.BlockSpec((tm, d), lambda i: (i, 0)), pl.BlockSpec((d, LANES), lambda i: (0, 0))],
        out_specs=pl.BlockSpec((tm, LANES), lambda i: (i, 0)),
        out_shape=jax.ShapeDtypeStruct((n, LANES), F32),
        compiler_params=_params("parallel"),
        name="router_top2",
    )(x, w_router)


def _gather_rows_kernel(idx_ref, src_ref, out_ref, sem):
    tg = out_ref.shape[0]

    def issue(r, carry):
        pltpu.make_async_copy(src_ref.at[pl.ds(idx_ref[0, r], 1)], out_ref.at[pl.ds(r, 1)], sem).start()
        return carry

    lax.fori_loop(0, tg, issue, 0)
    pltpu.make_async_copy(src_ref.at[pl.ds(0, tg)], out_ref, sem).wait()


def gather_rows(src, idx, *, tg=512):
    r = idx.shape[0]
    d = src.shape[1]
    return pl.pallas_call(
        _gather_rows_kernel,
        grid=(r // tg,),
        in_specs=[pl.BlockSpec((None, 1, tg), lambda i: (i, 0, 0), memory_space=pltpu.SMEM),
                  pl.BlockSpec(memory_space=pl.ANY)],
        out_specs=pl.BlockSpec((tg, d), lambda i: (i, 0)),
        out_shape=jax.ShapeDtypeStruct((r, d), src.dtype),
        scratch_shapes=[pltpu.SemaphoreType.DMA(())],
        compiler_params=_params("arbitrary"),
        name="gather_rows",
    )(idx.reshape(r // tg, 1, tg), src)


def _combine_kernel(idx_ref, src_ref, x_ref, info_ref, g_ref, b_ref, out_ref, buf_ref, sem):
    tc = out_ref.shape[0]

    def issue(t, carry):
        for j in range(TOP_K):
            pltpu.make_async_copy(src_ref.at[pl.ds(idx_ref[j, t], 1)], buf_ref.at[j, pl.ds(t, 1)],
                                  sem.at[j]).start()
        return carry

    lax.fori_loop(0, tc, issue, 0)
    for j in range(TOP_K):
        pltpu.make_async_copy(src_ref.at[pl.ds(0, tc)], buf_ref.at[j], sem.at[j]).wait()
    sub = buf_ref[0] * info_ref[:, R_PROB:R_PROB + 1]
    for j in range(1, TOP_K):
        sub = sub + buf_ref[j] * info_ref[:, R_PROB + j:R_PROB + j + 1]
    out_ref[...] = _deepnorm_rows(x_ref[...], sub, g_ref[...], b_ref[...])


def combine_deepnorm(ys, idx, x, info, g, b):
    nt, _, tc = idx.shape
    n, d = x.shape
    row = lambda i: (i, 0)
    fixed = lambda i: (0, 0)
    return pl.pallas_call(
        _combine_kernel,
        grid=(nt,),
        in_specs=[pl.BlockSpec((None, TOP_K, tc), lambda i: (i, 0, 0), memory_space=pltpu.SMEM),
                  pl.BlockSpec(memory_space=pl.ANY),
                  pl.BlockSpec((tc, d), row), pl.BlockSpec((tc, LANES), row),
                  pl.BlockSpec((1, d), fixed), pl.BlockSpec((1, d), fixed)],
        out_specs=pl.BlockSpec((tc, d), row),
        out_shape=jax.ShapeDtypeStruct((n, d), F32),
        scratch_shapes=[pltpu.VMEM((TOP_K, tc, d), F32), pltpu.SemaphoreType.DMA((TOP_K,))],
        compiler_params=_params("arbitrary"),
        name="combine_deepnorm",
    )(idx, ys, x, info, g, b)


def _expert_kernel(te_ref, tv_ref, x_ref, wg_ref, wu_ref, wd_ref, y_ref, acc_ref, xb_ref):
    i = pl.program_id(0)
    f = pl.program_id(1)

    @pl.when(f == 0)
    def _():
        acc_ref[...] = jnp.zeros_like(acc_ref)
        xb_ref[...] = x_ref[...].astype(BF16)

    @pl.when(tv_ref[i] != 0)
    def _():
        xb = xb_ref[...]
        hg = jnp.dot(xb, wg_ref[...], preferred_element_type=F32)
        hu = jnp.dot(xb, wu_ref[...], preferred_element_type=F32)
        h = (hg * _sigmoid(hg)) * hu
        acc_ref[...] += jnp.dot(h.astype(BF16), wd_ref[...], preferred_element_type=F32)

    @pl.when(f == pl.num_programs(1) - 1)
    def _():
        y_ref[...] = acc_ref[...]


def expert_swiglu(xs, tile_expert, tile_valid, wg, wu, wd, *, tm, tf=1792):
    r, d = xs.shape
    ff = wg.shape[2]
    grid_spec = pltpu.PrefetchScalarGridSpec(
        num_scalar_prefetch=2,
        grid=(r // tm, ff // tf),
        in_specs=[pl.BlockSpec((tm, d), lambda i, f, te, tv: (i, 0)),
                  pl.BlockSpec((None, d, tf), lambda i, f, te, tv: (te[i], 0, f)),
                  pl.BlockSpec((None, d, tf), lambda i, f, te, tv: (te[i], 0, f)),
                  pl.BlockSpec((None, tf, d), lambda i, f, te, tv: (te[i], f, 0))],
        out_specs=pl.BlockSpec((tm, d), lambda i, f, te, tv: (i, 0)),
        scratch_shapes=[pltpu.VMEM((tm, d), F32), pltpu.VMEM((tm, d), BF16)],
    )
    return pl.pallas_call(
        _expert_kernel,
        grid_spec=grid_spec,
        out_shape=jax.ShapeDtypeStruct((r, d), F32),
        compiler_params=_params("arbitrary", "arbitrary"),
        name="expert_swiglu",
    )(tile_expert, tile_valid, xs, wg, wu, wd)


def moe_top2_deepnorm(x, w_router, wg, wu, wd, g, b, *, tm=512, tc=256):
    n, d = x.shape
    ne = wg.shape[0]
    tc = min(tc, n)
    info = router_top2(x, w_router)
    e_flat = info[:, R_IDX:R_IDX + TOP_K].astype(jnp.int32).reshape(-1)

    onehot = (e_flat[:, None] == jnp.arange(ne, dtype=jnp.int32)[None, :]).astype(jnp.int32)
    csum = jnp.cumsum(onehot, axis=0)
    rank = jnp.sum((csum - 1) * onehot, axis=1)
    padded = ((csum[-1] + tm - 1) // tm) * tm
    ends = jnp.cumsum(padded)
    dest = (ends - padded)[e_flat] + rank
    n_rows = n * TOP_K + ne * tm
    n_tiles = n_rows // tm
    src_tok = jnp.zeros((n_rows,), jnp.int32).at[dest].set(jnp.arange(n * TOP_K, dtype=jnp.int32) // TOP_K)
    tile_start = jnp.arange(n_tiles, dtype=jnp.int32) * tm
    tile_expert = jnp.minimum(jnp.sum((tile_start[:, None] >= ends[None, :]).astype(jnp.int32), axis=1), ne - 1)
    tile_valid = (tile_start < ends[-1]).astype(jnp.int32)

    xs = gather_rows(x, src_tok, tg=tm)
    ys = expert_swiglu(xs, tile_expert, tile_valid, wg, wu, wd, tm=tm)
    pair_idx = jnp.transpose(dest.reshape(n // tc, tc, TOP_K), (0, 2, 1))
    return combine_deepnorm(ys, pair_idx, x, info, g, b)


def _ffn_kernel(x_ref, wg_ref, wu_ref, wd_ref, g_ref, b_ref, o_ref, acc_ref, xb_ref):
    f = pl.program_id(1)

    @pl.when(f == 0)
    def _():
        acc_ref[...] = jnp.zeros_like(acc_ref)
        xb_ref[...] = x_ref[...].astype(BF16)

    xb = xb_ref[...]
    hg = jnp.dot(xb, wg_ref[...], preferred_element_type=F32)
    hu = jnp.dot(xb, wu_ref[...], preferred_element_type=F32)
    h = (hg * _sigmoid(hg)) * hu
    acc_ref[...] += jnp.dot(h.astype(BF16), wd_ref[...], preferred_element_type=F32)

    @pl.when(f == pl.num_programs(1) - 1)
    def _():
        o_ref[...] = _deepnorm_rows(x_ref[...], acc_ref[...], g_ref[...], b_ref[...])


def ffn_deepnorm(x, wg, wu, wd, g, b, *, tm=512, tf=1792):
    n, d = x.shape
    ff = wg.shape[1]
    tm = min(tm, n)
    row = lambda i, f: (i, 0)
    fixed = lambda i, f: (0, 0)
    return pl.pallas_call(
        _ffn_kernel,
        grid=(n // tm, ff // tf),
        in_specs=[pl.BlockSpec((tm, d), row),
                  pl.BlockSpec((d, tf), lambda i, f: (0, f)),
                  pl.BlockSpec((d, tf), lambda i, f: (0, f)),
                  pl.BlockSpec((tf, d), lambda i, f: (f, 0)),
                  pl.BlockSpec((1, d), fixed), pl.BlockSpec((1, d), fixed)],
        out_specs=pl.BlockSpec((tm, d), row),
        out_shape=jax.ShapeDtypeStruct((n, d), F32),
        scratch_shapes=[pltpu.VMEM((tm, d), F32), pltpu.VMEM((tm, d), BF16)],
        compiler_params=_params("parallel", "arbitrary"),
        name="ffn_deepnorm",
    )(x, wg, wu, wd, g, b)


CONV_HISTORY = SUBLANES


def _ssd_kernel(xs_ref, b_ref, c_ref, z_ref, dt_ref, cw_ref, cb_ref, dtb_ref, alog_ref, dskip_ref, mnorm_ref,
                y_ref, state_ref, ext_ref, yacc_ref):
    l, hp = xs_ref.shape
    gn = b_ref.shape[1]
    p = M_HEAD_DIM
    ns = M_STATE
    nh = hp // p
    hpg = nh // M_GROUPS
    kc = cw_ref.shape[0]
    hist = CONV_HISTORY

    @pl.when(pl.program_id(1) == 0)
    def _():
        state_ref[...] = jnp.zeros_like(state_ref)
        ext_ref[0:hist, :] = jnp.zeros((hist, ext_ref.shape[1]), F32)

    ext_ref[hist:, 0:hp] = xs_ref[...]
    ext_ref[hist:, hp:hp + gn] = b_ref[...]
    ext_ref[hist:, hp + gn:] = c_ref[...]
    conv = cb_ref[...]
    for i in range(kc):
        conv = conv + cw_ref[i:i + 1, :] * ext_ref[pl.ds(hist - (kc - 1) + i, l), :]
    ext_ref[0:hist, :] = ext_ref[l:l + hist, :]
    xbc = conv * _sigmoid(conv)
    xs = xbc[:, :hp]

    dt_pre = dt_ref[...] + dtb_ref[...]
    dt = jnp.maximum(dt_pre, 0.0) + jnp.log(1.0 + jnp.exp(-jnp.abs(dt_pre)))
    a = -jnp.exp(alog_ref[...])

    row = lax.broadcasted_iota(jnp.int32, (l, l), 0)
    col = lax.broadcasted_iota(jnp.int32, (l, l), 1)
    causal = row >= col
    acs = jnp.dot(causal.astype(F32), dt * a, preferred_element_type=F32, precision=HIGHEST)
    acst = acs.T
    dtt = dt.T
    acs_last = acs[l - 1:l, :]
    w_end = jnp.exp(acs_last - acs) * dt
    exp_acs = jnp.exp(acs)
    exp_last = jnp.exp(acs_last)

    for g in range(M_GROUPS):
        b_g = xbc[:, hp + g * ns:hp + (g + 1) * ns]
        c_g = xbc[:, hp + gn + g * ns:hp + gn + (g + 1) * ns].astype(BF16)
        bt_g = b_g.T.astype(BF16)
        cb = jnp.dot(c_g, bt_g, preferred_element_type=F32)
        for hh in range(hpg):
            h = g * hpg + hh
            x_h = xs[:, h * p:(h + 1) * p]
            seg = acs[:, h:h + 1] - acst[h:h + 1, :]
            decay = jnp.where(causal, jnp.exp(seg), 0.0)
            m = cb * decay * dtt[h:h + 1, :]
            st = state_ref[h]
            y = jnp.dot(m.astype(BF16), x_h.astype(BF16), preferred_element_type=F32)
            y = y + jnp.dot(c_g, st.astype(BF16), preferred_element_type=F32) * exp_acs[:, h:h + 1]
            yacc_ref[:, h * p:(h + 1) * p] = y
            xw = (x_h * w_end[:, h:h + 1]).astype(BF16)
            state_ref[h] = st * exp_last[:, h:h + 1] + jnp.dot(bt_g, xw, preferred_element_type=F32)

    z = z_ref[...]
    y = (yacc_ref[...] + dskip_ref[...] * xs) * (z * _sigmoid(z))
    gw = hp // M_GROUPS
    for g in range(M_GROUPS):
        seg = y[:, g * gw:(g + 1) * gw]
        ms = jnp.mean(seg * seg, axis=-1, keepdims=True)
        y_ref[:, g * gw:(g + 1) * gw] = (seg * lax.rsqrt(ms + LN_EPS)
                                         * mnorm_ref[:, g * gw:(g + 1) * gw]).astype(y_ref.dtype)


def ssd_mixer(proj, cols, conv_w, conv_b, dt_bias, a_log, d_skip, m_norm):
    bsz, t, _ = proj.shape
    hp = m_norm.shape[1]
    gn = M_GROUPS * M_STATE
    conv_dim = hp + 2 * gn
    nh = hp // M_HEAD_DIM
    l = CHUNK

    def col_block(width, off):
        blk = off // width
        return pl.BlockSpec((None, l, width), lambda b, c: (b, c, blk))

    fixed = lambda b, c: (0, 0)
    return pl.pallas_call(
        _ssd_kernel,
        grid=(bsz, t // l),
        in_specs=[col_block(hp, cols["xs"]), col_block(gn, cols["B"]), col_block(gn, cols["C"]),
                  col_block(hp, cols["z"]), col_block(LANES, cols["dt"]),
                  pl.BlockSpec(conv_w.shape, fixed), pl.BlockSpec((1, conv_dim), fixed),
                  pl.BlockSpec((1, LANES), fixed), pl.BlockSpec((1, LANES), fixed),
                  pl.BlockSpec((1, hp), fixed), pl.BlockSpec((1, hp), fixed)],
        out_specs=pl.BlockSpec((None, l, hp), lambda b, c: (b, c, 0)),
        out_shape=jax.ShapeDtypeStruct((bsz, t, hp), BF16),
        scratch_shapes=[pltpu.VMEM((nh, M_STATE, M_HEAD_DIM), F32),
                        pltpu.VMEM((CONV_HISTORY + l, conv_dim), F32),
                        pltpu.VMEM((l, hp), F32)],
        compiler_params=_params("parallel", "arbitrary"),
        name="ssd_mixer",
    )(proj, proj, proj, proj, proj, conv_w, conv_b, dt_bias, a_log, d_skip, m_norm)


EXP_MINUS_HALF = float(np.exp(-0.5))


def _head_sum(x, ones):
    hi = x.astype(BF16)
    lo = (x - hi.astype(F32)).astype(BF16)
    outs = []
    for j in range(x.shape[1] // MXU_DIM):
        sl = slice(j * MXU_DIM, (j + 1) * MXU_DIM)
        outs.append(jnp.dot(hi[:, sl], ones, preferred_element_type=F32)
                    + jnp.dot(lo[:, sl], ones, preferred_element_type=F32))
    return jnp.concatenate(outs, axis=-1)


def _rwkv_prep_kernel(r_ref, k_ref, v_ref, lo_ref, mu_ref, w0_ref, a0_ref, kk_ref, ka_ref, rk_ref,
                      wup_ref, aup_ref, gup_ref, ones_ref,
                      ro_ref, wo_ref, ko_ref, vo_ref, ao_ref, bo_ref, bonus_ref, gate_ref, prev_ref):
    tl, c = r_ref.shape

    @pl.when(pl.program_id(1) == 0)
    def _():
        prev_ref[...] = jnp.zeros_like(prev_ref)

    first_row = lax.broadcasted_iota(jnp.int32, (tl, 1), 0) == 0

    def shift_mix(x_ref, off):
        x = x_ref[...]
        width = x.shape[1]
        prev = jnp.where(first_row, prev_ref[0:1, off:off + width], pltpu.roll(x, 1, 0))
        prev_ref[0:1, off:off + width] = x[tl - 1:tl, :]
        return x + (prev - x) * mu_ref[:, off:off + width]

    r = shift_mix(r_ref, 0)
    k = shift_mix(k_ref, c)
    v = shift_mix(v_ref, 2 * c)
    lo = shift_mix(lo_ref, 3 * c)
    lo_a = lo[:, 0:LANES]
    lo_g = lo[:, LORA_GATE_OFF:LORA_GATE_OFF + LORA_GATE_PAD]
    w = w0_ref[...] + jnp.dot(jnp.tanh(lo_a).astype(BF16), wup_ref[...], preferred_element_type=F32)
    log_decay = -EXP_MINUS_HALF * _sigmoid(w)
    iclr = _sigmoid(a0_ref[...] + jnp.dot(lo_a.astype(BF16), aup_ref[...], preferred_element_type=F32))
    gate = jnp.dot(_sigmoid(lo_g).astype(BF16), gup_ref[...], preferred_element_type=F32)
    ones = ones_ref[...]
    kk = k * kk_ref[...]
    kk = kk / jnp.maximum(jnp.sqrt(_head_sum(kk * kk, ones)), 1e-12)
    k2 = k * (1.0 + (iclr - 1.0) * ka_ref[...])
    ro_ref[...] = r
    wo_ref[...] = log_decay
    ko_ref[...] = k2
    vo_ref[...] = v
    ao_ref[...] = -kk
    bo_ref[...] = kk * iclr
    bonus_ref[...] = _head_sum(r * k2 * rk_ref[...], ones) * v
    gate_ref[...] = gate


def rwkv_prep(proj, cols, mu, w0, a0, k_k, k_a, r_k, w_up, a_up, g_up, *, tl=256):
    bsz, t, _ = proj.shape
    c = w0.shape[1]
    tl = min(tl, t)
    ones = jnp.asarray(np.kron(np.eye(MXU_DIM // R_HEAD_DIM), np.ones((R_HEAD_DIM, R_HEAD_DIM))), BF16)

    def col_block(width, off):
        blk = off // width
        return pl.BlockSpec((None, tl, width), lambda b, i: (b, i, blk))

    def fixed(shape):
        return pl.BlockSpec(shape, lambda b, i: (0, 0))

    out_spec = pl.BlockSpec((None, tl, c), lambda b, i: (b, i, 0))
    return pl.pallas_call(
        _rwkv_prep_kernel,
        grid=(bsz, t // tl),
        in_specs=[col_block(c, cols["r"]), col_block(c, cols["k"]), col_block(c, cols["v"]),
                  col_block(LORA_BLOCK, cols["lora"]), fixed(mu.shape)]
        + [fixed((1, c))] * 5 + [fixed(w_up.shape), fixed(a_up.shape), fixed(g_up.shape), fixed(ones.shape)],
        out_specs=[out_spec] * 8,
        out_shape=[jax.ShapeDtypeStruct((bsz, t, c), F32)] * 8,
        scratch_shapes=[pltpu.VMEM((SUBLANES, mu.shape[1]), F32)],
        compiler_params=_params("parallel", "arbitrary"),
        name="rwkv_prep",
    )(proj, proj, proj, proj, mu, w0, a0, k_k, k_a, r_k, w_up, a_up, g_up, ones)


RWKV_CHUNK = 64


def _nt(a, b):
    return lax.dot_general(a, b, (((1,), (1,)), ((), ())), preferred_element_type=F32)


def _tn(a, b):
    return lax.dot_general(a, b, (((0,), (0,)), ((), ())), preferred_element_type=F32)


def _rwkv_chunk_kernel(r_ref, lw_ref, k_ref, v_ref, a_ref, b_ref, bonus_ref, gate_ref, rlnw_ref, rlnb_ref,
                       ones_ref, y_ref, state_ref, yacc_ref):
    l, c = r_ref.shape
    hd = R_HEAD_DIM
    n_pair = c // LANES

    @pl.when(pl.program_id(1) == 0)
    def _():
        state_ref[...] = jnp.zeros_like(state_ref)

    row = lax.broadcasted_iota(jnp.int32, (l, l), 0)
    col = lax.broadcasted_iota(jnp.int32, (l, l), 1)
    lw = lw_ref[...]
    g = jnp.dot((row >= col).astype(F32), lw, preferred_element_type=F32, precision=HIGHEST)
    g_last = g[l - 1:l, :]
    e_g = jnp.exp(g)
    e_ng = jnp.exp(-g)
    e_last = jnp.exp(g_last)
    at_all = a_ref[...] * jnp.exp(g - lw)
    rt_all = r_ref[...] * e_g
    bt_all = b_ref[...] * e_ng
    kt_all = k_ref[...] * e_ng
    v_all = v_ref[...]

    wi = lax.broadcasted_iota(jnp.int32, (l, 2 * l), 0)
    wl = lax.broadcasted_iota(jnp.int32, (l, 2 * l), 1)
    wj = jnp.where(wl >= l, wl - l, wl)
    strict = wj < wi
    incl = wj <= wi
    eye_w = (wj == wi).astype(F32)
    first_w = wl < l
    first_c = lax.broadcasted_iota(jnp.int32, (l, LANES), 1) < hd
    bi = lax.broadcasted_iota(jnp.int32, (LANES, LANES), 0)
    bj = lax.broadcasted_iota(jnp.int32, (LANES, LANES), 1)
    same_head = (bi < hd) == (bj < hd)

    def block_rows(x, first):
        return jnp.concatenate([jnp.where(first, x, 0.0), jnp.where(first, 0.0, x)], axis=0)

    pairs = range(n_pair)
    lanes_of = [slice(p * LANES, (p + 1) * LANES) for p in pairs]
    dot = lambda x, y: jnp.dot(x, y, preferred_element_type=F32)
    ht = [state_ref[p] for p in pairs]
    lhs_ar = [jnp.concatenate([at_all[:, s], rt_all[:, s]], axis=0).astype(BF16) for s in lanes_of]
    gram = [_nt(lhs_ar[p], jnp.concatenate([block_rows(bt_all[:, s], first_c),
                                            block_rows(kt_all[:, s], first_c)], axis=0).astype(BF16))
            for p, s in zip(pairs, lanes_of)]
    h_ar = [_nt(lhs_ar[p], ht[p].astype(BF16)) for p in pairs]
    n_w = [jnp.where(strict, gram[p][0:l, 0:2 * l], 0.0) for p in pairs]
    t_w = [eye_w + n_w[p] for p in pairs]
    p_w = [dot(n_w[p].astype(BF16), block_rows(n_w[p], first_w).astype(BF16)) for p in pairs]
    for _ in range(int(np.log2(l)) - 1):
        res = [dot(jnp.concatenate([t_w[p], p_w[p]], axis=0).astype(BF16),
                   block_rows(p_w[p], first_w).astype(BF16)) for p in pairs]
        t_w = [t_w[p] + res[p][0:l] for p in pairs]
        p_w = [res[p][l:2 * l] for p in pairs]
    v_bd = [block_rows(v_all[:, s], first_c).astype(BF16) for s in lanes_of]
    rhs_u = [h_ar[p][0:l] + dot(jnp.where(strict, gram[p][0:l, 2 * l:4 * l], 0.0).astype(BF16), v_bd[p])
             for p in pairs]
    u = [dot(t_w[p].astype(BF16), block_rows(rhs_u[p], first_c).astype(BF16)) for p in pairs]
    for p, s in zip(pairs, lanes_of):
        rbk_w = jnp.where(jnp.concatenate([incl, incl], axis=1), gram[p][l:2 * l, :], 0.0)
        yacc_ref[:, s] = h_ar[p][l:2 * l] + dot(
            rbk_w.astype(BF16), jnp.concatenate([block_rows(u[p], first_c).astype(BF16), v_bd[p]], axis=0))
    for p, s in zip(pairs, lanes_of):
        e_l = e_last[:, s]
        upd = _tn(jnp.concatenate([u[p], v_all[:, s]], axis=0).astype(BF16),
                  jnp.concatenate([bt_all[:, s] * e_l, kt_all[:, s] * e_l], axis=0).astype(BF16))
        state_ref[p] = ht[p] * e_l + jnp.where(same_head, upd, 0.0)

    y = yacc_ref[...]
    ones = ones_ref[...]
    inv = 1.0 / hd
    mu = _head_sum(y, ones) * inv
    yc = y - mu
    var = _head_sum(yc * yc, ones) * inv
    yn = yc * lax.rsqrt(var + R_LN_EPS)
    y_ref[...] = (((yn * rlnw_ref[...] + rlnb_ref[...]) + bonus_ref[...]) * gate_ref[...]).astype(y_ref.dtype)


def rwkv_chunk_mixer(r, lw, k, v, a, b, bonus, gate, rln_w, rln_b):
    bsz, t, c = r.shape
    l = RWKV_CHUNK
    ones = jnp.asarray(np.kron(np.eye(MXU_DIM // R_HEAD_DIM), np.ones((R_HEAD_DIM, R_HEAD_DIM))), BF16)
    blk = pl.BlockSpec((None, l, c), lambda i, j: (i, j, 0))
    fixed = lambda shape: pl.BlockSpec(shape, lambda i, j: (0, 0))
    return pl.pallas_call(
        _rwkv_chunk_kernel,
        grid=(bsz, t // l),
        in_specs=[blk] * 8 + [fixed((1, c)), fixed((1, c)), fixed(ones.shape)],
        out_specs=blk,
        out_shape=jax.ShapeDtypeStruct((bsz, t, c), BF16),
        scratch_shapes=[pltpu.VMEM((c // LANES, LANES, LANES), F32), pltpu.VMEM((l, c), F32)],
        compiler_params=_params("parallel", "arbitrary"),
        name="rwkv_chunk_mixer",
    )(r, lw, k, v, a, b, bonus, gate, rln_w, rln_b, ones)


def _retention_kernel(q_ref, k_ref, v_ref, g_ref, pos_ref, freq_ref, y_ref, state_ref):
    l = q_ref.shape[0]
    dk = RET_QK_HEAD
    dv = RET_V_HEAD
    half = dk // 2

    @pl.when(pl.program_id(1) == 0)
    def _():
        state_ref[...] = jnp.zeros_like(state_ref)

    ang = pos_ref[...].astype(F32) * freq_ref[...]
    cos = jnp.cos(ang)
    sin = jnp.sin(ang)

    def rotate(u):
        ue = u[:, :half]
        uo = u[:, half:]
        return jnp.concatenate([ue * cos - uo * sin, uo * cos + ue * sin], axis=-1)

    row = lax.broadcasted_iota(jnp.int32, (l, l), 0)
    col = lax.broadcasted_iota(jnp.int32, (l, l), 1)
    rel = (row - col).astype(F32)
    causal = row >= col
    idx = lax.broadcasted_iota(jnp.int32, (l, 1), 0).astype(F32)
    for h in range(RET_HEADS):
        log_gamma = float(np.log(np.float32(1.0) - np.float32(2.0) ** np.float32(-5.0 - h)))
        intra = jnp.where(causal, jnp.exp(log_gamma * jnp.where(causal, rel, 0.0)), 0.0)
        q_decay = jnp.exp(log_gamma * (idx + 1.0))
        k_decay = jnp.exp(log_gamma * (l - 1.0 - idx))
        chunk_decay = float(np.exp(np.float32(log_gamma) * np.float32(l)))
        q_h = rotate(q_ref[:, h * dk:(h + 1) * dk]).astype(BF16)
        k_h = rotate(k_ref[:, h * dk:(h + 1) * dk]) * (dk ** -0.5)
        v_h = v_ref[:, h * dv:(h + 1) * dv].astype(BF16)
        s = lax.dot_general(q_h, k_h.astype(BF16), (((1,), (1,)), ((), ())),
                            preferred_element_type=F32) * intra
        st = state_ref[h]
        y = jnp.dot(s.astype(BF16), v_h, preferred_element_type=F32)
        y = y + jnp.dot(q_h, st.astype(BF16), preferred_element_type=F32) * q_decay
        kd = (k_h * k_decay).astype(BF16)
        state_ref[h] = st * chunk_decay + lax.dot_general(
            kd, v_h, (((0,), (0,)), ((), ())), preferred_element_type=F32)
        mu = jnp.mean(y, axis=-1, keepdims=True)
        yc = y - mu
        var = jnp.mean(yc * yc, axis=-1, keepdims=True)
        g_h = g_ref[:, h * dv:(h + 1) * dv]
        y_ref[:, h * dv:(h + 1) * dv] = (g_h * _sigmoid(g_h) * (yc * lax.rsqrt(var + LN_EPS))).astype(y_ref.dtype)


def retention_mixer(proj, positions, inv_freq):
    bsz, t, _ = proj.shape
    hk = RET_HEADS * RET_QK_HEAD
    hv = RET_HEADS * RET_V_HEAD
    l = CHUNK
    return pl.pallas_call(
        _retention_kernel,
        grid=(bsz, t // l),
        in_specs=[pl.BlockSpec((None, l, hk), lambda b, c: (b, c, 0)),
                  pl.BlockSpec((None, l, hk), lambda b, c: (b, c, 1)),
                  pl.BlockSpec((None, l, hv), lambda b, c: (b, c, 1)),
                  pl.BlockSpec((None, l, hv), lambda b, c: (b, c, 2)),
                  pl.BlockSpec((None, l, 1), lambda b, c: (b, c, 0)),
                  pl.BlockSpec((1, RET_QK_HEAD // 2), lambda b, c: (0, 0))],
        out_specs=pl.BlockSpec((None, l, hv), lambda b, c: (b, c, 0)),
        out_shape=jax.ShapeDtypeStruct((bsz, t, hv), BF16),
        scratch_shapes=[pltpu.VMEM((RET_HEADS, RET_QK_HEAD, RET_V_HEAD), F32)],
        compiler_params=_params("parallel", "arbitrary"),
        name="retention_mixer",
    )(proj, proj, proj, proj, positions, inv_freq)


def _even_layer(x, w_in, conv_w, conv_b, dt_bias, a_log, d_skip, m_norm, mu_shift, w0, w_up, a0, a_up,
                g_up, k_k, k_a, r_k, rln_w, rln_b, w_out, ln1_g, ln1_b, wg, wu, wd, ln2_g, ln2_b):
    bsz, t, d = x.shape
    n = bsz * t
    m_inner = m_norm.shape[0]
    m_heads = dt_bias.shape[0]
    c = w0.shape[0]
    r_heads = c // R_HEAD_DIM
    gn = M_GROUPS * M_STATE
    o_xbc = m_inner
    o_dt = o_xbc + m_inner + 2 * gn
    o_rw = o_dt + m_heads
    o_lora = o_rw + 3 * c
    n_lora = R_DECAY_LORA + R_AAA_LORA + R_GATE_LORA
    x2 = x.reshape(n, d)

    zeros = lambda k: jnp.zeros((d, k), w_in.dtype)
    w_in_p = jnp.concatenate([
        w_in[:, o_rw:o_lora],
        w_in[:, :m_inner],
        w_in[:, o_xbc:o_dt],
        w_in[:, o_lora:o_lora + n_lora], zeros(LORA_DT_OFF - n_lora),
        w_in[:, o_dt:o_rw], zeros(LORA_BLOCK - LORA_DT_OFF - m_heads)], axis=1).astype(BF16)
    cols = {"r": 0, "k": c, "v": 2 * c, "z": 3 * c, "xs": 3 * c + m_inner, "B": 3 * c + 2 * m_inner,
            "C": 3 * c + 2 * m_inner + gn, "lora": 3 * c + 2 * m_inner + 2 * gn}
    cols["dt"] = cols["lora"] + LORA_DT_OFF
    proj = matmul(x2, w_in_p).reshape(bsz, t, -1)

    pad_lane = lambda u: jnp.pad(u.reshape(1, -1), ((0, 0), (0, LANES - u.shape[0])))
    y_ssd = ssd_mixer(proj, cols, conv_w, conv_b.reshape(1, -1), pad_lane(dt_bias), pad_lane(a_log),
                      jnp.repeat(d_skip, M_HEAD_DIM).reshape(1, -1), m_norm.reshape(1, -1))

    mu_p = jnp.pad(mu_shift, (0, LORA_BLOCK - n_lora)).reshape(1, -1)
    w_up_p = jnp.pad(w_up, ((0, LANES - R_DECAY_LORA), (0, 0))).astype(BF16)
    a_up_p = jnp.pad(a_up, ((R_DECAY_LORA, LANES - R_DECAY_LORA - R_AAA_LORA), (0, 0))).astype(BF16)
    g_up_p = jnp.pad(g_up, ((0, LORA_GATE_PAD - R_GATE_LORA), (0, 0))).astype(BF16)
    row = lambda u: u.reshape(1, -1)
    r, lw, k, v, a, b, bonus, gate = rwkv_prep(proj, cols, mu_p, row(w0), row(a0), row(k_k), row(k_a), row(r_k),
                                               w_up_p, a_up_p, g_up_p)
    y_rwkv = rwkv_chunk_mixer(r, lw, k, v, a, b, bonus, gate, row(rln_w), row(rln_b))
    x2 = out_proj_deepnorm(y_ssd.reshape(n, m_inner), y_rwkv.reshape(n, c), w_out.astype(BF16), x2,
                           row(ln1_g), row(ln1_b))
    x2 = ffn_deepnorm(x2, wg.astype(BF16), wu.astype(BF16), wd.astype(BF16), row(ln2_g), row(ln2_b))
    return x2.reshape(bsz, t, d)


def _odd_layer(x, positions, w_in, w_out, ln1_g, ln1_b, router, wg, wu, wd, ln2_g, ln2_b):
    bsz, t, d = x.shape
    n = bsz * t
    ret_qk = RET_HEADS * RET_QK_HEAD
    x2 = x.reshape(n, d)
    head_perm = np.concatenate([np.arange(0, RET_QK_HEAD, 2), np.arange(1, RET_QK_HEAD, 2)])
    qk_perm = (np.arange(2 * RET_HEADS)[:, None] * RET_QK_HEAD + head_perm[None, :]).reshape(-1)
    w_in_p = jnp.concatenate([w_in[:, qk_perm], w_in[:, 2 * ret_qk:]], axis=1).astype(BF16)
    proj = matmul(x2, w_in_p).reshape(bsz, t, -1)
    inv_freq = (1.0 / (10000.0 ** jnp.linspace(0.0, 1.0, RET_QK_HEAD // 2, dtype=F32))).reshape(1, -1)
    a = retention_mixer(proj, positions.reshape(bsz, t, 1), inv_freq)
    row = lambda u: u.reshape(1, -1)
    x2 = out_proj_deepnorm(a.reshape(n, -1), None, w_out.astype(BF16), x2, row(ln1_g), row(ln1_b))
    x2 = moe_top2_deepnorm(x2, jnp.pad(router, ((0, 0), (0, LANES - router.shape[1]))),
                           wg.astype(BF16), wu.astype(BF16), wd.astype(BF16), row(ln2_g), row(ln2_b))
    return x2.reshape(bsz, t, d)


def kernel(x, positions, ev_w_in, ev_conv_w, ev_conv_b, ev_dt_bias, ev_a_log, ev_d_skip, ev_m_norm, ev_mu_shift, ev_w0, ev_w_up, ev_a0, ev_a_up, ev_g_up, ev_k_k, ev_k_a, ev_r_k, ev_rln_w, ev_rln_b, ev_w_out, ev_ln1_g, ev_ln1_b, ev_ffn_wg, ev_ffn_wu, ev_ffn_wd, ev_ln2_g, ev_ln2_b, od_w_in, od_w_out, od_ln1_g, od_ln1_b, od_router, od_moe_wg, od_moe_wu, od_moe_wd, od_ln2_g, od_ln2_b):
    for layer in range(DEPTH):
        i = layer // 2
        if layer % 2 == 0:
            x = _even_layer(x, ev_w_in[i], ev_conv_w[i], ev_conv_b[i], ev_dt_bias[i], ev_a_log[i], ev_d_skip[i],
                            ev_m_norm[i], ev_mu_shift[i], ev_w0[i], ev_w_up[i], ev_a0[i], ev_a_up[i], ev_g_up[i],
                            ev_k_k[i], ev_k_a[i], ev_r_k[i], ev_rln_w[i], ev_rln_b[i], ev_w_out[i],
                            ev_ln1_g[i], ev_ln1_b[i], ev_ffn_wg[i], ev_ffn_wu[i], ev_ffn_wd[i],
                            ev_ln2_g[i], ev_ln2_b[i])
        else:
            x = _odd_layer(x, positions, od_w_in[i], od_w_out[i], od_ln1_g[i], od_ln1_b[i], od_router[i],
                           od_moe_wg[i], od_moe_wu[i], od_moe_wd[i], od_ln2_g[i], od_ln2_b[i])
    return x
```

```python
import jax
import jax.numpy as jnp
import numpy as np
from jax import lax
from jax.experimental import pallas as pl
from jax.experimental.pallas import tpu as pltpu

F32 = jnp.float32
BF16 = jnp.bfloat16
HIGHEST = lax.Precision.HIGHEST

CHUNK = 128
M_HEAD_DIM = 64
M_GROUPS = 2
M_STATE = 128
R_HEAD_DIM = 64
R_DECAY_LORA = 64
R_AAA_LORA = 64
R_GATE_LORA = 160
R_LN_EPS = 64e-5
RET_HEADS = 4
RET_QK_HEAD = 256
RET_V_HEAD = 512
N_EXPERTS = 8
LN_EPS = 1e-5
DEPTH = 2
ALPHA = (2.0 * DEPTH) ** 0.25

LANES = 128
SUBLANES = 8
MXU_DIM = 256
VMEM_LIMIT_BYTES = 56 * 1024 * 1024

LORA_BLOCK = 512
LORA_GATE_OFF = 128
LORA_GATE_PAD = 256
LORA_DT_OFF = 384


def _params(*sem):
    return pltpu.CompilerParams(dimension_semantics=sem, vmem_limit_bytes=VMEM_LIMIT_BYTES)


def _sigmoid(x):
    return jax.nn.sigmoid(x)


def _matmul_kernel(x_ref, w_ref, o_ref, xb_ref):
    @pl.when(pl.program_id(1) == 0)
    def _():
        xb_ref[...] = x_ref[...].astype(BF16)

    o_ref[...] = jnp.dot(xb_ref[...], w_ref[...], preferred_element_type=F32).astype(o_ref.dtype)


def matmul(x, w, *, tm=1024, tn=1024, out_dtype=F32):
    n, k = x.shape
    m = w.shape[1]
    tm = min(tm, n)
    return pl.pallas_call(
        _matmul_kernel,
        grid=(n // tm, m // tn),
        in_specs=[pl.BlockSpec((tm, k), lambda i, j: (i, 0)),
                  pl.BlockSpec((k, tn), lambda i, j: (0, j))],
        out_specs=pl.BlockSpec((tm, tn), lambda i, j: (i, j)),
        out_shape=jax.ShapeDtypeStruct((n, m), out_dtype),
        scratch_shapes=[pltpu.VMEM((tm, k), BF16)],
        compiler_params=_params("parallel", "arbitrary"),
        name="matmul",
    )(x, w)


def _deepnorm_rows(resid, sub, g, b):
    y = ALPHA * resid + sub
    mu = jnp.mean(y, axis=-1, keepdims=True)
    yc = y - mu
    var = jnp.mean(yc * yc, axis=-1, keepdims=True)
    return yc * lax.rsqrt(var + LN_EPS) * g + b


def _out_proj_kernel(a1_ref, a2_ref, w_ref, x_ref, g_ref, b_ref, o_ref):
    kh = a1_ref.shape[1]
    sub = jnp.dot(a1_ref[...], w_ref[:kh, :], preferred_element_type=F32)
    sub = sub + jnp.dot(a2_ref[...], w_ref[kh:, :], preferred_element_type=F32)
    o_ref[...] = _deepnorm_rows(x_ref[...], sub, g_ref[...], b_ref[...])


def out_proj_deepnorm(a1, a2, w, x, g, b, *, tm=512):
    n, d = x.shape
    kh = w.shape[0] // 2
    tm = min(tm, n)
    row = lambda i: (i, 0)
    fixed = lambda i: (0, 0)
    if a2 is None:
        a2, second = a1, pl.BlockSpec((tm, kh), lambda i: (i, 1))
    else:
        second = pl.BlockSpec((tm, kh), row)
    return pl.pallas_call(
        _out_proj_kernel,
        grid=(n // tm,),
        in_specs=[pl.BlockSpec((tm, kh), row), second,
                  pl.BlockSpec((2 * kh, d), fixed), pl.BlockSpec((tm, d), row),
                  pl.BlockSpec((1, d), fixed), pl.BlockSpec((1, d), fixed)],
        out_specs=pl.BlockSpec((tm, d), row),
        out_shape=jax.ShapeDtypeStruct((n, d), F32),
        compiler_params=_params("parallel"),
        name="out_proj_deepnorm",
    )(a1, a2, w, x, g, b)


TOP_K = 2
R_IDX = 0
R_PROB = TOP_K


def _router_kernel(x_ref, wr_ref, info_ref):
    logits = jnp.dot(x_ref[...], wr_ref[...], preferred_element_type=F32, precision=HIGHEST)
    lane = lax.broadcasted_iota(jnp.int32, logits.shape, 1)
    neg = jnp.float32(-jnp.inf)
    lg = jnp.where(lane < N_EXPERTS, logits, neg)
    m1 = jnp.max(lg, axis=-1, keepdims=True)
    i1 = jnp.min(jnp.where(lg == m1, lane, LANES), axis=-1, keepdims=True)
    lg2 = jnp.where(lane == i1, neg, lg)
    m2 = jnp.max(lg2, axis=-1, keepdims=True)
    i2 = jnp.min(jnp.where(lg2 == m2, lane, LANES), axis=-1, keepdims=True)
    e2 = jnp.exp(m2 - m1)
    p1 = 1.0 / (1.0 + e2)
    p2 = e2 / (1.0 + e2)
    info = jnp.where(lane == R_IDX, i1.astype(F32), 0.0)
    info = jnp.where(lane == R_IDX + 1, i2.astype(F32), info)
    info = jnp.where(lane == R_PROB, p1, info)
    info = jnp.where(lane == R_PROB + 1, p2, info)
    info_ref[...] = info


def router_top2(x, w_router, *, tm=1024):
    n, d = x.shape
    tm = min(tm, n)
    return pl.pallas_call(
        _router_kernel,
        grid=(n // tm,),
        in_specs=[pl.BlockSpec((tm, d), lambda i: (i, 0)), pl.BlockSpec((d, LANES), lambda i: (0, 0))],
        out_specs=pl.BlockSpec((tm, LANES), lambda i: (i, 0)),
        out_shape=jax.ShapeDtypeStruct((n, LANES), F32),
        compiler_params=_params("parallel"),
        name="router_top2",
    )(x, w_router)


def _combine_kernel(x_ref, y0_ref, y1_ref, info_ref, g_ref, b_ref, out_ref):
    sub = y0_ref[...] * info_ref[:, R_PROB:R_PROB + 1] + y1_ref[...] * info_ref[:, R_PROB + 1:R_PROB + 2]
    out_ref[...] = _deepnorm_rows(x_ref[...], sub, g_ref[...], b_ref[...])


def combine_deepnorm(x, y_slots, info, g, b, *, tc=512):
    n, d = x.shape
    tc = min(tc, n)
    nt = n // tc
    row = lambda i: (i, 0)
    fixed = lambda i: (0, 0)
    return pl.pallas_call(
        _combine_kernel,
        grid=(nt,),
        in_specs=[pl.BlockSpec((tc, d), row), pl.BlockSpec((tc, d), row),
                  pl.BlockSpec((tc, d), lambda i: (i + nt, 0)), pl.BlockSpec((tc, LANES), row),
                  pl.BlockSpec((1, d), fixed), pl.BlockSpec((1, d), fixed)],
        out_specs=pl.BlockSpec((tc, d), row),
        out_shape=jax.ShapeDtypeStruct((n, d), F32),
        compiler_params=_params("parallel"),
        name="combine_deepnorm",
    )(x, y_slots, y_slots, info, g, b)


DMA_ISSUE_UNROLL = 8


def _start_row_copies(src_ref, src_row, dst_ref, dst_row, sem, n_rows):
    def issue(r, carry):
        pltpu.make_async_copy(src_ref.at[pl.ds(src_row(r), 1)], dst_ref.at[pl.ds(dst_row(r), 1)], sem).start()
        return carry

    lax.fori_loop(0, n_rows, issue, 0, unroll=DMA_ISSUE_UNROLL)


def _expert_kernel(te_ref, tv_ref, src_ref, src_next_ref, dst_ref, x_hbm, wg_ref, wu_ref, wd_ref, y_hbm,
                   xbuf_ref, ybuf_ref, acc_ref, xb_ref, xsem, ysem):
    i = pl.program_id(0)
    f = pl.program_id(1)
    n_i = pl.num_programs(0)
    tm = acc_ref.shape[0]
    slot = i % 2
    same = lambda r: r

    def gather(idx_ref, s):
        _start_row_copies(x_hbm, lambda r: idx_ref[0, r], xbuf_ref.at[s], same, xsem.at[s], tm)

    def wait_tile(src, dst, sem):
        pltpu.make_async_copy(src, dst, sem).wait()

    @pl.when(f == 0)
    def _():
        @pl.when(i == 0)
        def _():
            gather(src_ref, 0)

        wait_tile(x_hbm.at[pl.ds(0, tm)], xbuf_ref.at[slot], xsem.at[slot])

        @pl.when(i + 1 < n_i)
        def _():
            gather(src_next_ref, 1 - slot)

        acc_ref[...] = jnp.zeros_like(acc_ref)
        xb_ref[...] = xbuf_ref[slot].astype(BF16)

    @pl.when(tv_ref[i] != 0)
    def _():
        xb = xb_ref[...]
        hg = jnp.dot(xb, wg_ref[...], preferred_element_type=F32)
        hu = jnp.dot(xb, wu_ref[...], preferred_element_type=F32)
        h = (hg * _sigmoid(hg)) * hu
        acc_ref[...] += jnp.dot(h.astype(BF16), wd_ref[...], preferred_element_type=F32)

    @pl.when(f == pl.num_programs(1) - 1)
    def _():
        @pl.when(i > 0)
        def _():
            wait_tile(ybuf_ref, y_hbm.at[pl.ds(0, tm)], ysem)

        ybuf_ref[...] = acc_ref[...]
        _start_row_copies(ybuf_ref, same, y_hbm, lambda r: dst_ref[0, r], ysem, tm)

        @pl.when(i == n_i - 1)
        def _():
            wait_tile(ybuf_ref, y_hbm.at[pl.ds(0, tm)], ysem)


def expert_swiglu(x, src_rows, dst_rows, n_out_rows, tile_expert, tile_valid, wg, wu, wd, *, tm, tf=1792):
    d = x.shape[1]
    r = src_rows.shape[0]
    ff = wg.shape[2]
    n_tiles = r // tm
    idx_tile = lambda u: u.reshape(n_tiles, 1, tm)
    smem_tile = lambda index: pl.BlockSpec((None, 1, tm), index, memory_space=pltpu.SMEM)
    grid_spec = pltpu.PrefetchScalarGridSpec(
        num_scalar_prefetch=2,
        grid=(n_tiles, ff // tf),
        in_specs=[smem_tile(lambda i, f, te, tv: (i, 0, 0)),
                  smem_tile(lambda i, f, te, tv: (jnp.minimum(i + 1, n_tiles - 1), 0, 0)),
                  smem_tile(lambda i, f, te, tv: (i, 0, 0)),
                  pl.BlockSpec(memory_space=pl.ANY),
                  pl.BlockSpec((None, d, tf), lambda i, f, te, tv: (te[i], 0, f)),
                  pl.BlockSpec((None, d, tf), lambda i, f, te, tv: (te[i], 0, f)),
                  pl.BlockSpec((None, tf, d), lambda i, f, te, tv: (te[i], f, 0))],
        out_specs=pl.BlockSpec(memory_space=pl.ANY),
        scratch_shapes=[pltpu.VMEM((2, tm, d), F32), pltpu.VMEM((tm, d), F32), pltpu.VMEM((tm, d), F32),
                        pltpu.VMEM((tm, d), BF16), pltpu.SemaphoreType.DMA((2,)), pltpu.SemaphoreType.DMA(())],
    )
    return pl.pallas_call(
        _expert_kernel,
        grid_spec=grid_spec,
        out_shape=jax.ShapeDtypeStruct((n_out_rows, d), F32),
        compiler_params=_params("arbitrary", "arbitrary"),
        name="expert_swiglu",
    )(tile_expert, tile_valid, idx_tile(src_rows), idx_tile(src_rows), idx_tile(dst_rows), x, wg, wu, wd)


def moe_top2_deepnorm(x, w_router, wg, wu, wd, g, b, *, tm=512):
    n = x.shape[0]
    ne = wg.shape[0]
    info = router_top2(x, w_router)
    e_flat = info[:, R_IDX:R_IDX + TOP_K].astype(jnp.int32).reshape(-1)

    onehot = (e_flat[:, None] == jnp.arange(ne, dtype=jnp.int32)[None, :]).astype(jnp.int32)
    csum = jnp.cumsum(onehot, axis=0)
    rank = jnp.sum((csum - 1) * onehot, axis=1)
    padded = ((csum[-1] + tm - 1) // tm) * tm
    ends = jnp.cumsum(padded)
    dest = (ends - padded)[e_flat] + rank
    n_rows = n * TOP_K + ne * tm
    n_tiles = n_rows // tm
    assign = jnp.full((n_rows,), -1, jnp.int32).at[dest].set(jnp.arange(n * TOP_K, dtype=jnp.int32))
    token, slot = assign // TOP_K, assign % TOP_K
    src_rows = jnp.where(assign >= 0, token, 0)
    pad_rank = jnp.cumsum((assign < 0).astype(jnp.int32)) - 1
    dst_rows = jnp.where(assign >= 0, slot * n + token, n * TOP_K + pad_rank)
    tile_start = jnp.arange(n_tiles, dtype=jnp.int32) * tm
    tile_expert = jnp.minimum(jnp.sum((tile_start[:, None] >= ends[None, :]).astype(jnp.int32), axis=1), ne - 1)
    tile_valid = (tile_start < ends[-1]).astype(jnp.int32)

    y_slots = expert_swiglu(x, src_rows, dst_rows, n_rows, tile_expert, tile_valid, wg, wu, wd, tm=tm)
    return combine_deepnorm(x, y_slots, info, g, b)


def _ffn_kernel(x_ref, wg_ref, wu_ref, wd_ref, g_ref, b_ref, o_ref, acc_ref, xb_ref):
    f = pl.program_id(1)

    @pl.when(f == 0)
    def _():
        acc_ref[...] = jnp.zeros_like(acc_ref)
        xb_ref[...] = x_ref[...].astype(BF16)

    xb = xb_ref[...]
    hg = jnp.dot(xb, wg_ref[...], preferred_element_type=F32)
    hu = jnp.dot(xb, wu_ref[...], preferred_element_type=F32)
    h = (hg * _sigmoid(hg)) * hu
    acc_ref[...] += jnp.dot(h.astype(BF16), wd_ref[...], preferred_element_type=F32)

    @pl.when(f == pl.num_programs(1) - 1)
    def _():
        o_ref[...] = _deepnorm_rows(x_ref[...], acc_ref[...], g_ref[...], b_ref[...])


def ffn_deepnorm(x, wg, wu, wd, g, b, *, tm=512, tf=1792):
    n, d = x.shape
    ff = wg.shape[1]
    tm = min(tm, n)
    row = lambda i, f: (i, 0)
    fixed = lambda i, f: (0, 0)
    return pl.pallas_call(
        _ffn_kernel,
        grid=(n // tm, ff // tf),
        in_specs=[pl.BlockSpec((tm, d), row),
                  pl.BlockSpec((d, tf), lambda i, f: (0, f)),
                  pl.BlockSpec((d, tf), lambda i, f: (0, f)),
                  pl.BlockSpec((tf, d), lambda i, f: (f, 0)),
                  pl.BlockSpec((1, d), fixed), pl.BlockSpec((1, d), fixed)],
        out_specs=pl.BlockSpec((tm, d), row),
        out_shape=jax.ShapeDtypeStruct((n, d), F32),
        scratch_shapes=[pltpu.VMEM((tm, d), F32), pltpu.VMEM((tm, d), BF16)],
        compiler_params=_params("parallel", "arbitrary"),
        name="ffn_deepnorm",
    )(x, wg, wu, wd, g, b)


CONV_HISTORY = SUBLANES


def _ssd_kernel(xs_ref, b_ref, c_ref, z_ref, dt_ref, cw_ref, cb_ref, dtb_ref, alog_ref, dskip_ref, mnorm_ref,
                y_ref, state_ref, ext_ref, yacc_ref):
    l, hp = xs_ref.shape
    gn = b_ref.shape[1]
    p = M_HEAD_DIM
    ns = M_STATE
    nh = hp // p
    hpg = nh // M_GROUPS
    kc = cw_ref.shape[0]
    hist = CONV_HISTORY

    @pl.when(pl.program_id(1) == 0)
    def _():
        state_ref[...] = jnp.zeros_like(state_ref)
        ext_ref[0:hist, :] = jnp.zeros((hist, ext_ref.shape[1]), F32)

    ext_ref[hist:, 0:hp] = xs_ref[...]
    ext_ref[hist:, hp:hp + gn] = b_ref[...]
    ext_ref[hist:, hp + gn:] = c_ref[...]
    conv = cb_ref[...]
    for i in range(kc):
        conv = conv + cw_ref[i:i + 1, :] * ext_ref[pl.ds(hist - (kc - 1) + i, l), :]
    ext_ref[0:hist, :] = ext_ref[l:l + hist, :]
    xbc = conv * _sigmoid(conv)
    xs = xbc[:, :hp]

    dt_pre = dt_ref[...] + dtb_ref[...]
    dt = jnp.maximum(dt_pre, 0.0) + jnp.log(1.0 + jnp.exp(-jnp.abs(dt_pre)))
    a = -jnp.exp(alog_ref[...])

    row = lax.broadcasted_iota(jnp.int32, (l, l), 0)
    col = lax.broadcasted_iota(jnp.int32, (l, l), 1)
    causal = row >= col
    acs = jnp.dot(causal.astype(F32), dt * a, preferred_element_type=F32, precision=HIGHEST)
    acst = acs.T
    dtt = dt.T
    acs_last = acs[l - 1:l, :]
    w_end = jnp.exp(acs_last - acs) * dt
    exp_acs = jnp.exp(acs)
    exp_last = jnp.exp(acs_last)

    for g in range(M_GROUPS):
        b_g = xbc[:, hp + g * ns:hp + (g + 1) * ns]
        c_g = xbc[:, hp + gn + g * ns:hp + gn + (g + 1) * ns].astype(BF16)
        bt_g = b_g.T.astype(BF16)
        cb = jnp.dot(c_g, bt_g, preferred_element_type=F32)
        for hh in range(hpg):
            h = g * hpg + hh
            x_h = xs[:, h * p:(h + 1) * p]
            seg = acs[:, h:h + 1] - acst[h:h + 1, :]
            decay = jnp.where(causal, jnp.exp(seg), 0.0)
            m = cb * decay * dtt[h:h + 1, :]
            st = state_ref[h]
            y = jnp.dot(m.astype(BF16), x_h.astype(BF16), preferred_element_type=F32)
            y = y + jnp.dot(c_g, st.astype(BF16), preferred_element_type=F32) * exp_acs[:, h:h + 1]
            yacc_ref[:, h * p:(h + 1) * p] = y
            xw = (x_h * w_end[:, h:h + 1]).astype(BF16)
            state_ref[h] = st * exp_last[:, h:h + 1] + jnp.dot(bt_g, xw, preferred_element_type=F32)

    z = z_ref[...]
    y = (yacc_ref[...] + dskip_ref[...] * xs) * (z * _sigmoid(z))
    gw = hp // M_GROUPS
    for g in range(M_GROUPS):
        seg = y[:, g * gw:(g + 1) * gw]
        ms = jnp.mean(seg * seg, axis=-1, keepdims=True)
        y_ref[:, g * gw:(g + 1) * gw] = (seg * lax.rsqrt(ms + LN_EPS)
                                         * mnorm_ref[:, g * gw:(g + 1) * gw]).astype(y_ref.dtype)


def ssd_mixer(proj, cols, conv_w, conv_b, dt_bias, a_log, d_skip, m_norm):
    bsz, t, _ = proj.shape
    hp = m_norm.shape[1]
    gn = M_GROUPS * M_STATE
    conv_dim = hp + 2 * gn
    nh = hp // M_HEAD_DIM
    l = CHUNK

    def col_block(width, off):
        blk = off // width
        return pl.BlockSpec((None, l, width), lambda b, c: (b, c, blk))

    fixed = lambda b, c: (0, 0)
    return pl.pallas_call(
        _ssd_kernel,
        grid=(bsz, t // l),
        in_specs=[col_block(hp, cols["xs"]), col_block(gn, cols["B"]), col_block(gn, cols["C"]),
                  col_block(hp, cols["z"]), col_block(LANES, cols["dt"]),
                  pl.BlockSpec(conv_w.shape, fixed), pl.BlockSpec((1, conv_dim), fixed),
                  pl.BlockSpec((1, LANES), fixed), pl.BlockSpec((1, LANES), fixed),
                  pl.BlockSpec((1, hp), fixed), pl.BlockSpec((1, hp), fixed)],
        out_specs=pl.BlockSpec((None, l, hp), lambda b, c: (b, c, 0)),
        out_shape=jax.ShapeDtypeStruct((bsz, t, hp), BF16),
        scratch_shapes=[pltpu.VMEM((nh, M_STATE, M_HEAD_DIM), F32),
                        pltpu.VMEM((CONV_HISTORY + l, conv_dim), F32),
                        pltpu.VMEM((l, hp), F32)],
        compiler_params=_params("parallel", "arbitrary"),
        name="ssd_mixer",
    )(proj, proj, proj, proj, proj, conv_w, conv_b, dt_bias, a_log, d_skip, m_norm)


EXP_MINUS_HALF = float(np.exp(-0.5))


def _head_sum(x, ones):
    hi = x.astype(BF16)
    lo = (x - hi.astype(F32)).astype(BF16)
    outs = []
    for j in range(x.shape[1] // MXU_DIM):
        sl = slice(j * MXU_DIM, (j + 1) * MXU_DIM)
        outs.append(jnp.dot(hi[:, sl], ones, preferred_element_type=F32)
                    + jnp.dot(lo[:, sl], ones, preferred_element_type=F32))
    return jnp.concatenate(outs, axis=-1)


def _rwkv_prep_kernel(r_ref, k_ref, v_ref, lo_ref, mu_ref, w0_ref, a0_ref, kk_ref, ka_ref, rk_ref,
                      wup_ref, aup_ref, gup_ref, ones_ref,
                      ro_ref, wo_ref, ko_ref, vo_ref, ao_ref, bo_ref, bonus_ref, gate_ref, prev_ref):
    tl, c = r_ref.shape

    @pl.when(pl.program_id(1) == 0)
    def _():
        prev_ref[...] = jnp.zeros_like(prev_ref)

    first_row = lax.broadcasted_iota(jnp.int32, (tl, 1), 0) == 0

    def shift_mix(x_ref, off):
        x = x_ref[...]
        width = x.shape[1]
        prev = jnp.where(first_row, prev_ref[0:1, off:off + width], pltpu.roll(x, 1, 0))
        prev_ref[0:1, off:off + width] = x[tl - 1:tl, :]
        return x + (prev - x) * mu_ref[:, off:off + width]

    r = shift_mix(r_ref, 0)
    k = shift_mix(k_ref, c)
    v = shift_mix(v_ref, 2 * c)
    lo = shift_mix(lo_ref, 3 * c)
    lo_a = lo[:, 0:LANES]
    lo_g = lo[:, LORA_GATE_OFF:LORA_GATE_OFF + LORA_GATE_PAD]
    w = w0_ref[...] + jnp.dot(jnp.tanh(lo_a).astype(BF16), wup_ref[...], preferred_element_type=F32)
    log_decay = -EXP_MINUS_HALF * _sigmoid(w)
    iclr = _sigmoid(a0_ref[...] + jnp.dot(lo_a.astype(BF16), aup_ref[...], preferred_element_type=F32))
    gate = jnp.dot(_sigmoid(lo_g).astype(BF16), gup_ref[...], preferred_element_type=F32)
    ones = ones_ref[...]
    kk = k * kk_ref[...]
    kk = kk / jnp.maximum(jnp.sqrt(_head_sum(kk * kk, ones)), 1e-12)
    k2 = k * (1.0 + (iclr - 1.0) * ka_ref[...])
    ro_ref[...] = r
    wo_ref[...] = log_decay
    ko_ref[...] = k2
    vo_ref[...] = v
    ao_ref[...] = -kk
    bo_ref[...] = kk * iclr
    bonus_ref[...] = _head_sum(r * k2 * rk_ref[...], ones) * v
    gate_ref[...] = gate


def rwkv_prep(proj, cols, mu, w0, a0, k_k, k_a, r_k, w_up, a_up, g_up, *, tl=256):
    bsz, t, _ = proj.shape
    c = w0.shape[1]
    tl = min(tl, t)
    ones = jnp.asarray(np.kron(np.eye(MXU_DIM // R_HEAD_DIM), np.ones((R_HEAD_DIM, R_HEAD_DIM))), BF16)

    def col_block(width, off):
        blk = off // width
        return pl.BlockSpec((None, tl, width), lambda b, i: (b, i, blk))

    def fixed(shape):
        return pl.BlockSpec(shape, lambda b, i: (0, 0))

    out_spec = pl.BlockSpec((None, tl, c), lambda b, i: (b, i, 0))
    return pl.pallas_call(
        _rwkv_prep_kernel,
        grid=(bsz, t // tl),
        in_specs=[col_block(c, cols["r"]), col_block(c, cols["k"]), col_block(c, cols["v"]),
                  col_block(LORA_BLOCK, cols["lora"]), fixed(mu.shape)]
        + [fixed((1, c))] * 5 + [fixed(w_up.shape), fixed(a_up.shape), fixed(g_up.shape), fixed(ones.shape)],
        out_specs=[out_spec] * 8,
        out_shape=[jax.ShapeDtypeStruct((bsz, t, c), F32)] * 8,
        scratch_shapes=[pltpu.VMEM((SUBLANES, mu.shape[1]), F32)],
        compiler_params=_params("parallel", "arbitrary"),
        name="rwkv_prep",
    )(proj, proj, proj, proj, mu, w0, a0, k_k, k_a, r_k, w_up, a_up, g_up, ones)


RWKV_CHUNK = 64


def _nt(a, b):
    return lax.dot_general(a, b, (((1,), (1,)), ((), ())), preferred_element_type=F32)


def _tn(a, b):
    return lax.dot_general(a, b, (((0,), (0,)), ((), ())), preferred_element_type=F32)


def _rwkv_chunk_kernel(r_ref, lw_ref, k_ref, v_ref, a_ref, b_ref, bonus_ref, gate_ref, rlnw_ref, rlnb_ref,
                       ones_ref, y_ref, state_ref, yacc_ref):
    l, c = r_ref.shape
    hd = R_HEAD_DIM
    n_pair = c // LANES

    @pl.when(pl.program_id(1) == 0)
    def _():
        state_ref[...] = jnp.zeros_like(state_ref)

    row = lax.broadcasted_iota(jnp.int32, (l, l), 0)
    col = lax.broadcasted_iota(jnp.int32, (l, l), 1)
    lw = lw_ref[...]
    g = jnp.dot((row >= col).astype(F32), lw, preferred_element_type=F32, precision=HIGHEST)
    g_last = g[l - 1:l, :]
    e_g = jnp.exp(g)
    e_ng = jnp.exp(-g)
    e_last = jnp.exp(g_last)
    at_all = a_ref[...] * jnp.exp(g - lw)
    rt_all = r_ref[...] * e_g
    bt_all = b_ref[...] * e_ng
    kt_all = k_ref[...] * e_ng
    v_all = v_ref[...]

    wi = lax.broadcasted_iota(jnp.int32, (l, 2 * l), 0)
    wl = lax.broadcasted_iota(jnp.int32, (l, 2 * l), 1)
    wj = jnp.where(wl >= l, wl - l, wl)
    strict = wj < wi
    incl = wj <= wi
    eye_w = (wj == wi).astype(F32)
    first_w = wl < l
    first_c = lax.broadcasted_iota(jnp.int32, (l, LANES), 1) < hd
    bi = lax.broadcasted_iota(jnp.int32, (LANES, LANES), 0)
    bj = lax.broadcasted_iota(jnp.int32, (LANES, LANES), 1)
    same_head = (bi < hd) == (bj < hd)

    def block_rows(x, first):
        return jnp.concatenate([jnp.where(first, x, 0.0), jnp.where(first, 0.0, x)], axis=0)

    pairs = range(n_pair)
    lanes_of = [slice(p * LANES, (p + 1) * LANES) for p in pairs]
    dot = lambda x, y: jnp.dot(x, y, preferred_element_type=F32)
    ht = [state_ref[p] for p in pairs]
    lhs_ar = [jnp.concatenate([at_all[:, s], rt_all[:, s]], axis=0).astype(BF16) for s in lanes_of]
    gram = [_nt(lhs_ar[p], jnp.concatenate([block_rows(bt_all[:, s], first_c),
                                            block_rows(kt_all[:, s], first_c)], axis=0).astype(BF16))
            for p, s in zip(pairs, lanes_of)]
    h_ar = [_nt(lhs_ar[p], ht[p].astype(BF16)) for p in pairs]
    n_w = [jnp.where(strict, gram[p][0:l, 0:2 * l], 0.0) for p in pairs]
    t_w = [eye_w + n_w[p] for p in pairs]
    p_w = [dot(n_w[p].astype(BF16), block_rows(n_w[p], first_w).astype(BF16)) for p in pairs]
    for _ in range(int(np.log2(l)) - 1):
        res = [dot(jnp.concatenate([t_w[p], p_w[p]], axis=0).astype(BF16),
                   block_rows(p_w[p], first_w).astype(BF16)) for p in pairs]
        t_w = [t_w[p] + res[p][0:l] for p in pairs]
        p_w = [res[p][l:2 * l] for p in pairs]
    v_bd = [block_rows(v_all[:, s], first_c).astype(BF16) for s in lanes_of]
    rhs_u = [h_ar[p][0:l] + dot(jnp.where(strict, gram[p][0:l, 2 * l:4 * l], 0.0).astype(BF16), v_bd[p])
             for p in pairs]
    u = [dot(t_w[p].astype(BF16), block_rows(rhs_u[p], first_c).astype(BF16)) for p in pairs]
    for p, s in zip(pairs, lanes_of):
        rbk_w = jnp.where(jnp.concatenate([incl, incl], axis=1), gram[p][l:2 * l, :], 0.0)
        yacc_ref[:, s] = h_ar[p][l:2 * l] + dot(
            rbk_w.astype(BF16), jnp.concatenate([block_rows(u[p], first_c).astype(BF16), v_bd[p]], axis=0))
    for p, s in zip(pairs, lanes_of):
        e_l = e_last[:, s]
        upd = _tn(jnp.concatenate([u[p], v_all[:, s]], axis=0).astype(BF16),
                  jnp.concatenate([bt_all[:, s] * e_l, kt_all[:, s] * e_l], axis=0).astype(BF16))
        state_ref[p] = ht[p] * e_l + jnp.where(same_head, upd, 0.0)

    y = yacc_ref[...]
    ones = ones_ref[...]
    inv = 1.0 / hd
    mu = _head_sum(y, ones) * inv
    yc = y - mu
    var = _head_sum(yc * yc, ones) * inv
    yn = yc * lax.rsqrt(var + R_LN_EPS)
    y_ref[...] = (((yn * rlnw_ref[...] + rlnb_ref[...]) + bonus_ref[...]) * gate_ref[...]).astype(y_ref.dtype)


def rwkv_chunk_mixer(r, lw, k, v, a, b, bonus, gate, rln_w, rln_b):
    bsz, t, c = r.shape
    l = RWKV_CHUNK
    ones = jnp.asarray(np.kron(np.eye(MXU_DIM // R_HEAD_DIM), np.ones((R_HEAD_DIM, R_HEAD_DIM))), BF16)
    blk = pl.BlockSpec((None, l, c), lambda i, j: (i, j, 0))
    fixed = lambda shape: pl.BlockSpec(shape, lambda i, j: (0, 0))
    return pl.pallas_call(
        _rwkv_chunk_kernel,
        grid=(bsz, t // l),
        in_specs=[blk] * 8 + [fixed((1, c)), fixed((1, c)), fixed(ones.shape)],
        out_specs=blk,
        out_shape=jax.ShapeDtypeStruct((bsz, t, c), BF16),
        scratch_shapes=[pltpu.VMEM((c // LANES, LANES, LANES), F32), pltpu.VMEM((l, c), F32)],
        compiler_params=_params("parallel", "arbitrary"),
        name="rwkv_chunk_mixer",
    )(r, lw, k, v, a, b, bonus, gate, rln_w, rln_b, ones)


def _retention_kernel(q_ref, k_ref, v_ref, g_ref, pos_ref, freq_ref, y_ref, state_ref):
    l = q_ref.shape[0]
    dk = RET_QK_HEAD
    dv = RET_V_HEAD
    half = dk // 2

    @pl.when(pl.program_id(1) == 0)
    def _():
        state_ref[...] = jnp.zeros_like(state_ref)

    ang = pos_ref[...].astype(F32) * freq_ref[...]
    cos = jnp.cos(ang)
    sin = jnp.sin(ang)

    def rotate(u):
        ue = u[:, :half]
        uo = u[:, half:]
        return jnp.concatenate([ue * cos - uo * sin, uo * cos + ue * sin], axis=-1)

    row = lax.broadcasted_iota(jnp.int32, (l, l), 0)
    col = lax.broadcasted_iota(jnp.int32, (l, l), 1)
    rel = (row - col).astype(F32)
    causal = row >= col
    idx = lax.broadcasted_iota(jnp.int32, (l, 1), 0).astype(F32)
    for h in range(RET_HEADS):
        log_gamma = float(np.log(np.float32(1.0) - np.float32(2.0) ** np.float32(-5.0 - h)))
        intra = jnp.where(causal, jnp.exp(log_gamma * jnp.where(causal, rel, 0.0)), 0.0)
        q_decay = jnp.exp(log_gamma * (idx + 1.0))
        k_decay = jnp.exp(log_gamma * (l - 1.0 - idx))
        chunk_decay = float(np.exp(np.float32(log_gamma) * np.float32(l)))
        q_h = rotate(q_ref[:, h * dk:(h + 1) * dk]).astype(BF16)
        k_h = rotate(k_ref[:, h * dk:(h + 1) * dk]) * (dk ** -0.5)
        v_h = v_ref[:, h * dv:(h + 1) * dv].astype(BF16)
        s = lax.dot_general(q_h, k_h.astype(BF16), (((1,), (1,)), ((), ())),
                            preferred_element_type=F32) * intra
        st = state_ref[h]
        y = jnp.dot(s.astype(BF16), v_h, preferred_element_type=F32)
        y = y + jnp.dot(q_h, st.astype(BF16), preferred_element_type=F32) * q_decay
        kd = (k_h * k_decay).astype(BF16)
        state_ref[h] = st * chunk_decay + lax.dot_general(
            kd, v_h, (((0,), (0,)), ((), ())), preferred_element_type=F32)
        mu = jnp.mean(y, axis=-1, keepdims=True)
        yc = y - mu
        var = jnp.mean(yc * yc, axis=-1, keepdims=True)
        g_h = g_ref[:, h * dv:(h + 1) * dv]
        y_ref[:, h * dv:(h + 1) * dv] = (g_h * _sigmoid(g_h) * (yc * lax.rsqrt(var + LN_EPS))).astype(y_ref.dtype)


def retention_mixer(proj, positions, inv_freq):
    bsz, t, _ = proj.shape
    hk = RET_HEADS * RET_QK_HEAD
    hv = RET_HEADS * RET_V_HEAD
    l = CHUNK
    return pl.pallas_call(
        _retention_kernel,
        grid=(bsz, t // l),
        in_specs=[pl.BlockSpec((None, l, hk), lambda b, c: (b, c, 0)),
                  pl.BlockSpec((None, l, hk), lambda b, c: (b, c, 1)),
                  pl.BlockSpec((None, l, hv), lambda b, c: (b, c, 1)),
                  pl.BlockSpec((None, l, hv), lambda b, c: (b, c, 2)),
                  pl.BlockSpec((None, l, 1), lambda b, c: (b, c, 0)),
                  pl.BlockSpec((1, RET_QK_HEAD // 2), lambda b, c: (0, 0))],
        out_specs=pl.BlockSpec((None, l, hv), lambda b, c: (b, c, 0)),
        out_shape=jax.ShapeDtypeStruct((bsz, t, hv), BF16),
        scratch_shapes=[pltpu.VMEM((RET_HEADS, RET_QK_HEAD, RET_V_HEAD), F32)],
        compiler_params=_params("parallel", "arbitrary"),
        name="retention_mixer",
    )(proj, proj, proj, proj, positions, inv_freq)


def _even_layer(x, w_in, conv_w, conv_b, dt_bias, a_log, d_skip, m_norm, mu_shift, w0, w_up, a0, a_up,
                g_up, k_k, k_a, r_k, rln_w, rln_b, w_out, ln1_g, ln1_b, wg, wu, wd, ln2_g, ln2_b):
    bsz, t, d = x.shape
    n = bsz * t
    m_inner = m_norm.shape[0]
    m_heads = dt_bias.shape[0]
    c = w0.shape[0]
    r_heads = c // R_HEAD_DIM
    gn = M_GROUPS * M_STATE
    o_xbc = m_inner
    o_dt = o_xbc + m_inner + 2 * gn
    o_rw = o_dt + m_heads
    o_lora = o_rw + 3 * c
    n_lora = R_DECAY_LORA + R_AAA_LORA + R_GATE_LORA
    x2 = x.reshape(n, d)

    zeros = lambda k: jnp.zeros((d, k), w_in.dtype)
    w_in_p = jnp.concatenate([
        w_in[:, o_rw:o_lora],
        w_in[:, :m_inner],
        w_in[:, o_xbc:o_dt],
        w_in[:, o_lora:o_lora + n_lora], zeros(LORA_DT_OFF - n_lora),
        w_in[:, o_dt:o_rw], zeros(LORA_BLOCK - LORA_DT_OFF - m_heads)], axis=1).astype(BF16)
    cols = {"r": 0, "k": c, "v": 2 * c, "z": 3 * c, "xs": 3 * c + m_inner, "B": 3 * c + 2 * m_inner,
            "C": 3 * c + 2 * m_inner + gn, "lora": 3 * c + 2 * m_inner + 2 * gn}
    cols["dt"] = cols["lora"] + LORA_DT_OFF
    proj = matmul(x2, w_in_p).reshape(bsz, t, -1)

    pad_lane = lambda u: jnp.pad(u.reshape(1, -1), ((0, 0), (0, LANES - u.shape[0])))
    y_ssd = ssd_mixer(proj, cols, conv_w, conv_b.reshape(1, -1), pad_lane(dt_bias), pad_lane(a_log),
                      jnp.repeat(d_skip, M_HEAD_DIM).reshape(1, -1), m_norm.reshape(1, -1))

    mu_p = jnp.pad(mu_shift, (0, LORA_BLOCK - n_lora)).reshape(1, -1)
    w_up_p = jnp.pad(w_up, ((0, LANES - R_DECAY_LORA), (0, 0))).astype(BF16)
    a_up_p = jnp.pad(a_up, ((R_DECAY_LORA, LANES - R_DECAY_LORA - R_AAA_LORA), (0, 0))).astype(BF16)
    g_up_p = jnp.pad(g_up, ((0, LORA_GATE_PAD - R_GATE_LORA), (0, 0))).astype(BF16)
    row = lambda u: u.reshape(1, -1)
    r, lw, k, v, a, b, bonus, gate = rwkv_prep(proj, cols, mu_p, row(w0), row(a0), row(k_k), row(k_a), row(r_k),
                                               w_up_p, a_up_p, g_up_p)
    y_rwkv = rwkv_chunk_mixer(r, lw, k, v, a, b, bonus, gate, row(rln_w), row(rln_b))
    x2 = out_proj_deepnorm(y_ssd.reshape(n, m_inner), y_rwkv.reshape(n, c), w_out.astype(BF16), x2,
                           row(ln1_g), row(ln1_b))
    x2 = ffn_deepnorm(x2, wg.astype(BF16), wu.astype(BF16), wd.astype(BF16), row(ln2_g), row(ln2_b))
    return x2.reshape(bsz, t, d)


def _odd_layer(x, positions, w_in, w_out, ln1_g, ln1_b, router, wg, wu, wd, ln2_g, ln2_b):
    bsz, t, d = x.shape
    n = bsz * t
    ret_qk = RET_HEADS * RET_QK_HEAD
    x2 = x.reshape(n, d)
    head_perm = np.concatenate([np.arange(0, RET_QK_HEAD, 2), np.arange(1, RET_QK_HEAD, 2)])
    qk_perm = (np.arange(2 * RET_HEADS)[:, None] * RET_QK_HEAD + head_perm[None, :]).reshape(-1)
    w_in_p = jnp.concatenate([w_in[:, qk_perm], w_in[:, 2 * ret_qk:]], axis=1).astype(BF16)
    proj = matmul(x2, w_in_p).reshape(bsz, t, -1)
    inv_freq = (1.0 / (10000.0 ** jnp.linspace(0.0, 1.0, RET_QK_HEAD // 2, dtype=F32))).reshape(1, -1)
    a = retention_mixer(proj, positions.reshape(bsz, t, 1), inv_freq)
    row = lambda u: u.reshape(1, -1)
    x2 = out_proj_deepnorm(a.reshape(n, -1), None, w_out.astype(BF16), x2, row(ln1_g), row(ln1_b))
    x2 = moe_top2_deepnorm(x2, jnp.pad(router, ((0, 0), (0, LANES - router.shape[1]))),
                           wg.astype(BF16), wu.astype(BF16), wd.astype(BF16), row(ln2_g), row(ln2_b))
    return x2.reshape(bsz, t, d)


def kernel(x, positions, ev_w_in, ev_conv_w, ev_conv_b, ev_dt_bias, ev_a_log, ev_d_skip, ev_m_norm, ev_mu_shift, ev_w0, ev_w_up, ev_a0, ev_a_up, ev_g_up, ev_k_k, ev_k_a, ev_r_k, ev_rln_w, ev_rln_b, ev_w_out, ev_ln1_g, ev_ln1_b, ev_ffn_wg, ev_ffn_wu, ev_ffn_wd, ev_ln2_g, ev_ln2_b, od_w_in, od_w_out, od_ln1_g, od_ln1_b, od_router, od_moe_wg, od_moe_wu, od_moe_wd, od_ln2_g, od_ln2_b):
    for layer in range(DEPTH):
        i = layer // 2
        if layer % 2 == 0:
            x = _even_layer(x, ev_w_in[i], ev_conv_w[i], ev_conv_b[i], ev_dt_bias[i], ev_a_log[i], ev_d_skip[i],
                            ev_m_norm[i], ev_mu_shift[i], ev_w0[i], ev_w_up[i], ev_a0[i], ev_a_up[i], ev_g_up[i],
                            ev_k_k[i], ev_k_a[i], ev_r_k[i], ev_rln_w[i], ev_rln_b[i], ev_w_out[i],
                            ev_ln1_g[i], ev_ln1_b[i], ev_ffn_wg[i], ev_ffn_wu[i], ev_ffn_wd[i],
                            ev_ln2_g[i], ev_ln2_b[i])
        else:
            x = _odd_layer(x, positions, od_w_in[i], od_w_out[i], od_ln1_g[i], od_ln1_b[i], od_router[i],
                           od_moe_wg[i], od_moe_wu[i], od_moe_wd[i], od_ln2_g[i], od_ln2_b[i])
    return x
```

```python
import jax
import jax.numpy as jnp
import numpy as np
from jax import lax
from jax.experimental import pallas as pl
from jax.experimental.pallas import tpu as pltpu

F32 = jnp.float32
BF16 = jnp.bfloat16
HIGHEST = lax.Precision.HIGHEST

CHUNK = 128
M_HEAD_DIM = 64
M_GROUPS = 2
M_STATE = 128
R_HEAD_DIM = 64
R_DECAY_LORA = 64
R_AAA_LORA = 64
R_GATE_LORA = 160
R_LN_EPS = 64e-5
RET_HEADS = 4
RET_QK_HEAD = 256
RET_V_HEAD = 512
N_EXPERTS = 8
LN_EPS = 1e-5
DEPTH = 2
ALPHA = (2.0 * DEPTH) ** 0.25

LANES = 128
SUBLANES = 8
MXU_DIM = 256
VMEM_LIMIT_BYTES = 56 * 1024 * 1024

LORA_BLOCK = 512
LORA_GATE_OFF = 128
LORA_GATE_PAD = 256
LORA_DT_OFF = 384


def _params(*sem):
    return pltpu.CompilerParams(dimension_semantics=sem, vmem_limit_bytes=VMEM_LIMIT_BYTES)


def _sigmoid(x):
    return jax.nn.sigmoid(x)


def _matmul_kernel(x_ref, w_ref, o_ref, xb_ref):
    @pl.when(pl.program_id(1) == 0)
    def _():
        xb_ref[...] = x_ref[...].astype(BF16)

    o_ref[...] = jnp.dot(xb_ref[...], w_ref[...], preferred_element_type=F32).astype(o_ref.dtype)


def matmul(x, w, *, tm=1024, tn=1024, out_dtype=F32):
    n, k = x.shape
    m = w.shape[1]
    tm = min(tm, n)
    return pl.pallas_call(
        _matmul_kernel,
        grid=(n // tm, m // tn),
        in_specs=[pl.BlockSpec((tm, k), lambda i, j: (i, 0)),
                  pl.BlockSpec((k, tn), lambda i, j: (0, j))],
        out_specs=pl.BlockSpec((tm, tn), lambda i, j: (i, j)),
        out_shape=jax.ShapeDtypeStruct((n, m), out_dtype),
        scratch_shapes=[pltpu.VMEM((tm, k), BF16)],
        compiler_params=_params("parallel", "arbitrary"),
        name="matmul",
    )(x, w)


def _deepnorm_rows(resid, sub, g, b):
    y = ALPHA * resid + sub
    mu = jnp.mean(y, axis=-1, keepdims=True)
    yc = y - mu
    var = jnp.mean(yc * yc, axis=-1, keepdims=True)
    return yc * lax.rsqrt(var + LN_EPS) * g + b


def _out_proj_kernel(a1_ref, a2_ref, w_ref, x_ref, g_ref, b_ref, o_ref):
    kh = a1_ref.shape[1]
    sub = jnp.dot(a1_ref[...], w_ref[:kh, :], preferred_element_type=F32)
    sub = sub + jnp.dot(a2_ref[...], w_ref[kh:, :], preferred_element_type=F32)
    o_ref[...] = _deepnorm_rows(x_ref[...], sub, g_ref[...], b_ref[...])


def out_proj_deepnorm(a1, a2, w, x, g, b, *, tm=512):
    n, d = x.shape
    kh = w.shape[0] // 2
    tm = min(tm, n)
    row = lambda i: (i, 0)
    fixed = lambda i: (0, 0)
    if a2 is None:
        a2, second = a1, pl.BlockSpec((tm, kh), lambda i: (i, 1))
    else:
        second = pl.BlockSpec((tm, kh), row)
    return pl.pallas_call(
        _out_proj_kernel,
        grid=(n // tm,),
        in_specs=[pl.BlockSpec((tm, kh), row), second,
                  pl.BlockSpec((2 * kh, d), fixed), pl.BlockSpec((tm, d), row),
                  pl.BlockSpec((1, d), fixed), pl.BlockSpec((1, d), fixed)],
        out_specs=pl.BlockSpec((tm, d), row),
        out_shape=jax.ShapeDtypeStruct((n, d), F32),
        compiler_params=_params("parallel"),
        name="out_proj_deepnorm",
    )(a1, a2, w, x, g, b)


TOP_K = 2
R_IDX = 0
R_PROB = TOP_K


def _router_kernel(x_ref, wr_ref, info_ref):
    logits = jnp.dot(x_ref[...], wr_ref[...], preferred_element_type=F32, precision=HIGHEST)
    lane = lax.broadcasted_iota(jnp.int32, logits.shape, 1)
    neg = jnp.float32(-jnp.inf)
    lg = jnp.where(lane < N_EXPERTS, logits, neg)
    m1 = jnp.max(lg, axis=-1, keepdims=True)
    i1 = jnp.min(jnp.where(lg == m1, lane, LANES), axis=-1, keepdims=True)
    lg2 = jnp.where(lane == i1, neg, lg)
    m2 = jnp.max(lg2, axis=-1, keepdims=True)
    i2 = jnp.min(jnp.where(lg2 == m2, lane, LANES), axis=-1, keepdims=True)
    e2 = jnp.exp(m2 - m1)
    p1 = 1.0 / (1.0 + e2)
    p2 = e2 / (1.0 + e2)
    info = jnp.where(lane == R_IDX, i1.astype(F32), 0.0)
    info = jnp.where(lane == R_IDX + 1, i2.astype(F32), info)
    info = jnp.where(lane == R_PROB, p1, info)
    info = jnp.where(lane == R_PROB + 1, p2, info)
    info_ref[...] = info


def router_top2(x, w_router, *, tm=1024):
    n, d = x.shape
    tm = min(tm, n)
    return pl.pallas_call(
        _router_kernel,
        grid=(n // tm,),
        in_specs=[pl.BlockSpec((tm, d), lambda i: (i, 0)), pl.BlockSpec((d, LANES), lambda i: (0, 0))],
        out_specs=pl.BlockSpec((tm, LANES), lambda i: (i, 0)),
        out_shape=jax.ShapeDtypeStruct((n, LANES), F32),
        compiler_params=_params("parallel"),
        name="router_top2",
    )(x, w_router)


def _combine_kernel(x_ref, y0_ref, y1_ref, info_ref, g_ref, b_ref, out_ref):
    sub = y0_ref[...] * info_ref[:, R_PROB:R_PROB + 1] + y1_ref[...] * info_ref[:, R_PROB + 1:R_PROB + 2]
    out_ref[...] = _deepnorm_rows(x_ref[...], sub, g_ref[...], b_ref[...])


def combine_deepnorm(x, y_slots, info, g, b, *, tc=512):
    n, d = x.shape
    tc = min(tc, n)
    nt = n // tc
    row = lambda i: (i, 0)
    fixed = lambda i: (0, 0)
    return pl.pallas_call(
        _combine_kernel,
        grid=(nt,),
        in_specs=[pl.BlockSpec((tc, d), row), pl.BlockSpec((tc, d), row),
                  pl.BlockSpec((tc, d), lambda i: (i + nt, 0)), pl.BlockSpec((tc, LANES), row),
                  pl.BlockSpec((1, d), fixed), pl.BlockSpec((1, d), fixed)],
        out_specs=pl.BlockSpec((tc, d), row),
        out_shape=jax.ShapeDtypeStruct((n, d), F32),
        compiler_params=_params("parallel"),
        name="combine_deepnorm",
    )(x, y_slots, y_slots, info, g, b)


DMA_ISSUE_UNROLL = 8


def _start_row_copies(src_ref, src_row, dst_ref, dst_row, sem, n_rows):
    def issue(r, carry):
        pltpu.make_async_copy(src_ref.at[pl.ds(src_row(r), 1)], dst_ref.at[pl.ds(dst_row(r), 1)], sem).start()
        return carry

    lax.fori_loop(0, n_rows, issue, 0, unroll=DMA_ISSUE_UNROLL)


def _expert_kernel(te_ref, src_ref, src_next_ref, dst_prev_ref, x_hbm, wg_ref, wu_ref, wd_ref, y_hbm,
                   xbuf_ref, ybuf_ref, acc_ref, xb_ref, xsem, ysem):
    i = pl.program_id(0)
    f = pl.program_id(1)
    n_i = pl.num_programs(0)
    n_f = pl.num_programs(1)
    tm = acc_ref.shape[0]
    slot = i % 2

    def wait_tile(src, dst, sem):
        pltpu.make_async_copy(src, dst, sem).wait()

    def swiglu_step(first):
        xb = xb_ref[...]
        hg = jnp.dot(xb, wg_ref[...], preferred_element_type=F32)
        hu = jnp.dot(xb, wu_ref[...], preferred_element_type=F32)
        h = (hg * _sigmoid(hg)) * hu
        out = jnp.dot(h.astype(BF16), wd_ref[...], preferred_element_type=F32)
        acc_ref[...] = out if first else acc_ref[...] + out

    @pl.when(f == 0)
    def _():
        @pl.when(i == 0)
        def _():
            _start_row_copies(x_hbm, lambda r: src_ref[0, r], xbuf_ref.at[0], lambda r: r, xsem.at[0], tm)
            ybuf_ref[...] = jnp.zeros_like(ybuf_ref)

        wait_tile(x_hbm.at[pl.ds(0, tm)], xbuf_ref.at[slot], xsem.at[slot])
        xb_ref[...] = xbuf_ref[slot].astype(BF16)
        swiglu_step(True)
        for r in range(tm):
            pltpu.make_async_copy(ybuf_ref.at[pl.ds(r, 1)], y_hbm.at[pl.ds(dst_prev_ref[0, r], 1)], ysem).start()
            pltpu.make_async_copy(x_hbm.at[pl.ds(src_next_ref[0, r], 1)], xbuf_ref.at[1 - slot, pl.ds(r, 1)],
                                  xsem.at[1 - slot]).start()

    @pl.when(f > 0)
    def _():
        swiglu_step(False)

    @pl.when(f == n_f - 1)
    def _():
        wait_tile(ybuf_ref, y_hbm.at[pl.ds(0, tm)], ysem)
        ybuf_ref[...] = acc_ref[...]

        @pl.when(i == n_i - 1)
        def _():
            wait_tile(x_hbm.at[pl.ds(0, tm)], xbuf_ref.at[1 - slot], xsem.at[1 - slot])


def expert_swiglu(x, src_rows, dst_rows, tile_expert, wg, wu, wd, *, tm, tf=1792):
    d = x.shape[1]
    r = src_rows.shape[0]
    ff = wg.shape[2]
    n_tiles = r // tm
    last = n_tiles - 1
    idx_tile = lambda u: u.reshape(n_tiles, 1, tm)
    smem_tile = lambda index: pl.BlockSpec((None, 1, tm), index, memory_space=pltpu.SMEM)
    tile = lambda i: jnp.minimum(i, last)
    grid_spec = pltpu.PrefetchScalarGridSpec(
        num_scalar_prefetch=1,
        grid=(n_tiles + 1, ff // tf),
        in_specs=[smem_tile(lambda i, f, te: (tile(i), 0, 0)),
                  smem_tile(lambda i, f, te: (tile(i + 1), 0, 0)),
                  smem_tile(lambda i, f, te: (jnp.maximum(i - 1, 0), 0, 0)),
                  pl.BlockSpec(memory_space=pl.ANY),
                  pl.BlockSpec((None, d, tf), lambda i, f, te: (te[tile(i)], 0, f)),
                  pl.BlockSpec((None, d, tf), lambda i, f, te: (te[tile(i)], 0, f)),
                  pl.BlockSpec((None, tf, d), lambda i, f, te: (te[tile(i)], f, 0))],
        out_specs=pl.BlockSpec(memory_space=pl.ANY),
        scratch_shapes=[pltpu.VMEM((2, tm, d), F32), pltpu.VMEM((tm, d), F32), pltpu.VMEM((tm, d), F32),
                        pltpu.VMEM((tm, d), BF16), pltpu.SemaphoreType.DMA((2,)), pltpu.SemaphoreType.DMA(())],
    )
    return pl.pallas_call(
        _expert_kernel,
        grid_spec=grid_spec,
        out_shape=jax.ShapeDtypeStruct((r, d), F32),
        compiler_params=_params("arbitrary", "arbitrary"),
        name="expert_swiglu",
    )(tile_expert, idx_tile(src_rows), idx_tile(src_rows), idx_tile(dst_rows), x, wg, wu, wd)


def moe_top2_deepnorm(x, w_router, wg, wu, wd, g, b, *, tm=512):
    n = x.shape[0]
    ne = wg.shape[0]
    info = router_top2(x, w_router)
    e_flat = info[:, R_IDX:R_IDX + TOP_K].astype(jnp.int32).reshape(-1)

    onehot = (e_flat[:, None] == jnp.arange(ne, dtype=jnp.int32)[None, :]).astype(jnp.int32)
    csum = jnp.cumsum(onehot, axis=0)
    rank = jnp.sum((csum - 1) * onehot, axis=1)
    padded = ((csum[-1] + tm - 1) // tm) * tm
    ends = jnp.cumsum(padded)
    dest = (ends - padded)[e_flat] + rank
    n_rows = n * TOP_K + ne * tm
    n_tiles = n_rows // tm
    assign = jnp.full((n_rows,), -1, jnp.int32).at[dest].set(jnp.arange(n * TOP_K, dtype=jnp.int32))
    token, slot = assign // TOP_K, assign % TOP_K
    src_rows = jnp.where(assign >= 0, token, 0)
    pad_rank = jnp.cumsum((assign < 0).astype(jnp.int32)) - 1
    dst_rows = jnp.where(assign >= 0, slot * n + token, n * TOP_K + pad_rank)
    tile_start = jnp.arange(n_tiles, dtype=jnp.int32) * tm
    tile_expert = jnp.minimum(jnp.sum((tile_start[:, None] >= ends[None, :]).astype(jnp.int32), axis=1), ne - 1)

    y_slots = expert_swiglu(x, src_rows, dst_rows, tile_expert, wg, wu, wd, tm=tm)
    return combine_deepnorm(x, y_slots, info, g, b)


def _ffn_kernel(x_ref, wg_ref, wu_ref, wd_ref, g_ref, b_ref, o_ref, acc_ref, xb_ref):
    f = pl.program_id(1)

    @pl.when(f == 0)
    def _():
        acc_ref[...] = jnp.zeros_like(acc_ref)
        xb_ref[...] = x_ref[...].astype(BF16)

    xb = xb_ref[...]
    hg = jnp.dot(xb, wg_ref[...], preferred_element_type=F32)
    hu = jnp.dot(xb, wu_ref[...], preferred_element_type=F32)
    h = (hg * _sigmoid(hg)) * hu
    acc_ref[...] += jnp.dot(h.astype(BF16), wd_ref[...], preferred_element_type=F32)

    @pl.when(f == pl.num_programs(1) - 1)
    def _():
        o_ref[...] = _deepnorm_rows(x_ref[...], acc_ref[...], g_ref[...], b_ref[...])


def ffn_deepnorm(x, wg, wu, wd, g, b, *, tm=512, tf=1792):
    n, d = x.shape
    ff = wg.shape[1]
    tm = min(tm, n)
    row = lambda i, f: (i, 0)
    fixed = lambda i, f: (0, 0)
    return pl.pallas_call(
        _ffn_kernel,
        grid=(n // tm, ff // tf),
        in_specs=[pl.BlockSpec((tm, d), row),
                  pl.BlockSpec((d, tf), lambda i, f: (0, f)),
                  pl.BlockSpec((d, tf), lambda i, f: (0, f)),
                  pl.BlockSpec((tf, d), lambda i, f: (f, 0)),
                  pl.BlockSpec((1, d), fixed), pl.BlockSpec((1, d), fixed)],
        out_specs=pl.BlockSpec((tm, d), row),
        out_shape=jax.ShapeDtypeStruct((n, d), F32),
        scratch_shapes=[pltpu.VMEM((tm, d), F32), pltpu.VMEM((tm, d), BF16)],
        compiler_params=_params("parallel", "arbitrary"),
        name="ffn_deepnorm",
    )(x, wg, wu, wd, g, b)


CONV_HISTORY = SUBLANES


def _ssd_kernel(xs_ref, b_ref, c_ref, z_ref, dt_ref, cw_ref, cb_ref, dtb_ref, alog_ref, dskip_ref, mnorm_ref,
                y_ref, state_ref, ext_ref, yacc_ref):
    l, hp = xs_ref.shape
    gn = b_ref.shape[1]
    p = M_HEAD_DIM
    ns = M_STATE
    nh = hp // p
    hpg = nh // M_GROUPS
    kc = cw_ref.shape[0]
    hist = CONV_HISTORY

    @pl.when(pl.program_id(1) == 0)
    def _():
        state_ref[...] = jnp.zeros_like(state_ref)
        ext_ref[0:hist, :] = jnp.zeros((hist, ext_ref.shape[1]), F32)

    ext_ref[hist:, 0:hp] = xs_ref[...]
    ext_ref[hist:, hp:hp + gn] = b_ref[...]
    ext_ref[hist:, hp + gn:] = c_ref[...]
    conv = cb_ref[...]
    for i in range(kc):
        conv = conv + cw_ref[i:i + 1, :] * ext_ref[pl.ds(hist - (kc - 1) + i, l), :]
    ext_ref[0:hist, :] = ext_ref[l:l + hist, :]
    xbc = conv * _sigmoid(conv)
    xs = xbc[:, :hp]

    dt_pre = dt_ref[...] + dtb_ref[...]
    dt = jnp.maximum(dt_pre, 0.0) + jnp.log(1.0 + jnp.exp(-jnp.abs(dt_pre)))
    a = -jnp.exp(alog_ref[...])

    row = lax.broadcasted_iota(jnp.int32, (l, l), 0)
    col = lax.broadcasted_iota(jnp.int32, (l, l), 1)
    causal = row >= col
    acs = jnp.dot(causal.astype(F32), dt * a, preferred_element_type=F32, precision=HIGHEST)
    acst = acs.T
    dtt = dt.T
    acs_last = acs[l - 1:l, :]
    w_end = jnp.exp(acs_last - acs) * dt
    exp_acs = jnp.exp(acs)
    exp_last = jnp.exp(acs_last)

    for g in range(M_GROUPS):
        b_g = xbc[:, hp + g * ns:hp + (g + 1) * ns]
        c_g = xbc[:, hp + gn + g * ns:hp + gn + (g + 1) * ns].astype(BF16)
        bt_g = b_g.T.astype(BF16)
        cb = jnp.dot(c_g, bt_g, preferred_element_type=F32)
        for hh in range(hpg):
            h = g * hpg + hh
            x_h = xs[:, h * p:(h + 1) * p]
            seg = acs[:, h:h + 1] - acst[h:h + 1, :]
            decay = jnp.where(causal, jnp.exp(seg), 0.0)
            m = cb * decay * dtt[h:h + 1, :]
            st = state_ref[h]
            y = jnp.dot(m.astype(BF16), x_h.astype(BF16), preferred_element_type=F32)
            y = y + jnp.dot(c_g, st.astype(BF16), preferred_element_type=F32) * exp_acs[:, h:h + 1]
            yacc_ref[:, h * p:(h + 1) * p] = y
            xw = (x_h * w_end[:, h:h + 1]).astype(BF16)
            state_ref[h] = st * exp_last[:, h:h + 1] + jnp.dot(bt_g, xw, preferred_element_type=F32)

    z = z_ref[...]
    y = (yacc_ref[...] + dskip_ref[...] * xs) * (z * _sigmoid(z))
    gw = hp // M_GROUPS
    for g in range(M_GROUPS):
        seg = y[:, g * gw:(g + 1) * gw]
        ms = jnp.mean(seg * seg, axis=-1, keepdims=True)
        y_ref[:, g * gw:(g + 1) * gw] = (seg * lax.rsqrt(ms + LN_EPS)
                                         * mnorm_ref[:, g * gw:(g + 1) * gw]).astype(y_ref.dtype)


def ssd_mixer(proj, cols, conv_w, conv_b, dt_bias, a_log, d_skip, m_norm):
    bsz, t, _ = proj.shape
    hp = m_norm.shape[1]
    gn = M_GROUPS * M_STATE
    conv_dim = hp + 2 * gn
    nh = hp // M_HEAD_DIM
    l = CHUNK

    def col_block(width, off):
        blk = off // width
        return pl.BlockSpec((None, l, width), lambda b, c: (b, c, blk))

    fixed = lambda b, c: (0, 0)
    return pl.pallas_call(
        _ssd_kernel,
        grid=(bsz, t // l),
        in_specs=[col_block(hp, cols["xs"]), col_block(gn, cols["B"]), col_block(gn, cols["C"]),
                  col_block(hp, cols["z"]), col_block(LANES, cols["dt"]),
                  pl.BlockSpec(conv_w.shape, fixed), pl.BlockSpec((1, conv_dim), fixed),
                  pl.BlockSpec((1, LANES), fixed), pl.BlockSpec((1, LANES), fixed),
                  pl.BlockSpec((1, hp), fixed), pl.BlockSpec((1, hp), fixed)],
        out_specs=pl.BlockSpec((None, l, hp), lambda b, c: (b, c, 0)),
        out_shape=jax.ShapeDtypeStruct((bsz, t, hp), BF16),
        scratch_shapes=[pltpu.VMEM((nh, M_STATE, M_HEAD_DIM), F32),
                        pltpu.VMEM((CONV_HISTORY + l, conv_dim), F32),
                        pltpu.VMEM((l, hp), F32)],
        compiler_params=_params("parallel", "arbitrary"),
        name="ssd_mixer",
    )(proj, proj, proj, proj, proj, conv_w, conv_b, dt_bias, a_log, d_skip, m_norm)


EXP_MINUS_HALF = float(np.exp(-0.5))


def _head_sum(x, ones):
    hi = x.astype(BF16)
    lo = (x - hi.astype(F32)).astype(BF16)
    outs = []
    for j in range(x.shape[1] // MXU_DIM):
        sl = slice(j * MXU_DIM, (j + 1) * MXU_DIM)
        outs.append(jnp.dot(hi[:, sl], ones, preferred_element_type=F32)
                    + jnp.dot(lo[:, sl], ones, preferred_element_type=F32))
    return jnp.concatenate(outs, axis=-1)


def _rwkv_prep_kernel(r_ref, k_ref, v_ref, lo_ref, mu_ref, w0_ref, a0_ref, kk_ref, ka_ref, rk_ref,
                      wup_ref, aup_ref, gup_ref, ones_ref,
                      ro_ref, wo_ref, ko_ref, vo_ref, ao_ref, bo_ref, bonus_ref, gate_ref, prev_ref):
    tl, c = r_ref.shape

    @pl.when(pl.program_id(1) == 0)
    def _():
        prev_ref[...] = jnp.zeros_like(prev_ref)

    first_row = lax.broadcasted_iota(jnp.int32, (tl, 1), 0) == 0

    def shift_mix(x_ref, off):
        x = x_ref[...]
        width = x.shape[1]
        prev = jnp.where(first_row, prev_ref[0:1, off:off + width], pltpu.roll(x, 1, 0))
        prev_ref[0:1, off:off + width] = x[tl - 1:tl, :]
        return x + (prev - x) * mu_ref[:, off:off + width]

    r = shift_mix(r_ref, 0)
    k = shift_mix(k_ref, c)
    v = shift_mix(v_ref, 2 * c)
    lo = shift_mix(lo_ref, 3 * c)
    lo_a = lo[:, 0:LANES]
    lo_g = lo[:, LORA_GATE_OFF:LORA_GATE_OFF + LORA_GATE_PAD]
    w = w0_ref[...] + jnp.dot(jnp.tanh(lo_a).astype(BF16), wup_ref[...], preferred_element_type=F32)
    log_decay = -EXP_MINUS_HALF * _sigmoid(w)
    iclr = _sigmoid(a0_ref[...] + jnp.dot(lo_a.astype(BF16), aup_ref[...], preferred_element_type=F32))
    gate = jnp.dot(_sigmoid(lo_g).astype(BF16), gup_ref[...], preferred_element_type=F32)
    ones = ones_ref[...]
    kk = k * kk_ref[...]
    kk = kk / jnp.maximum(jnp.sqrt(_head_sum(kk * kk, ones)), 1e-12)
    k2 = k * (1.0 + (iclr - 1.0) * ka_ref[...])
    ro_ref[...] = r
    wo_ref[...] = log_decay
    ko_ref[...] = k2
    vo_ref[...] = v
    ao_ref[...] = -kk
    bo_ref[...] = kk * iclr
    bonus_ref[...] = _head_sum(r * k2 * rk_ref[...], ones) * v
    gate_ref[...] = gate


def rwkv_prep(proj, cols, mu, w0, a0, k_k, k_a, r_k, w_up, a_up, g_up, *, tl=256):
    bsz, t, _ = proj.shape
    c = w0.shape[1]
    tl = min(tl, t)
    ones = jnp.asarray(np.kron(np.eye(MXU_DIM // R_HEAD_DIM), np.ones((R_HEAD_DIM, R_HEAD_DIM))), BF16)

    def col_block(width, off):
        blk = off // width
        return pl.BlockSpec((None, tl, width), lambda b, i: (b, i, blk))

    def fixed(shape):
        return pl.BlockSpec(shape, lambda b, i: (0, 0))

    out_spec = pl.BlockSpec((None, tl, c), lambda b, i: (b, i, 0))
    return pl.pallas_call(
        _rwkv_prep_kernel,
        grid=(bsz, t // tl),
        in_specs=[col_block(c, cols["r"]), col_block(c, cols["k"]), col_block(c, cols["v"]),
                  col_block(LORA_BLOCK, cols["lora"]), fixed(mu.shape)]
        + [fixed((1, c))] * 5 + [fixed(w_up.shape), fixed(a_up.shape), fixed(g_up.shape), fixed(ones.shape)],
        out_specs=[out_spec] * 8,
        out_shape=[jax.ShapeDtypeStruct((bsz, t, c), F32)] * 8,
        scratch_shapes=[pltpu.VMEM((SUBLANES, mu.shape[1]), F32)],
        compiler_params=_params("parallel", "arbitrary"),
        name="rwkv_prep",
    )(proj, proj, proj, proj, mu, w0, a0, k_k, k_a, r_k, w_up, a_up, g_up, ones)


RWKV_CHUNK = 64


def _nt(a, b):
    return lax.dot_general(a, b, (((1,), (1,)), ((), ())), preferred_element_type=F32)


def _tn(a, b):
    return lax.dot_general(a, b, (((0,), (0,)), ((), ())), preferred_element_type=F32)


def _rwkv_chunk_kernel(r_ref, lw_ref, k_ref, v_ref, a_ref, b_ref, bonus_ref, gate_ref, rlnw_ref, rlnb_ref,
                       ones_ref, y_ref, state_ref, yacc_ref):
    l, c = r_ref.shape
    hd = R_HEAD_DIM
    n_pair = c // LANES

    @pl.when(pl.program_id(1) == 0)
    def _():
        state_ref[...] = jnp.zeros_like(state_ref)

    row = lax.broadcasted_iota(jnp.int32, (l, l), 0)
    col = lax.broadcasted_iota(jnp.int32, (l, l), 1)
    lw = lw_ref[...]
    g = jnp.dot((row >= col).astype(F32), lw, preferred_element_type=F32, precision=HIGHEST)
    g_last = g[l - 1:l, :]
    e_g = jnp.exp(g)
    e_ng = jnp.exp(-g)
    e_last = jnp.exp(g_last)
    at_all = a_ref[...] * jnp.exp(g - lw)
    rt_all = r_ref[...] * e_g
    bt_all = b_ref[...] * e_ng
    kt_all = k_ref[...] * e_ng
    v_all = v_ref[...]

    wi = lax.broadcasted_iota(jnp.int32, (l, 2 * l), 0)
    wl = lax.broadcasted_iota(jnp.int32, (l, 2 * l), 1)
    wj = jnp.where(wl >= l, wl - l, wl)
    strict = wj < wi
    incl = wj <= wi
    eye_w = (wj == wi).astype(F32)
    first_w = wl < l
    first_c = lax.broadcasted_iota(jnp.int32, (l, LANES), 1) < hd
    bi = lax.broadcasted_iota(jnp.int32, (LANES, LANES), 0)
    bj = lax.broadcasted_iota(jnp.int32, (LANES, LANES), 1)
    same_head = (bi < hd) == (bj < hd)

    def block_rows(x, first):
        return jnp.concatenate([jnp.where(first, x, 0.0), jnp.where(first, 0.0, x)], axis=0)

    pairs = range(n_pair)
    lanes_of = [slice(p * LANES, (p + 1) * LANES) for p in pairs]
    dot = lambda x, y: jnp.dot(x, y, preferred_element_type=F32)
    ht = [state_ref[p] for p in pairs]
    lhs_ar = [jnp.concatenate([at_all[:, s], rt_all[:, s]], axis=0).astype(BF16) for s in lanes_of]
    gram = [_nt(lhs_ar[p], jnp.concatenate([block_rows(bt_all[:, s], first_c),
                                            block_rows(kt_all[:, s], first_c)], axis=0).astype(BF16))
            for p, s in zip(pairs, lanes_of)]
    h_ar = [_nt(lhs_ar[p], ht[p].astype(BF16)) for p in pairs]
    n_w = [jnp.where(strict, gram[p][0:l, 0:2 * l], 0.0) for p in pairs]
    t_w = [eye_w + n_w[p] for p in pairs]
    p_w = [dot(n_w[p].astype(BF16), block_rows(n_w[p], first_w).astype(BF16)) for p in pairs]
    for _ in range(int(np.log2(l)) - 1):
        res = [dot(jnp.concatenate([t_w[p], p_w[p]], axis=0).astype(BF16),
                   block_rows(p_w[p], first_w).astype(BF16)) for p in pairs]
        t_w = [t_w[p] + res[p][0:l] for p in pairs]
        p_w = [res[p][l:2 * l] for p in pairs]
    v_bd = [block_rows(v_all[:, s], first_c).astype(BF16) for s in lanes_of]
    rhs_u = [h_ar[p][0:l] + dot(jnp.where(strict, gram[p][0:l, 2 * l:4 * l], 0.0).astype(BF16), v_bd[p])
             for p in pairs]
    u = [dot(t_w[p].astype(BF16), block_rows(rhs_u[p], first_c).astype(BF16)) for p in pairs]
    for p, s in zip(pairs, lanes_of):
        rbk_w = jnp.where(jnp.concatenate([incl, incl], axis=1), gram[p][l:2 * l, :], 0.0)
        yacc_ref[:, s] = h_ar[p][l:2 * l] + dot(
            rbk_w.astype(BF16), jnp.concatenate([block_rows(u[p], first_c).astype(BF16), v_bd[p]], axis=0))
    for p, s in zip(pairs, lanes_of):
        e_l = e_last[:, s]
        upd = _tn(jnp.concatenate([u[p], v_all[:, s]], axis=0).astype(BF16),
                  jnp.concatenate([bt_all[:, s] * e_l, kt_all[:, s] * e_l], axis=0).astype(BF16))
        state_ref[p] = ht[p] * e_l + jnp.where(same_head, upd, 0.0)

    y = yacc_ref[...]
    ones = ones_ref[...]
    inv = 1.0 / hd
    mu = _head_sum(y, ones) * inv
    yc = y - mu
    var = _head_sum(yc * yc, ones) * inv
    yn = yc * lax.rsqrt(var + R_LN_EPS)
    y_ref[...] = (((yn * rlnw_ref[...] + rlnb_ref[...]) + bonus_ref[...]) * gate_ref[...]).astype(y_ref.dtype)


def rwkv_chunk_mixer(r, lw, k, v, a, b, bonus, gate, rln_w, rln_b):
    bsz, t, c = r.shape
    l = RWKV_CHUNK
    ones = jnp.asarray(np.kron(np.eye(MXU_DIM // R_HEAD_DIM), np.ones((R_HEAD_DIM, R_HEAD_DIM))), BF16)
    blk = pl.BlockSpec((None, l, c), lambda i, j: (i, j, 0))
    fixed = lambda shape: pl.BlockSpec(shape, lambda i, j: (0, 0))
    return pl.pallas_call(
        _rwkv_chunk_kernel,
        grid=(bsz, t // l),
        in_specs=[blk] * 8 + [fixed((1, c)), fixed((1, c)), fixed(ones.shape)],
        out_specs=blk,
        out_shape=jax.ShapeDtypeStruct((bsz, t, c), BF16),
        scratch_shapes=[pltpu.VMEM((c // LANES, LANES, LANES), F32), pltpu.VMEM((l, c), F32)],
        compiler_params=_params("parallel", "arbitrary"),
        name="rwkv_chunk_mixer",
    )(r, lw, k, v, a, b, bonus, gate, rln_w, rln_b, ones)


def _retention_kernel(q_ref, k_ref, v_ref, g_ref, pos_ref, freq_ref, y_ref, state_ref):
    l = q_ref.shape[0]
    dk = RET_QK_HEAD
    dv = RET_V_HEAD
    half = dk // 2

    @pl.when(pl.program_id(1) == 0)
    def _():
        state_ref[...] = jnp.zeros_like(state_ref)

    ang = pos_ref[...].astype(F32) * freq_ref[...]
    cos = jnp.cos(ang)
    sin = jnp.sin(ang)

    def rotate(u):
        ue = u[:, :half]
        uo = u[:, half:]
        return jnp.concatenate([ue * cos - uo * sin, uo * cos + ue * sin], axis=-1)

    row = lax.broadcasted_iota(jnp.int32, (l, l), 0)
    col = lax.broadcasted_iota(jnp.int32, (l, l), 1)
    rel = (row - col).astype(F32)
    causal = row >= col
    idx = lax.broadcasted_iota(jnp.int32, (l, 1), 0).astype(F32)
    for h in range(RET_HEADS):
        log_gamma = float(np.log(np.float32(1.0) - np.float32(2.0) ** np.float32(-5.0 - h)))
        intra = jnp.where(causal, jnp.exp(log_gamma * jnp.where(causal, rel, 0.0)), 0.0)
        q_decay = jnp.exp(log_gamma * (idx + 1.0))
        k_decay = jnp.exp(log_gamma * (l - 1.0 - idx))
        chunk_decay = float(np.exp(np.float32(log_gamma) * np.float32(l)))
        q_h = rotate(q_ref[:, h * dk:(h + 1) * dk]).astype(BF16)
        k_h = rotate(k_ref[:, h * dk:(h + 1) * dk]) * (dk ** -0.5)
        v_h = v_ref[:, h * dv:(h + 1) * dv].astype(BF16)
        s = lax.dot_general(q_h, k_h.astype(BF16), (((1,), (1,)), ((), ())),
                            preferred_element_type=F32) * intra
        st = state_ref[h]
        y = jnp.dot(s.astype(BF16), v_h, preferred_element_type=F32)
        y = y + jnp.dot(q_h, st.astype(BF16), preferred_element_type=F32) * q_decay
        kd = (k_h * k_decay).astype(BF16)
        state_ref[h] = st * chunk_decay + lax.dot_general(
            kd, v_h, (((0,), (0,)), ((), ())), preferred_element_type=F32)
        mu = jnp.mean(y, axis=-1, keepdims=True)
        yc = y - mu
        var = jnp.mean(yc * yc, axis=-1, keepdims=True)
        g_h = g_ref[:, h * dv:(h + 1) * dv]
        y_ref[:, h * dv:(h + 1) * dv] = (g_h * _sigmoid(g_h) * (yc * lax.rsqrt(var + LN_EPS))).astype(y_ref.dtype)


def retention_mixer(proj, positions, inv_freq):
    bsz, t, _ = proj.shape
    hk = RET_HEADS * RET_QK_HEAD
    hv = RET_HEADS * RET_V_HEAD
    l = CHUNK
    return pl.pallas_call(
        _retention_kernel,
        grid=(bsz, t // l),
        in_specs=[pl.BlockSpec((None, l, hk), lambda b, c: (b, c, 0)),
                  pl.BlockSpec((None, l, hk), lambda b, c: (b, c, 1)),
                  pl.BlockSpec((None, l, hv), lambda b, c: (b, c, 1)),
                  pl.BlockSpec((None, l, hv), lambda b, c: (b, c, 2)),
                  pl.BlockSpec((None, l, 1), lambda b, c: (b, c, 0)),
                  pl.BlockSpec((1, RET_QK_HEAD // 2), lambda b, c: (0, 0))],
        out_specs=pl.BlockSpec((None, l, hv), lambda b, c: (b, c, 0)),
        out_shape=jax.ShapeDtypeStruct((bsz, t, hv), BF16),
        scratch_shapes=[pltpu.VMEM((RET_HEADS, RET_QK_HEAD, RET_V_HEAD), F32)],
        compiler_params=_params("parallel", "arbitrary"),
        name="retention_mixer",
    )(proj, proj, proj, proj, positions, inv_freq)


def _even_layer(x, w_in, conv_w, conv_b, dt_bias, a_log, d_skip, m_norm, mu_shift, w0, w_up, a0, a_up,
                g_up, k_k, k_a, r_k, rln_w, rln_b, w_out, ln1_g, ln1_b, wg, wu, wd, ln2_g, ln2_b):
    bsz, t, d = x.shape
    n = bsz * t
    m_inner = m_norm.shape[0]
    m_heads = dt_bias.shape[0]
    c = w0.shape[0]
    r_heads = c // R_HEAD_DIM
    gn = M_GROUPS * M_STATE
    o_xbc = m_inner
    o_dt = o_xbc + m_inner + 2 * gn
    o_rw = o_dt + m_heads
    o_lora = o_rw + 3 * c
    n_lora = R_DECAY_LORA + R_AAA_LORA + R_GATE_LORA
    x2 = x.reshape(n, d)

    zeros = lambda k: jnp.zeros((d, k), w_in.dtype)
    w_in_p = jnp.concatenate([
        w_in[:, o_rw:o_lora],
        w_in[:, :m_inner],
        w_in[:, o_xbc:o_dt],
        w_in[:, o_lora:o_lora + n_lora], zeros(LORA_DT_OFF - n_lora),
        w_in[:, o_dt:o_rw], zeros(LORA_BLOCK - LORA_DT_OFF - m_heads)], axis=1).astype(BF16)
    cols = {"r": 0, "k": c, "v": 2 * c, "z": 3 * c, "xs": 3 * c + m_inner, "B": 3 * c + 2 * m_inner,
            "C": 3 * c + 2 * m_inner + gn, "lora": 3 * c + 2 * m_inner + 2 * gn}
    cols["dt"] = cols["lora"] + LORA_DT_OFF
    proj = matmul(x2, w_in_p).reshape(bsz, t, -1)

    pad_lane = lambda u: jnp.pad(u.reshape(1, -1), ((0, 0), (0, LANES - u.shape[0])))
    y_ssd = ssd_mixer(proj, cols, conv_w, conv_b.reshape(1, -1), pad_lane(dt_bias), pad_lane(a_log),
                      jnp.repeat(d_skip, M_HEAD_DIM).reshape(1, -1), m_norm.reshape(1, -1))

    mu_p = jnp.pad(mu_shift, (0, LORA_BLOCK - n_lora)).reshape(1, -1)
    w_up_p = jnp.pad(w_up, ((0, LANES - R_DECAY_LORA), (0, 0))).astype(BF16)
    a_up_p = jnp.pad(a_up, ((R_DECAY_LORA, LANES - R_DECAY_LORA - R_AAA_LORA), (0, 0))).astype(BF16)
    g_up_p = jnp.pad(g_up, ((0, LORA_GATE_PAD - R_GATE_LORA), (0, 0))).astype(BF16)
    row = lambda u: u.reshape(1, -1)
    r, lw, k, v, a, b, bonus, gate = rwkv_prep(proj, cols, mu_p, row(w0), row(a0), row(k_k), row(k_a), row(r_k),
                                               w_up_p, a_up_p, g_up_p)
    y_rwkv = rwkv_chunk_mixer(r, lw, k, v, a, b, bonus, gate, row(rln_w), row(rln_b))
    x2 = out_proj_deepnorm(y_ssd.reshape(n, m_inner), y_rwkv.reshape(n, c), w_out.astype(BF16), x2,
                           row(ln1_g), row(ln1_b))
    x2 = ffn_deepnorm(x2, wg.astype(BF16), wu.astype(BF16), wd.astype(BF16), row(ln2_g), row(ln2_b))
    return x2.reshape(bsz, t, d)


def _odd_layer(x, positions, w_in, w_out, ln1_g, ln1_b, router, wg, wu, wd, ln2_g, ln2_b):
    bsz, t, d = x.shape
    n = bsz * t
    ret_qk = RET_HEADS * RET_QK_HEAD
    x2 = x.reshape(n, d)
    head_perm = np.concatenate([np.arange(0, RET_QK_HEAD, 2), np.arange(1, RET_QK_HEAD, 2)])
    qk_perm = (np.arange(2 * RET_HEADS)[:, None] * RET_QK_HEAD + head_perm[None, :]).reshape(-1)
    w_in_p = jnp.concatenate([w_in[:, qk_perm], w_in[:, 2 * ret_qk:]], axis=1).astype(BF16)
    proj = matmul(x2, w_in_p).reshape(bsz, t, -1)
    inv_freq = (1.0 / (10000.0 ** jnp.linspace(0.0, 1.0, RET_QK_HEAD // 2, dtype=F32))).reshape(1, -1)
    a = retention_mixer(proj, positions.reshape(bsz, t, 1), inv_freq)
    row = lambda u: u.reshape(1, -1)
    x2 = out_proj_deepnorm(a.reshape(n, -1), None, w_out.astype(BF16), x2, row(ln1_g), row(ln1_b))
    x2 = moe_top2_deepnorm(x2, jnp.pad(router, ((0, 0), (0, LANES - router.shape[1]))),
                           wg.astype(BF16), wu.astype(BF16), wd.astype(BF16), row(ln2_g), row(ln2_b))
    return x2.reshape(bsz, t, d)


def kernel(x, positions, ev_w_in, ev_conv_w, ev_conv_b, ev_dt_bias, ev_a_log, ev_d_skip, ev_m_norm, ev_mu_shift, ev_w0, ev_w_up, ev_a0, ev_a_up, ev_g_up, ev_k_k, ev_k_a, ev_r_k, ev_rln_w, ev_rln_b, ev_w_out, ev_ln1_g, ev_ln1_b, ev_ffn_wg, ev_ffn_wu, ev_ffn_wd, ev_ln2_g, ev_ln2_b, od_w_in, od_w_out, od_ln1_g, od_ln1_b, od_router, od_moe_wg, od_moe_wu, od_moe_wd, od_ln2_g, od_ln2_b):
    for layer in range(DEPTH):
        i = layer // 2
        if layer % 2 == 0:
            x = _even_layer(x, ev_w_in[i], ev_conv_w[i], ev_conv_b[i], ev_dt_bias[i], ev_a_log[i], ev_d_skip[i],
                            ev_m_norm[i], ev_mu_shift[i], ev_w0[i], ev_w_up[i], ev_a0[i], ev_a_up[i], ev_g_up[i],
                            ev_k_k[i], ev_k_a[i], ev_r_k[i], ev_rln_w[i], ev_rln_b[i], ev_w_out[i],
                            ev_ln1_g[i], ev_ln1_b[i], ev_ffn_wg[i], ev_ffn_wu[i], ev_ffn_wd[i],
                            ev_ln2_g[i], ev_ln2_b[i])
        else:
            x = _odd_layer(x, positions, od_w_in[i], od_w_out[i], od_ln1_g[i], od_ln1_b[i], od_router[i],
                           od_moe_wg[i], od_moe_wu[i], od_moe_wd[i], od_ln2_g[i], od_ln2_b[i])
    return x
```

```python
import jax
import jax.numpy as jnp
import numpy as np
from jax import lax
from jax.experimental import pallas as pl
from jax.experimental.pallas import tpu as pltpu

F32 = jnp.float32
BF16 = jnp.bfloat16
HIGHEST = lax.Precision.HIGHEST

CHUNK = 128
M_HEAD_DIM = 64
M_GROUPS = 2
M_STATE = 128
R_HEAD_DIM = 64
R_DECAY_LORA = 64
R_AAA_LORA = 64
R_GATE_LORA = 160
R_LN_EPS = 64e-5
RET_HEADS = 4
RET_QK_HEAD = 256
RET_V_HEAD = 512
N_EXPERTS = 8
LN_EPS = 1e-5
DEPTH = 2
ALPHA = (2.0 * DEPTH) ** 0.25

LANES = 128
SUBLANES = 8
MXU_DIM = 256
VMEM_LIMIT_BYTES = 56 * 1024 * 1024

LORA_BLOCK = 512
LORA_GATE_OFF = 128
LORA_GATE_PAD = 256
LORA_DT_OFF = 384


def _params(*sem):
    return pltpu.CompilerParams(dimension_semantics=sem, vmem_limit_bytes=VMEM_LIMIT_BYTES)


def _sigmoid(x):
    return jax.nn.sigmoid(x)


def _matmul_kernel(x_ref, w_ref, o_ref, xb_ref):
    @pl.when(pl.program_id(1) == 0)
    def _():
        xb_ref[...] = x_ref[...].astype(BF16)

    o_ref[...] = jnp.dot(xb_ref[...], w_ref[...], preferred_element_type=F32).astype(o_ref.dtype)


def matmul(x, w, *, tm=1024, tn=1024, out_dtype=F32):
    n, k = x.shape
    m = w.shape[1]
    tm = min(tm, n)
    return pl.pallas_call(
        _matmul_kernel,
        grid=(n // tm, m // tn),
        in_specs=[pl.BlockSpec((tm, k), lambda i, j: (i, 0)),
                  pl.BlockSpec((k, tn), lambda i, j: (0, j))],
        out_specs=pl.BlockSpec((tm, tn), lambda i, j: (i, j)),
        out_shape=jax.ShapeDtypeStruct((n, m), out_dtype),
        scratch_shapes=[pltpu.VMEM((tm, k), BF16)],
        compiler_params=_params("parallel", "arbitrary"),
        name="matmul",
    )(x, w)


def _deepnorm_rows(resid, sub, g, b):
    y = ALPHA * resid + sub
    mu = jnp.mean(y, axis=-1, keepdims=True)
    yc = y - mu
    var = jnp.mean(yc * yc, axis=-1, keepdims=True)
    return yc * lax.rsqrt(var + LN_EPS) * g + b


def _out_proj_kernel(a1_ref, a2_ref, w_ref, x_ref, g_ref, b_ref, o_ref):
    kh = a1_ref.shape[1]
    sub = jnp.dot(a1_ref[...], w_ref[:kh, :], preferred_element_type=F32)
    sub = sub + jnp.dot(a2_ref[...], w_ref[kh:, :], preferred_element_type=F32)
    o_ref[...] = _deepnorm_rows(x_ref[...], sub, g_ref[...], b_ref[...])


def out_proj_deepnorm(a1, a2, w, x, g, b, *, tm=512):
    n, d = x.shape
    kh = w.shape[0] // 2
    tm = min(tm, n)
    row = lambda i: (i, 0)
    fixed = lambda i: (0, 0)
    if a2 is None:
        a2, second = a1, pl.BlockSpec((tm, kh), lambda i: (i, 1))
    else:
        second = pl.BlockSpec((tm, kh), row)
    return pl.pallas_call(
        _out_proj_kernel,
        grid=(n // tm,),
        in_specs=[pl.BlockSpec((tm, kh), row), second,
                  pl.BlockSpec((2 * kh, d), fixed), pl.BlockSpec((tm, d), row),
                  pl.BlockSpec((1, d), fixed), pl.BlockSpec((1, d), fixed)],
        out_specs=pl.BlockSpec((tm, d), row),
        out_shape=jax.ShapeDtypeStruct((n, d), F32),
        compiler_params=_params("parallel"),
        name="out_proj_deepnorm",
    )(a1, a2, w, x, g, b)


TOP_K = 2
R_IDX = 0
R_PROB = TOP_K


def _router_kernel(x_ref, wr_ref, info_ref):
    logits = jnp.dot(x_ref[...], wr_ref[...], preferred_element_type=F32, precision=HIGHEST)
    lane = lax.broadcasted_iota(jnp.int32, logits.shape, 1)
    neg = jnp.float32(-jnp.inf)
    lg = jnp.where(lane < N_EXPERTS, logits, neg)
    m1 = jnp.max(lg, axis=-1, keepdims=True)
    i1 = jnp.min(jnp.where(lg == m1, lane, LANES), axis=-1, keepdims=True)
    lg2 = jnp.where(lane == i1, neg, lg)
    m2 = jnp.max(lg2, axis=-1, keepdims=True)
    i2 = jnp.min(jnp.where(lg2 == m2, lane, LANES), axis=-1, keepdims=True)
    e2 = jnp.exp(m2 - m1)
    p1 = 1.0 / (1.0 + e2)
    p2 = e2 / (1.0 + e2)
    info = jnp.where(lane == R_IDX, i1.astype(F32), 0.0)
    info = jnp.where(lane == R_IDX + 1, i2.astype(F32), info)
    info = jnp.where(lane == R_PROB, p1, info)
    info = jnp.where(lane == R_PROB + 1, p2, info)
    info_ref[...] = info


def router_top2(x, w_router, *, tm=1024):
    n, d = x.shape
    tm = min(tm, n)
    return pl.pallas_call(
        _router_kernel,
        grid=(n // tm,),
        in_specs=[pl.BlockSpec((tm, d), lambda i: (i, 0)), pl.BlockSpec((d, LANES), lambda i: (0, 0))],
        out_specs=pl.BlockSpec((tm, LANES), lambda i: (i, 0)),
        out_shape=jax.ShapeDtypeStruct((n, LANES), F32),
        compiler_params=_params("parallel"),
        name="router_top2",
    )(x, w_router)


def _combine_kernel(x_ref, y0_ref, y1_ref, info_ref, g_ref, b_ref, out_ref):
    sub = y0_ref[...] * info_ref[:, R_PROB:R_PROB + 1] + y1_ref[...] * info_ref[:, R_PROB + 1:R_PROB + 2]
    out_ref[...] = _deepnorm_rows(x_ref[...], sub, g_ref[...], b_ref[...])


def combine_deepnorm(x, y_slots, info, g, b, *, tc=512):
    n, d = x.shape
    tc = min(tc, n)
    nt = n // tc
    row = lambda i: (i, 0)
    fixed = lambda i: (0, 0)
    return pl.pallas_call(
        _combine_kernel,
        grid=(nt,),
        in_specs=[pl.BlockSpec((tc, d), row), pl.BlockSpec((tc, d), row),
                  pl.BlockSpec((tc, d), lambda i: (i + nt, 0)), pl.BlockSpec((tc, LANES), row),
                  pl.BlockSpec((1, d), fixed), pl.BlockSpec((1, d), fixed)],
        out_specs=pl.BlockSpec((tc, d), row),
        out_shape=jax.ShapeDtypeStruct((n, d), F32),
        compiler_params=_params("parallel"),
        name="combine_deepnorm",
    )(x, y_slots, y_slots, info, g, b)


DMA_ISSUE_UNROLL = 8


def _start_row_copies(src_ref, src_row, dst_ref, dst_row, sem, n_rows):
    def issue(r, carry):
        pltpu.make_async_copy(src_ref.at[pl.ds(src_row(r), 1)], dst_ref.at[pl.ds(dst_row(r), 1)], sem).start()
        return carry

    lax.fori_loop(0, n_rows, issue, 0, unroll=DMA_ISSUE_UNROLL)


def _expert_kernel(te_ref, src_ref, src_next_ref, dst_prev_ref, x_hbm, wg_ref, wu_ref, wd_ref, y_hbm,
                   xbuf_ref, ybuf_ref, acc_ref, xb_ref, xsem, ysem):
    i = pl.program_id(0)
    f = pl.program_id(1)
    n_i = pl.num_programs(0)
    n_f = pl.num_programs(1)
    tm = acc_ref.shape[0]
    slot = i % 2

    def wait_tile(src, dst, sem):
        pltpu.make_async_copy(src, dst, sem).wait()

    def swiglu_step(first):
        xb = xb_ref[...]
        hg = jnp.dot(xb, wg_ref[...], preferred_element_type=F32)
        hu = jnp.dot(xb, wu_ref[...], preferred_element_type=F32)
        h = (hg * _sigmoid(hg)) * hu
        out = jnp.dot(h.astype(BF16), wd_ref[...], preferred_element_type=F32)
        acc_ref[...] = out if first else acc_ref[...] + out

    @pl.when(f == 0)
    def _():
        @pl.when(i == 0)
        def _():
            _start_row_copies(x_hbm, lambda r: src_ref[0, r], xbuf_ref.at[0], lambda r: r, xsem.at[0], tm)
            ybuf_ref[...] = jnp.zeros_like(ybuf_ref)

        wait_tile(x_hbm.at[pl.ds(0, tm)], xbuf_ref.at[slot], xsem.at[slot])
        xb_ref[...] = xbuf_ref[slot].astype(BF16)
        swiglu_step(True)
        for r in range(tm):
            pltpu.make_async_copy(ybuf_ref.at[pl.ds(r, 1)], y_hbm.at[pl.ds(dst_prev_ref[0, r], 1)], ysem).start()
            pltpu.make_async_copy(x_hbm.at[pl.ds(src_next_ref[0, r], 1)], xbuf_ref.at[1 - slot, pl.ds(r, 1)],
                                  xsem.at[1 - slot]).start()

    @pl.when(f > 0)
    def _():
        swiglu_step(False)

    @pl.when(f == n_f - 1)
    def _():
        wait_tile(ybuf_ref, y_hbm.at[pl.ds(0, tm)], ysem)
        ybuf_ref[...] = acc_ref[...]

        @pl.when(i == n_i - 1)
        def _():
            wait_tile(x_hbm.at[pl.ds(0, tm)], xbuf_ref.at[1 - slot], xsem.at[1 - slot])


def expert_swiglu(x, src_rows, dst_rows, tile_expert, wg, wu, wd, *, tm, tf=1792):
    d = x.shape[1]
    r = src_rows.shape[0]
    ff = wg.shape[2]
    n_tiles = r // tm
    last = n_tiles - 1
    idx_tile = lambda u: u.reshape(n_tiles, 1, tm)
    smem_tile = lambda index: pl.BlockSpec((None, 1, tm), index, memory_space=pltpu.SMEM)
    tile = lambda i: jnp.minimum(i, last)
    grid_spec = pltpu.PrefetchScalarGridSpec(
        num_scalar_prefetch=1,
        grid=(n_tiles + 1, ff // tf),
        in_specs=[smem_tile(lambda i, f, te: (tile(i), 0, 0)),
                  smem_tile(lambda i, f, te: (tile(i + 1), 0, 0)),
                  smem_tile(lambda i, f, te: (jnp.maximum(i - 1, 0), 0, 0)),
                  pl.BlockSpec(memory_space=pl.ANY),
                  pl.BlockSpec((None, d, tf), lambda i, f, te: (te[tile(i)], 0, f)),
                  pl.BlockSpec((None, d, tf), lambda i, f, te: (te[tile(i)], 0, f)),
                  pl.BlockSpec((None, tf, d), lambda i, f, te: (te[tile(i)], f, 0))],
        out_specs=pl.BlockSpec(memory_space=pl.ANY),
        scratch_shapes=[pltpu.VMEM((2, tm, d), F32), pltpu.VMEM((tm, d), F32), pltpu.VMEM((tm, d), F32),
                        pltpu.VMEM((tm, d), BF16), pltpu.SemaphoreType.DMA((2,)), pltpu.SemaphoreType.DMA(())],
    )
    return pl.pallas_call(
        _expert_kernel,
        grid_spec=grid_spec,
        out_shape=jax.ShapeDtypeStruct((r, d), F32),
        compiler_params=_params("arbitrary", "arbitrary"),
        name="expert_swiglu",
    )(tile_expert, idx_tile(src_rows), idx_tile(src_rows), idx_tile(dst_rows), x, wg, wu, wd)


def moe_top2_deepnorm(x, w_router, wg, wu, wd, g, b, *, tm=512):
    n = x.shape[0]
    ne = wg.shape[0]
    info = router_top2(x, w_router)
    e_flat = info[:, R_IDX:R_IDX + TOP_K].astype(jnp.int32).reshape(-1)

    onehot = (e_flat[:, None] == jnp.arange(ne, dtype=jnp.int32)[None, :]).astype(jnp.int32)
    csum = jnp.cumsum(onehot, axis=0)
    rank = jnp.sum((csum - 1) * onehot, axis=1)
    padded = ((csum[-1] + tm - 1) // tm) * tm
    ends = jnp.cumsum(padded)
    dest = (ends - padded)[e_flat] + rank
    n_rows = n * TOP_K + ne * tm
    n_tiles = n_rows // tm
    assign = jnp.full((n_rows,), -1, jnp.int32).at[dest].set(jnp.arange(n * TOP_K, dtype=jnp.int32))
    token, slot = assign // TOP_K, assign % TOP_K
    src_rows = jnp.where(assign >= 0, token, 0)
    pad_rank = jnp.cumsum((assign < 0).astype(jnp.int32)) - 1
    dst_rows = jnp.where(assign >= 0, slot * n + token, n * TOP_K + pad_rank)
    tile_start = jnp.arange(n_tiles, dtype=jnp.int32) * tm
    tile_expert = jnp.minimum(jnp.sum((tile_start[:, None] >= ends[None, :]).astype(jnp.int32), axis=1), ne - 1)

    y_slots = expert_swiglu(x, src_rows, dst_rows, tile_expert, wg, wu, wd, tm=tm)
    return combine_deepnorm(x, y_slots, info, g, b)


def _ffn_kernel(x_ref, wg_ref, wu_ref, wd_ref, g_ref, b_ref, o_ref, acc_ref, xb_ref):
    f = pl.program_id(1)

    @pl.when(f == 0)
    def _():
        acc_ref[...] = jnp.zeros_like(acc_ref)
        xb_ref[...] = x_ref[...].astype(BF16)

    xb = xb_ref[...]
    hg = jnp.dot(xb, wg_ref[...], preferred_element_type=F32)
    hu = jnp.dot(xb, wu_ref[...], preferred_element_type=F32)
    h = (hg * _sigmoid(hg)) * hu
    acc_ref[...] += jnp.dot(h.astype(BF16), wd_ref[...], preferred_element_type=F32)

    @pl.when(f == pl.num_programs(1) - 1)
    def _():
        o_ref[...] = _deepnorm_rows(x_ref[...], acc_ref[...], g_ref[...], b_ref[...])


def ffn_deepnorm(x, wg, wu, wd, g, b, *, tm=512, tf=1792):
    n, d = x.shape
    ff = wg.shape[1]
    tm = min(tm, n)
    row = lambda i, f: (i, 0)
    fixed = lambda i, f: (0, 0)
    return pl.pallas_call(
        _ffn_kernel,
        grid=(n // tm, ff // tf),
        in_specs=[pl.BlockSpec((tm, d), row),
                  pl.BlockSpec((d, tf), lambda i, f: (0, f)),
                  pl.BlockSpec((d, tf), lambda i, f: (0, f)),
                  pl.BlockSpec((tf, d), lambda i, f: (f, 0)),
                  pl.BlockSpec((1, d), fixed), pl.BlockSpec((1, d), fixed)],
        out_specs=pl.BlockSpec((tm, d), row),
        out_shape=jax.ShapeDtypeStruct((n, d), F32),
        scratch_shapes=[pltpu.VMEM((tm, d), F32), pltpu.VMEM((tm, d), BF16)],
        compiler_params=_params("parallel", "arbitrary"),
        name="ffn_deepnorm",
    )(x, wg, wu, wd, g, b)


CONV_HISTORY = SUBLANES


def _ssd_kernel(xs_ref, b_ref, c_ref, z_ref, dt_ref, cw_ref, cb_ref, dtb_ref, alog_ref, dskip_ref, mnorm_ref,
                y_ref, state_ref, ext_ref, yacc_ref):
    l, hp = xs_ref.shape
    gn = b_ref.shape[1]
    p = M_HEAD_DIM
    ns = M_STATE
    nh = hp // p
    hpg = nh // M_GROUPS
    kc = cw_ref.shape[0]
    hist = CONV_HISTORY

    @pl.when(pl.program_id(1) == 0)
    def _():
        state_ref[...] = jnp.zeros_like(state_ref)
        ext_ref[0:hist, :] = jnp.zeros((hist, ext_ref.shape[1]), F32)

    ext_ref[hist:, 0:hp] = xs_ref[...]
    ext_ref[hist:, hp:hp + gn] = b_ref[...]
    ext_ref[hist:, hp + gn:] = c_ref[...]
    conv = cb_ref[...]
    for i in range(kc):
        conv = conv + cw_ref[i:i + 1, :] * ext_ref[pl.ds(hist - (kc - 1) + i, l), :]
    ext_ref[0:hist, :] = ext_ref[l:l + hist, :]
    xbc = conv * _sigmoid(conv)
    xs = xbc[:, :hp]

    dt_pre = dt_ref[...] + dtb_ref[...]
    dt = jnp.maximum(dt_pre, 0.0) + jnp.log(1.0 + jnp.exp(-jnp.abs(dt_pre)))
    a = -jnp.exp(alog_ref[...])

    row = lax.broadcasted_iota(jnp.int32, (l, l), 0)
    col = lax.broadcasted_iota(jnp.int32, (l, l), 1)
    causal = row >= col
    acs = jnp.dot(causal.astype(F32), dt * a, preferred_element_type=F32, precision=HIGHEST)
    acst = acs.T
    dtt = dt.T
    acs_last = acs[l - 1:l, :]
    w_end = jnp.exp(acs_last - acs) * dt
    exp_acs = jnp.exp(acs)
    exp_last = jnp.exp(acs_last)

    for g in range(M_GROUPS):
        b_g = xbc[:, hp + g * ns:hp + (g + 1) * ns]
        c_g = xbc[:, hp + gn + g * ns:hp + gn + (g + 1) * ns].astype(BF16)
        bt_g = b_g.T.astype(BF16)
        cb = jnp.dot(c_g, bt_g, preferred_element_type=F32)
        for hh in range(hpg):
            h = g * hpg + hh
            x_h = xs[:, h * p:(h + 1) * p]
            seg = acs[:, h:h + 1] - acst[h:h + 1, :]
            decay = jnp.where(causal, jnp.exp(seg), 0.0)
            m = cb * decay * dtt[h:h + 1, :]
            st = state_ref[h]
            y = jnp.dot(m.astype(BF16), x_h.astype(BF16), preferred_element_type=F32)
            y = y + jnp.dot(c_g, st.astype(BF16), preferred_element_type=F32) * exp_acs[:, h:h + 1]
            yacc_ref[:, h * p:(h + 1) * p] = y
            xw = (x_h * w_end[:, h:h + 1]).astype(BF16)
            state_ref[h] = st * exp_last[:, h:h + 1] + jnp.dot(bt_g, xw, preferred_element_type=F32)

    z = z_ref[...]
    y = (yacc_ref[...] + dskip_ref[...] * xs) * (z * _sigmoid(z))
    gw = hp // M_GROUPS
    for g in range(M_GROUPS):
        seg = y[:, g * gw:(g + 1) * gw]
        ms = jnp.mean(seg * seg, axis=-1, keepdims=True)
        y_ref[:, g * gw:(g + 1) * gw] = (seg * lax.rsqrt(ms + LN_EPS)
                                         * mnorm_ref[:, g * gw:(g + 1) * gw]).astype(y_ref.dtype)


def ssd_mixer(proj, cols, conv_w, conv_b, dt_bias, a_log, d_skip, m_norm):
    bsz, t, _ = proj.shape
    hp = m_norm.shape[1]
    gn = M_GROUPS * M_STATE
    conv_dim = hp + 2 * gn
    nh = hp // M_HEAD_DIM
    l = CHUNK

    def col_block(width, off):
        blk = off // width
        return pl.BlockSpec((None, l, width), lambda b, c: (b, c, blk))

    fixed = lambda b, c: (0, 0)
    return pl.pallas_call(
        _ssd_kernel,
        grid=(bsz, t // l),
        in_specs=[col_block(hp, cols["xs"]), col_block(gn, cols["B"]), col_block(gn, cols["C"]),
                  col_block(hp, cols["z"]), col_block(LANES, cols["dt"]),
                  pl.BlockSpec(conv_w.shape, fixed), pl.BlockSpec((1, conv_dim), fixed),
                  pl.BlockSpec((1, LANES), fixed), pl.BlockSpec((1, LANES), fixed),
                  pl.BlockSpec((1, hp), fixed), pl.BlockSpec((1, hp), fixed)],
        out_specs=pl.BlockSpec((None, l, hp), lambda b, c: (b, c, 0)),
        out_shape=jax.ShapeDtypeStruct((bsz, t, hp), BF16),
        scratch_shapes=[pltpu.VMEM((nh, M_STATE, M_HEAD_DIM), F32),
                        pltpu.VMEM((CONV_HISTORY + l, conv_dim), F32),
                        pltpu.VMEM((l, hp), F32)],
        compiler_params=_params("parallel", "arbitrary"),
        name="ssd_mixer",
    )(proj, proj, proj, proj, proj, conv_w, conv_b, dt_bias, a_log, d_skip, m_norm)


EXP_MINUS_HALF = float(np.exp(-0.5))


def _head_sum(x, ones):
    hi = x.astype(BF16)
    lo = (x - hi.astype(F32)).astype(BF16)
    outs = []
    for j in range(x.shape[1] // MXU_DIM):
        sl = slice(j * MXU_DIM, (j + 1) * MXU_DIM)
        outs.append(jnp.dot(hi[:, sl], ones, preferred_element_type=F32)
                    + jnp.dot(lo[:, sl], ones, preferred_element_type=F32))
    return jnp.concatenate(outs, axis=-1)


def _rwkv_prep(r_ref, k_ref, v_ref, lo_ref, mu_ref, w0_ref, a0_ref, kk_ref, ka_ref, rk_ref,
               wup_ref, aup_ref, gup_ref, ones_ref, prev_ref):
    tl, c = r_ref.shape
    first_row = lax.broadcasted_iota(jnp.int32, (tl, 1), 0) == 0

    def shift_mix(x_ref, off):
        x = x_ref[...]
        width = x.shape[1]
        prev = jnp.where(first_row, prev_ref[0:1, off:off + width], pltpu.roll(x, 1, 0))
        prev_ref[0:1, off:off + width] = x[tl - 1:tl, :]
        return x + (prev - x) * mu_ref[:, off:off + width]

    r = shift_mix(r_ref, 0)
    k = shift_mix(k_ref, c)
    v = shift_mix(v_ref, 2 * c)
    lo = shift_mix(lo_ref, 3 * c)
    lo_a = lo[:, 0:LANES]
    lo_g = lo[:, LORA_GATE_OFF:LORA_GATE_OFF + LORA_GATE_PAD]
    w = w0_ref[...] + jnp.dot(jnp.tanh(lo_a).astype(BF16), wup_ref[...], preferred_element_type=F32)
    log_decay = -EXP_MINUS_HALF * _sigmoid(w)
    iclr = _sigmoid(a0_ref[...] + jnp.dot(lo_a.astype(BF16), aup_ref[...], preferred_element_type=F32))
    gate = jnp.dot(_sigmoid(lo_g).astype(BF16), gup_ref[...], preferred_element_type=F32)
    ones = ones_ref[...]
    kk = k * kk_ref[...]
    kk = kk / jnp.maximum(jnp.sqrt(_head_sum(kk * kk, ones)), 1e-12)
    k2 = k * (1.0 + (iclr - 1.0) * ka_ref[...])
    bonus = _head_sum(r * k2 * rk_ref[...], ones) * v
    return r, log_decay, k2, v, -kk, kk * iclr, bonus, gate


RWKV_CHUNK = 64


def _nt(a, b):
    return lax.dot_general(a, b, (((1,), (1,)), ((), ())), preferred_element_type=F32)


def _tn(a, b):
    return lax.dot_general(a, b, (((0,), (0,)), ((), ())), preferred_element_type=F32)


def _rwkv_chunk_kernel(r_ref, k_ref, v_ref, lo_ref, mu_ref, w0_ref, a0_ref, kk_ref, ka_ref, rk_ref,
                       wup_ref, aup_ref, gup_ref, ones_ref, rlnw_ref, rlnb_ref,
                       y_ref, state_ref, yacc_ref, prev_ref):
    l, c = r_ref.shape
    hd = R_HEAD_DIM
    n_pair = c // LANES

    @pl.when(pl.program_id(1) == 0)
    def _():
        state_ref[...] = jnp.zeros_like(state_ref)
        prev_ref[...] = jnp.zeros_like(prev_ref)

    r_all, lw, k_all, v_all, a_all, b_all, bonus, gate = _rwkv_prep(
        r_ref, k_ref, v_ref, lo_ref, mu_ref, w0_ref, a0_ref, kk_ref, ka_ref, rk_ref,
        wup_ref, aup_ref, gup_ref, ones_ref, prev_ref)
    row = lax.broadcasted_iota(jnp.int32, (l, l), 0)
    col = lax.broadcasted_iota(jnp.int32, (l, l), 1)
    g = jnp.dot((row >= col).astype(F32), lw, preferred_element_type=F32, precision=HIGHEST)
    g_last = g[l - 1:l, :]
    e_g = jnp.exp(g)
    e_ng = jnp.exp(-g)
    e_last = jnp.exp(g_last)
    at_all = a_all * jnp.exp(g - lw)
    rt_all = r_all * e_g
    bt_all = b_all * e_ng
    kt_all = k_all * e_ng

    wi = lax.broadcasted_iota(jnp.int32, (l, 2 * l), 0)
    wl = lax.broadcasted_iota(jnp.int32, (l, 2 * l), 1)
    wj = jnp.where(wl >= l, wl - l, wl)
    strict = wj < wi
    incl = wj <= wi
    eye_w = (wj == wi).astype(F32)
    first_w = wl < l
    first_c = lax.broadcasted_iota(jnp.int32, (l, LANES), 1) < hd
    bi = lax.broadcasted_iota(jnp.int32, (LANES, LANES), 0)
    bj = lax.broadcasted_iota(jnp.int32, (LANES, LANES), 1)
    same_head = (bi < hd) == (bj < hd)

    def block_rows(x, first):
        return jnp.concatenate([jnp.where(first, x, 0.0), jnp.where(first, 0.0, x)], axis=0)

    pairs = range(n_pair)
    lanes_of = [slice(p * LANES, (p + 1) * LANES) for p in pairs]
    dot = lambda x, y: jnp.dot(x, y, preferred_element_type=F32)
    ht = [state_ref[p] for p in pairs]
    lhs_ar = [jnp.concatenate([at_all[:, s], rt_all[:, s]], axis=0).astype(BF16) for s in lanes_of]
    gram = [_nt(lhs_ar[p], jnp.concatenate([block_rows(bt_all[:, s], first_c),
                                            block_rows(kt_all[:, s], first_c)], axis=0).astype(BF16))
            for p, s in zip(pairs, lanes_of)]
    h_ar = [_nt(lhs_ar[p], ht[p].astype(BF16)) for p in pairs]
    n_w = [jnp.where(strict, gram[p][0:l, 0:2 * l], 0.0) for p in pairs]
    t_w = [eye_w + n_w[p] for p in pairs]
    p_w = [dot(n_w[p].astype(BF16), block_rows(n_w[p], first_w).astype(BF16)) for p in pairs]
    for _ in range(int(np.log2(l)) - 1):
        res = [dot(jnp.concatenate([t_w[p], p_w[p]], axis=0).astype(BF16),
                   block_rows(p_w[p], first_w).astype(BF16)) for p in pairs]
        t_w = [t_w[p] + res[p][0:l] for p in pairs]
        p_w = [res[p][l:2 * l] for p in pairs]
    v_bd = [block_rows(v_all[:, s], first_c).astype(BF16) for s in lanes_of]
    rhs_u = [h_ar[p][0:l] + dot(jnp.where(strict, gram[p][0:l, 2 * l:4 * l], 0.0).astype(BF16), v_bd[p])
             for p in pairs]
    u = [dot(t_w[p].astype(BF16), block_rows(rhs_u[p], first_c).astype(BF16)) for p in pairs]
    for p, s in zip(pairs, lanes_of):
        rbk_w = jnp.where(jnp.concatenate([incl, incl], axis=1), gram[p][l:2 * l, :], 0.0)
        yacc_ref[:, s] = h_ar[p][l:2 * l] + dot(
            rbk_w.astype(BF16), jnp.concatenate([block_rows(u[p], first_c).astype(BF16), v_bd[p]], axis=0))
    for p, s in zip(pairs, lanes_of):
        e_l = e_last[:, s]
        upd = _tn(jnp.concatenate([u[p], v_all[:, s]], axis=0).astype(BF16),
                  jnp.concatenate([bt_all[:, s] * e_l, kt_all[:, s] * e_l], axis=0).astype(BF16))
        state_ref[p] = ht[p] * e_l + jnp.where(same_head, upd, 0.0)

    y = yacc_ref[...]
    ones = ones_ref[...]
    inv = 1.0 / hd
    mu = _head_sum(y, ones) * inv
    yc = y - mu
    var = _head_sum(yc * yc, ones) * inv
    yn = yc * lax.rsqrt(var + R_LN_EPS)
    y_ref[...] = (((yn * rlnw_ref[...] + rlnb_ref[...]) + bonus) * gate).astype(y_ref.dtype)


def rwkv_mixer(proj, cols, mu, w0, a0, k_k, k_a, r_k, w_up, a_up, g_up, rln_w, rln_b):
    bsz, t, _ = proj.shape
    c = w0.shape[1]
    l = RWKV_CHUNK
    ones = jnp.asarray(np.kron(np.eye(MXU_DIM // R_HEAD_DIM), np.ones((R_HEAD_DIM, R_HEAD_DIM))), BF16)

    def col_block(width, off):
        blk = off // width
        return pl.BlockSpec((None, l, width), lambda i, j: (i, j, blk))

    fixed = lambda shape: pl.BlockSpec(shape, lambda i, j: (0, 0))
    return pl.pallas_call(
        _rwkv_chunk_kernel,
        grid=(bsz, t // l),
        in_specs=[col_block(c, cols["r"]), col_block(c, cols["k"]), col_block(c, cols["v"]),
                  col_block(LORA_BLOCK, cols["lora"]), fixed(mu.shape)]
        + [fixed((1, c))] * 5 + [fixed(w_up.shape), fixed(a_up.shape), fixed(g_up.shape), fixed(ones.shape),
                                 fixed((1, c)), fixed((1, c))],
        out_specs=pl.BlockSpec((None, l, c), lambda i, j: (i, j, 0)),
        out_shape=jax.ShapeDtypeStruct((bsz, t, c), BF16),
        scratch_shapes=[pltpu.VMEM((c // LANES, LANES, LANES), F32), pltpu.VMEM((l, c), F32),
                        pltpu.VMEM((SUBLANES, mu.shape[1]), F32)],
        compiler_params=_params("parallel", "arbitrary"),
        name="rwkv_mixer",
    )(proj, proj, proj, proj, mu, w0, a0, k_k, k_a, r_k, w_up, a_up, g_up, ones, rln_w, rln_b)


def _retention_kernel(q_ref, k_ref, v_ref, g_ref, pos_ref, freq_ref, y_ref, state_ref):
    l = q_ref.shape[0]
    dk = RET_QK_HEAD
    dv = RET_V_HEAD
    half = dk // 2

    @pl.when(pl.program_id(1) == 0)
    def _():
        state_ref[...] = jnp.zeros_like(state_ref)

    ang = pos_ref[...].astype(F32) * freq_ref[...]
    cos = jnp.cos(ang)
    sin = jnp.sin(ang)

    def rotate(u):
        ue = u[:, :half]
        uo = u[:, half:]
        return jnp.concatenate([ue * cos - uo * sin, uo * cos + ue * sin], axis=-1)

    row = lax.broadcasted_iota(jnp.int32, (l, l), 0)
    col = lax.broadcasted_iota(jnp.int32, (l, l), 1)
    rel = (row - col).astype(F32)
    causal = row >= col
    idx = lax.broadcasted_iota(jnp.int32, (l, 1), 0).astype(F32)
    for h in range(RET_HEADS):
        log_gamma = float(np.log(np.float32(1.0) - np.float32(2.0) ** np.float32(-5.0 - h)))
        intra = jnp.where(causal, jnp.exp(log_gamma * jnp.where(causal, rel, 0.0)), 0.0)
        q_decay = jnp.exp(log_gamma * (idx + 1.0))
        k_decay = jnp.exp(log_gamma * (l - 1.0 - idx))
        chunk_decay = float(np.exp(np.float32(log_gamma) * np.float32(l)))
        q_h = rotate(q_ref[:, h * dk:(h + 1) * dk]).astype(BF16)
        k_h = rotate(k_ref[:, h * dk:(h + 1) * dk]) * (dk ** -0.5)
        v_h = v_ref[:, h * dv:(h + 1) * dv].astype(BF16)
        s = lax.dot_general(q_h, k_h.astype(BF16), (((1,), (1,)), ((), ())),
                            preferred_element_type=F32) * intra
        st = state_ref[h]
        y = jnp.dot(s.astype(BF16), v_h, preferred_element_type=F32)
        y = y + jnp.dot(q_h, st.astype(BF16), preferred_element_type=F32) * q_decay
        kd = (k_h * k_decay).astype(BF16)
        state_ref[h] = st * chunk_decay + lax.dot_general(
            kd, v_h, (((0,), (0,)), ((), ())), preferred_element_type=F32)
        mu = jnp.mean(y, axis=-1, keepdims=True)
        yc = y - mu
        var = jnp.mean(yc * yc, axis=-1, keepdims=True)
        g_h = g_ref[:, h * dv:(h + 1) * dv]
        y_ref[:, h * dv:(h + 1) * dv] = (g_h * _sigmoid(g_h) * (yc * lax.rsqrt(var + LN_EPS))).astype(y_ref.dtype)


def retention_mixer(proj, positions, inv_freq):
    bsz, t, _ = proj.shape
    hk = RET_HEADS * RET_QK_HEAD
    hv = RET_HEADS * RET_V_HEAD
    l = CHUNK
    return pl.pallas_call(
        _retention_kernel,
        grid=(bsz, t // l),
        in_specs=[pl.BlockSpec((None, l, hk), lambda b, c: (b, c, 0)),
                  pl.BlockSpec((None, l, hk), lambda b, c: (b, c, 1)),
                  pl.BlockSpec((None, l, hv), lambda b, c: (b, c, 1)),
                  pl.BlockSpec((None, l, hv), lambda b, c: (b, c, 2)),
                  pl.BlockSpec((None, l, 1), lambda b, c: (b, c, 0)),
                  pl.BlockSpec((1, RET_QK_HEAD // 2), lambda b, c: (0, 0))],
        out_specs=pl.BlockSpec((None, l, hv), lambda b, c: (b, c, 0)),
        out_shape=jax.ShapeDtypeStruct((bsz, t, hv), BF16),
        scratch_shapes=[pltpu.VMEM((RET_HEADS, RET_QK_HEAD, RET_V_HEAD), F32)],
        compiler_params=_params("parallel", "arbitrary"),
        name="retention_mixer",
    )(proj, proj, proj, proj, positions, inv_freq)


def _even_layer(x, w_in, conv_w, conv_b, dt_bias, a_log, d_skip, m_norm, mu_shift, w0, w_up, a0, a_up,
                g_up, k_k, k_a, r_k, rln_w, rln_b, w_out, ln1_g, ln1_b, wg, wu, wd, ln2_g, ln2_b):
    bsz, t, d = x.shape
    n = bsz * t
    m_inner = m_norm.shape[0]
    m_heads = dt_bias.shape[0]
    c = w0.shape[0]
    r_heads = c // R_HEAD_DIM
    gn = M_GROUPS * M_STATE
    o_xbc = m_inner
    o_dt = o_xbc + m_inner + 2 * gn
    o_rw = o_dt + m_heads
    o_lora = o_rw + 3 * c
    n_lora = R_DECAY_LORA + R_AAA_LORA + R_GATE_LORA
    x2 = x.reshape(n, d)

    zeros = lambda k: jnp.zeros((d, k), w_in.dtype)
    w_in_p = jnp.concatenate([
        w_in[:, o_rw:o_lora],
        w_in[:, :m_inner],
        w_in[:, o_xbc:o_dt],
        w_in[:, o_lora:o_lora + n_lora], zeros(LORA_DT_OFF - n_lora),
        w_in[:, o_dt:o_rw], zeros(LORA_BLOCK - LORA_DT_OFF - m_heads)], axis=1).astype(BF16)
    cols = {"r": 0, "k": c, "v": 2 * c, "z": 3 * c, "xs": 3 * c + m_inner, "B": 3 * c + 2 * m_inner,
            "C": 3 * c + 2 * m_inner + gn, "lora": 3 * c + 2 * m_inner + 2 * gn}
    cols["dt"] = cols["lora"] + LORA_DT_OFF
    proj = matmul(x2, w_in_p).reshape(bsz, t, -1)

    pad_lane = lambda u: jnp.pad(u.reshape(1, -1), ((0, 0), (0, LANES - u.shape[0])))
    y_ssd = ssd_mixer(proj, cols, conv_w, conv_b.reshape(1, -1), pad_lane(dt_bias), pad_lane(a_log),
                      jnp.repeat(d_skip, M_HEAD_DIM).reshape(1, -1), m_norm.reshape(1, -1))

    mu_p = jnp.pad(mu_shift, (0, LORA_BLOCK - n_lora)).reshape(1, -1)
    w_up_p = jnp.pad(w_up, ((0, LANES - R_DECAY_LORA), (0, 0))).astype(BF16)
    a_up_p = jnp.pad(a_up, ((R_DECAY_LORA, LANES - R_DECAY_LORA - R_AAA_LORA), (0, 0))).astype(BF16)
    g_up_p = jnp.pad(g_up, ((0, LORA_GATE_PAD - R_GATE_LORA), (0, 0))).astype(BF16)
    row = lambda u: u.reshape(1, -1)
    y_rwkv = rwkv_mixer(proj, cols, mu_p, row(w0), row(a0), row(k_k), row(k_a), row(r_k),
                        w_up_p, a_up_p, g_up_p, row(rln_w), row(rln_b))
    x2 = out_proj_deepnorm(y_ssd.reshape(n, m_inner), y_rwkv.reshape(n, c), w_out.astype(BF16), x2,
                           row(ln1_g), row(ln1_b))
    x2 = ffn_deepnorm(x2, wg.astype(BF16), wu.astype(BF16), wd.astype(BF16), row(ln2_g), row(ln2_b))
    return x2.reshape(bsz, t, d)


def _odd_layer(x, positions, w_in, w_out, ln1_g, ln1_b, router, wg, wu, wd, ln2_g, ln2_b):
    bsz, t, d = x.shape
    n = bsz * t
    ret_qk = RET_HEADS * RET_QK_HEAD
    x2 = x.reshape(n, d)
    head_perm = np.concatenate([np.arange(0, RET_QK_HEAD, 2), np.arange(1, RET_QK_HEAD, 2)])
    qk_perm = (np.arange(2 * RET_HEADS)[:, None] * RET_QK_HEAD + head_perm[None, :]).reshape(-1)
    w_in_p = jnp.concatenate([w_in[:, qk_perm], w_in[:, 2 * ret_qk:]], axis=1).astype(BF16)
    proj = matmul(x2, w_in_p).reshape(bsz, t, -1)
    inv_freq = (1.0 / (10000.0 ** jnp.linspace(0.0, 1.0, RET_QK_HEAD // 2, dtype=F32))).reshape(1, -1)
    a = retention_mixer(proj, positions.reshape(bsz, t, 1), inv_freq)
    row = lambda u: u.reshape(1, -1)
    x2 = out_proj_deepnorm(a.reshape(n, -1), None, w_out.astype(BF16), x2, row(ln1_g), row(ln1_b))
    x2 = moe_top2_deepnorm(x2, jnp.pad(router, ((0, 0), (0, LANES - router.shape[1]))),
                           wg.astype(BF16), wu.astype(BF16), wd.astype(BF16), row(ln2_g), row(ln2_b))
    return x2.reshape(bsz, t, d)


def kernel(x, positions, ev_w_in, ev_conv_w, ev_conv_b, ev_dt_bias, ev_a_log, ev_d_skip, ev_m_norm, ev_mu_shift, ev_w0, ev_w_up, ev_a0, ev_a_up, ev_g_up, ev_k_k, ev_k_a, ev_r_k, ev_rln_w, ev_rln_b, ev_w_out, ev_ln1_g, ev_ln1_b, ev_ffn_wg, ev_ffn_wu, ev_ffn_wd, ev_ln2_g, ev_ln2_b, od_w_in, od_w_out, od_ln1_g, od_ln1_b, od_router, od_moe_wg, od_moe_wu, od_moe_wd, od_ln2_g, od_ln2_b):
    for layer in range(DEPTH):
        i = layer // 2
        if layer % 2 == 0:
            x = _even_layer(x, ev_w_in[i], ev_conv_w[i], ev_conv_b[i], ev_dt_bias[i], ev_a_log[i], ev_d_skip[i],
                            ev_m_norm[i], ev_mu_shift[i], ev_w0[i], ev_w_up[i], ev_a0[i], ev_a_up[i], ev_g_up[i],
                            ev_k_k[i], ev_k_a[i], ev_r_k[i], ev_rln_w[i], ev_rln_b[i], ev_w_out[i],
                            ev_ln1_g[i], ev_ln1_b[i], ev_ffn_wg[i], ev_ffn_wu[i], ev_ffn_wd[i],
                            ev_ln2_g[i], ev_ln2_b[i])
        else:
            x = _odd_layer(x, positions, od_w_in[i], od_w_out[i], od_ln1_g[i], od_ln1_b[i], od_router[i],
                           od_moe_wg[i], od_moe_wu[i], od_moe_wd[i], od_ln2_g[i], od_ln2_b[i])
    return x
```

```python
import jax
import jax.numpy as jnp
import numpy as np
from jax import lax
from jax.experimental import pallas as pl
from jax.experimental.pallas import tpu as pltpu

F32 = jnp.float32
BF16 = jnp.bfloat16
HIGHEST = lax.Precision.HIGHEST

CHUNK = 128
M_HEAD_DIM = 64
M_GROUPS = 2
M_STATE = 128
R_HEAD_DIM = 64
R_DECAY_LORA = 64
R_AAA_LORA = 64
R_GATE_LORA = 160
R_LN_EPS = 64e-5
RET_HEADS = 4
RET_QK_HEAD = 256
RET_V_HEAD = 512
N_EXPERTS = 8
LN_EPS = 1e-5
DEPTH = 2
ALPHA = (2.0 * DEPTH) ** 0.25

LANES = 128
SUBLANES = 8
MXU_DIM = 256
VMEM_LIMIT_BYTES = 56 * 1024 * 1024

LORA_BLOCK = 512
LORA_GATE_OFF = 128
LORA_GATE_PAD = 256
LORA_DT_OFF = 384


def _params(*sem):
    return pltpu.CompilerParams(dimension_semantics=sem, vmem_limit_bytes=VMEM_LIMIT_BYTES)


def _sigmoid(x):
    return jax.nn.sigmoid(x)


def _matmul_kernel(x_ref, w_ref, o_ref, xb_ref):
    @pl.when(pl.program_id(1) == 0)
    def _():
        xb_ref[...] = x_ref[...].astype(BF16)

    o_ref[...] = jnp.dot(xb_ref[...], w_ref[...], preferred_element_type=F32).astype(o_ref.dtype)


def matmul(x, w, *, tm=1024, tn=1024, out_dtype=F32):
    n, k = x.shape
    m = w.shape[1]
    tm = min(tm, n)
    return pl.pallas_call(
        _matmul_kernel,
        grid=(n // tm, m // tn),
        in_specs=[pl.BlockSpec((tm, k), lambda i, j: (i, 0)),
                  pl.BlockSpec((k, tn), lambda i, j: (0, j))],
        out_specs=pl.BlockSpec((tm, tn), lambda i, j: (i, j)),
        out_shape=jax.ShapeDtypeStruct((n, m), out_dtype),
        scratch_shapes=[pltpu.VMEM((tm, k), BF16)],
        compiler_params=_params("parallel", "arbitrary"),
        name="matmul",
    )(x, w)


def _deepnorm_rows(resid, sub, g, b):
    y = ALPHA * resid + sub
    mu = jnp.mean(y, axis=-1, keepdims=True)
    yc = y - mu
    var = jnp.mean(yc * yc, axis=-1, keepdims=True)
    return yc * lax.rsqrt(var + LN_EPS) * g + b


def _out_proj_kernel(a1_ref, a2_ref, w_ref, x_ref, g_ref, b_ref, o_ref):
    kh = a1_ref.shape[1]
    sub = jnp.dot(a1_ref[...], w_ref[:kh, :], preferred_element_type=F32)
    sub = sub + jnp.dot(a2_ref[...], w_ref[kh:, :], preferred_element_type=F32)
    o_ref[...] = _deepnorm_rows(x_ref[...], sub, g_ref[...], b_ref[...])


def out_proj_deepnorm(a1, a2, w, x, g, b, *, tm=512):
    n, d = x.shape
    kh = w.shape[0] // 2
    tm = min(tm, n)
    row = lambda i: (i, 0)
    fixed = lambda i: (0, 0)
    if a2 is None:
        a2, second = a1, pl.BlockSpec((tm, kh), lambda i: (i, 1))
    else:
        second = pl.BlockSpec((tm, kh), row)
    return pl.pallas_call(
        _out_proj_kernel,
        grid=(n // tm,),
        in_specs=[pl.BlockSpec((tm, kh), row), second,
                  pl.BlockSpec((2 * kh, d), fixed), pl.BlockSpec((tm, d), row),
                  pl.BlockSpec((1, d), fixed), pl.BlockSpec((1, d), fixed)],
        out_specs=pl.BlockSpec((tm, d), row),
        out_shape=jax.ShapeDtypeStruct((n, d), F32),
        compiler_params=_params("parallel"),
        name="out_proj_deepnorm",
    )(a1, a2, w, x, g, b)


TOP_K = 2
R_IDX = 0
R_PROB = TOP_K


def _router_kernel(x_ref, wr_ref, info_ref):
    logits = jnp.dot(x_ref[...], wr_ref[...], preferred_element_type=F32, precision=HIGHEST)
    lane = lax.broadcasted_iota(jnp.int32, logits.shape, 1)
    neg = jnp.float32(-jnp.inf)
    lg = jnp.where(lane < N_EXPERTS, logits, neg)
    m1 = jnp.max(lg, axis=-1, keepdims=True)
    i1 = jnp.min(jnp.where(lg == m1, lane, LANES), axis=-1, keepdims=True)
    lg2 = jnp.where(lane == i1, neg, lg)
    m2 = jnp.max(lg2, axis=-1, keepdims=True)
    i2 = jnp.min(jnp.where(lg2 == m2, lane, LANES), axis=-1, keepdims=True)
    e2 = jnp.exp(m2 - m1)
    p1 = 1.0 / (1.0 + e2)
    p2 = e2 / (1.0 + e2)
    info = jnp.where(lane == R_IDX, i1.astype(F32), 0.0)
    info = jnp.where(lane == R_IDX + 1, i2.astype(F32), info)
    info = jnp.where(lane == R_PROB, p1, info)
    info = jnp.where(lane == R_PROB + 1, p2, info)
    info_ref[...] = info


def router_top2(x, w_router, *, tm=1024):
    n, d = x.shape
    tm = min(tm, n)
    return pl.pallas_call(
        _router_kernel,
        grid=(n // tm,),
        in_specs=[pl.BlockSpec((tm, d), lambda i: (i, 0)), pl.BlockSpec((d, LANES), lambda i: (0, 0))],
        out_specs=pl.BlockSpec((tm, LANES), lambda i: (i, 0)),
        out_shape=jax.ShapeDtypeStruct((n, LANES), F32),
        compiler_params=_params("parallel"),
        name="router_top2",
    )(x, w_router)


def _combine_kernel(x_ref, y0_ref, y1_ref, info_ref, g_ref, b_ref, out_ref):
    sub = y0_ref[...] * info_ref[:, R_PROB:R_PROB + 1] + y1_ref[...] * info_ref[:, R_PROB + 1:R_PROB + 2]
    out_ref[...] = _deepnorm_rows(x_ref[...], sub, g_ref[...], b_ref[...])


def combine_deepnorm(x, y_slots, info, g, b, *, tc=512):
    n, d = x.shape
    tc = min(tc, n)
    nt = n // tc
    row = lambda i: (i, 0)
    fixed = lambda i: (0, 0)
    return pl.pallas_call(
        _combine_kernel,
        grid=(nt,),
        in_specs=[pl.BlockSpec((tc, d), row), pl.BlockSpec((tc, d), row),
                  pl.BlockSpec((tc, d), lambda i: (i + nt, 0)), pl.BlockSpec((tc, LANES), row),
                  pl.BlockSpec((1, d), fixed), pl.BlockSpec((1, d), fixed)],
        out_specs=pl.BlockSpec((tc, d), row),
        out_shape=jax.ShapeDtypeStruct((n, d), F32),
        compiler_params=_params("parallel"),
        name="combine_deepnorm",
    )(x, y_slots, y_slots, info, g, b)


DMA_ISSUE_UNROLL = 8


def _start_row_copies(src_ref, src_row, dst_ref, dst_row, sem, n_rows):
    def issue(r, carry):
        pltpu.make_async_copy(src_ref.at[pl.ds(src_row(r), 1)], dst_ref.at[pl.ds(dst_row(r), 1)], sem).start()
        return carry

    lax.fori_loop(0, n_rows, issue, 0, unroll=DMA_ISSUE_UNROLL)


def _expert_kernel(te_ref, src_ref, src_next_ref, dst_prev_ref, x_hbm, wg_ref, wu_ref, wd_ref, y_hbm,
                   xbuf_ref, ybuf_ref, acc_ref, xb_ref, xsem, ysem):
    i = pl.program_id(0)
    f = pl.program_id(1)
    n_i = pl.num_programs(0)
    n_f = pl.num_programs(1)
    tm = acc_ref.shape[0]
    slot = i % 2

    def wait_tile(src, dst, sem):
        pltpu.make_async_copy(src, dst, sem).wait()

    def swiglu_step(first):
        xb = xb_ref[...]
        hg = jnp.dot(xb, wg_ref[...], preferred_element_type=F32)
        hu = jnp.dot(xb, wu_ref[...], preferred_element_type=F32)
        h = (hg * _sigmoid(hg)) * hu
        out = jnp.dot(h.astype(BF16), wd_ref[...], preferred_element_type=F32)
        acc_ref[...] = out if first else acc_ref[...] + out

    @pl.when(f == 0)
    def _():
        @pl.when(i == 0)
        def _():
            _start_row_copies(x_hbm, lambda r: src_ref[0, r], xbuf_ref.at[0], lambda r: r, xsem.at[0], tm)
            ybuf_ref[...] = jnp.zeros_like(ybuf_ref)

        wait_tile(x_hbm.at[pl.ds(0, tm)], xbuf_ref.at[slot], xsem.at[slot])
        xb_ref[...] = xbuf_ref[slot].astype(BF16)
        swiglu_step(True)
        for r in range(tm):
            pltpu.make_async_copy(ybuf_ref.at[pl.ds(r, 1)], y_hbm.at[pl.ds(dst_prev_ref[0, r], 1)], ysem).start()
            pltpu.make_async_copy(x_hbm.at[pl.ds(src_next_ref[0, r], 1)], xbuf_ref.at[1 - slot, pl.ds(r, 1)],
                                  xsem.at[1 - slot]).start()

    @pl.when(f > 0)
    def _():
        swiglu_step(False)

    @pl.when(f == n_f - 1)
    def _():
        wait_tile(ybuf_ref, y_hbm.at[pl.ds(0, tm)], ysem)
        ybuf_ref[...] = acc_ref[...]

        @pl.when(i == n_i - 1)
        def _():
            wait_tile(x_hbm.at[pl.ds(0, tm)], xbuf_ref.at[1 - slot], xsem.at[1 - slot])


def expert_swiglu(x, src_rows, dst_rows, tile_expert, wg, wu, wd, *, tm, tf=1792):
    d = x.shape[1]
    r = src_rows.shape[0]
    ff = wg.shape[2]
    n_tiles = r // tm
    last = n_tiles - 1
    idx_tile = lambda u: u.reshape(n_tiles, 1, tm)
    smem_tile = lambda index: pl.BlockSpec((None, 1, tm), index, memory_space=pltpu.SMEM)
    tile = lambda i: jnp.minimum(i, last)
    grid_spec = pltpu.PrefetchScalarGridSpec(
        num_scalar_prefetch=1,
        grid=(n_tiles + 1, ff // tf),
        in_specs=[smem_tile(lambda i, f, te: (tile(i), 0, 0)),
                  smem_tile(lambda i, f, te: (tile(i + 1), 0, 0)),
                  smem_tile(lambda i, f, te: (jnp.maximum(i - 1, 0), 0, 0)),
                  pl.BlockSpec(memory_space=pl.ANY),
                  pl.BlockSpec((None, d, tf), lambda i, f, te: (te[tile(i)], 0, f)),
                  pl.BlockSpec((None, d, tf), lambda i, f, te: (te[tile(i)], 0, f)),
                  pl.BlockSpec((None, tf, d), lambda i, f, te: (te[tile(i)], f, 0))],
        out_specs=pl.BlockSpec(memory_space=pl.ANY),
        scratch_shapes=[pltpu.VMEM((2, tm, d), F32), pltpu.VMEM((tm, d), F32), pltpu.VMEM((tm, d), F32),
                        pltpu.VMEM((tm, d), BF16), pltpu.SemaphoreType.DMA((2,)), pltpu.SemaphoreType.DMA(())],
    )
    return pl.pallas_call(
        _expert_kernel,
        grid_spec=grid_spec,
        out_shape=jax.ShapeDtypeStruct((r, d), F32),
        compiler_params=_params("arbitrary", "arbitrary"),
        name="expert_swiglu",
    )(tile_expert, idx_tile(src_rows), idx_tile(src_rows), idx_tile(dst_rows), x, wg, wu, wd)


def moe_top2_deepnorm(x, w_router, wg, wu, wd, g, b, *, tm=512):
    n = x.shape[0]
    ne = wg.shape[0]
    info = router_top2(x, w_router)
    e_flat = info[:, R_IDX:R_IDX + TOP_K].astype(jnp.int32).reshape(-1)

    onehot = (e_flat[:, None] == jnp.arange(ne, dtype=jnp.int32)[None, :]).astype(jnp.int32)
    csum = jnp.cumsum(onehot, axis=0)
    rank = jnp.sum((csum - 1) * onehot, axis=1)
    padded = ((csum[-1] + tm - 1) // tm) * tm
    ends = jnp.cumsum(padded)
    dest = (ends - padded)[e_flat] + rank
    n_rows = n * TOP_K + ne * tm
    n_tiles = n_rows // tm
    assign = jnp.full((n_rows,), -1, jnp.int32).at[dest].set(jnp.arange(n * TOP_K, dtype=jnp.int32))
    token, slot = assign // TOP_K, assign % TOP_K
    src_rows = jnp.where(assign >= 0, token, 0)
    pad_rank = jnp.cumsum((assign < 0).astype(jnp.int32)) - 1
    dst_rows = jnp.where(assign >= 0, slot * n + token, n * TOP_K + pad_rank)
    tile_start = jnp.arange(n_tiles, dtype=jnp.int32) * tm
    tile_expert = jnp.minimum(jnp.sum((tile_start[:, None] >= ends[None, :]).astype(jnp.int32), axis=1), ne - 1)

    y_slots = expert_swiglu(x, src_rows, dst_rows, tile_expert, wg, wu, wd, tm=tm)
    return combine_deepnorm(x, y_slots, info, g, b)


def _ffn_kernel(x_ref, wg_ref, wu_ref, wd_ref, g_ref, b_ref, o_ref, acc_ref, xb_ref):
    f = pl.program_id(1)

    @pl.when(f == 0)
    def _():
        acc_ref[...] = jnp.zeros_like(acc_ref)
        xb_ref[...] = x_ref[...].astype(BF16)

    xb = xb_ref[...]
    hg = jnp.dot(xb, wg_ref[...], preferred_element_type=F32)
    hu = jnp.dot(xb, wu_ref[...], preferred_element_type=F32)
    h = (hg * _sigmoid(hg)) * hu
    acc_ref[...] += jnp.dot(h.astype(BF16), wd_ref[...], preferred_element_type=F32)

    @pl.when(f == pl.num_programs(1) - 1)
    def _():
        o_ref[...] = _deepnorm_rows(x_ref[...], acc_ref[...], g_ref[...], b_ref[...])


def ffn_deepnorm(x, wg, wu, wd, g, b, *, tm=512, tf=1792):
    n, d = x.shape
    ff = wg.shape[1]
    tm = min(tm, n)
    row = lambda i, f: (i, 0)
    fixed = lambda i, f: (0, 0)
    return pl.pallas_call(
        _ffn_kernel,
        grid=(n // tm, ff // tf),
        in_specs=[pl.BlockSpec((tm, d), row),
                  pl.BlockSpec((d, tf), lambda i, f: (0, f)),
                  pl.BlockSpec((d, tf), lambda i, f: (0, f)),
                  pl.BlockSpec((tf, d), lambda i, f: (f, 0)),
                  pl.BlockSpec((1, d), fixed), pl.BlockSpec((1, d), fixed)],
        out_specs=pl.BlockSpec((tm, d), row),
        out_shape=jax.ShapeDtypeStruct((n, d), F32),
        scratch_shapes=[pltpu.VMEM((tm, d), F32), pltpu.VMEM((tm, d), BF16)],
        compiler_params=_params("parallel", "arbitrary"),
        name="ffn_deepnorm",
    )(x, wg, wu, wd, g, b)


CONV_HISTORY = SUBLANES


def _ssd_kernel(xs_ref, b_ref, c_ref, z_ref, dt_ref, cw_ref, cb_ref, dtb_ref, alog_ref, dskip_ref, mnorm_ref,
                y_ref, state_ref, ext_ref, yacc_ref):
    l, hp = xs_ref.shape
    gn = b_ref.shape[1]
    p = M_HEAD_DIM
    ns = M_STATE
    nh = hp // p
    hpg = nh // M_GROUPS
    kc = cw_ref.shape[0]
    hist = CONV_HISTORY

    @pl.when(pl.program_id(1) == 0)
    def _():
        state_ref[...] = jnp.zeros_like(state_ref)
        ext_ref[0:hist, :] = jnp.zeros((hist, ext_ref.shape[1]), F32)

    ext_ref[hist:, 0:hp] = xs_ref[...].astype(F32)
    ext_ref[hist:, hp:hp + gn] = b_ref[...].astype(F32)
    ext_ref[hist:, hp + gn:] = c_ref[...].astype(F32)
    conv = cb_ref[...]
    for i in range(kc):
        conv = conv + cw_ref[i:i + 1, :] * ext_ref[pl.ds(hist - (kc - 1) + i, l), :]
    ext_ref[0:hist, :] = ext_ref[l:l + hist, :]
    xbc = conv * _sigmoid(conv)
    xs = xbc[:, :hp]

    dt_pre = dt_ref[...] + dtb_ref[...]
    dt = jnp.maximum(dt_pre, 0.0) + jnp.log(1.0 + jnp.exp(-jnp.abs(dt_pre)))
    a = -jnp.exp(alog_ref[...])

    row = lax.broadcasted_iota(jnp.int32, (l, l), 0)
    col = lax.broadcasted_iota(jnp.int32, (l, l), 1)
    causal = row >= col
    acs = jnp.dot(causal.astype(F32), dt * a, preferred_element_type=F32, precision=HIGHEST)
    acst = acs.T
    dtt = dt.T
    acs_last = acs[l - 1:l, :]
    w_end = jnp.exp(acs_last - acs) * dt
    exp_acs = jnp.exp(acs)
    exp_last = jnp.exp(acs_last)

    for g in range(M_GROUPS):
        b_g = xbc[:, hp + g * ns:hp + (g + 1) * ns]
        c_g = xbc[:, hp + gn + g * ns:hp + gn + (g + 1) * ns].astype(BF16)
        bt_g = b_g.T.astype(BF16)
        cb = jnp.dot(c_g, bt_g, preferred_element_type=F32)
        for hh in range(hpg):
            h = g * hpg + hh
            x_h = xs[:, h * p:(h + 1) * p]
            seg = acs[:, h:h + 1] - acst[h:h + 1, :]
            decay = jnp.where(causal, jnp.exp(seg), 0.0)
            m = cb * decay * dtt[h:h + 1, :]
            st = state_ref[h]
            y = jnp.dot(m.astype(BF16), x_h.astype(BF16), preferred_element_type=F32)
            y = y + jnp.dot(c_g, st.astype(BF16), preferred_element_type=F32) * exp_acs[:, h:h + 1]
            yacc_ref[:, h * p:(h + 1) * p] = y
            xw = (x_h * w_end[:, h:h + 1]).astype(BF16)
            state_ref[h] = st * exp_last[:, h:h + 1] + jnp.dot(bt_g, xw, preferred_element_type=F32)

    z = z_ref[...].astype(F32)
    y = (yacc_ref[...] + dskip_ref[...] * xs) * (z * _sigmoid(z))
    gw = hp // M_GROUPS
    for g in range(M_GROUPS):
        seg = y[:, g * gw:(g + 1) * gw]
        ms = jnp.mean(seg * seg, axis=-1, keepdims=True)
        y_ref[:, g * gw:(g + 1) * gw] = (seg * lax.rsqrt(ms + LN_EPS)
                                         * mnorm_ref[:, g * gw:(g + 1) * gw]).astype(y_ref.dtype)


def ssd_mixer(proj, lora, cols, conv_w, conv_b, dt_bias, a_log, d_skip, m_norm):
    bsz, t, _ = proj.shape
    hp = m_norm.shape[1]
    gn = M_GROUPS * M_STATE
    conv_dim = hp + 2 * gn
    nh = hp // M_HEAD_DIM
    l = CHUNK

    def col_block(width, off):
        blk = off // width
        return pl.BlockSpec((None, l, width), lambda b, c: (b, c, blk))

    fixed = lambda b, c: (0, 0)
    return pl.pallas_call(
        _ssd_kernel,
        grid=(bsz, t // l),
        in_specs=[col_block(hp, cols["xs"]), col_block(gn, cols["B"]), col_block(gn, cols["C"]),
                  col_block(hp, cols["z"]), col_block(LANES, cols["dt"]),
                  pl.BlockSpec(conv_w.shape, fixed), pl.BlockSpec((1, conv_dim), fixed),
                  pl.BlockSpec((1, LANES), fixed), pl.BlockSpec((1, LANES), fixed),
                  pl.BlockSpec((1, hp), fixed), pl.BlockSpec((1, hp), fixed)],
        out_specs=pl.BlockSpec((None, l, hp), lambda b, c: (b, c, 0)),
        out_shape=jax.ShapeDtypeStruct((bsz, t, hp), BF16),
        scratch_shapes=[pltpu.VMEM((nh, M_STATE, M_HEAD_DIM), F32),
                        pltpu.VMEM((CONV_HISTORY + l, conv_dim), F32),
                        pltpu.VMEM((l, hp), F32)],
        compiler_params=_params("parallel", "arbitrary"),
        name="ssd_mixer",
    )(proj, proj, proj, proj, lora, conv_w, conv_b, dt_bias, a_log, d_skip, m_norm)


EXP_MINUS_HALF = float(np.exp(-0.5))


def _head_sum(x, ones):
    hi = x.astype(BF16)
    lo = (x - hi.astype(F32)).astype(BF16)
    outs = []
    for j in range(x.shape[1] // MXU_DIM):
        sl = slice(j * MXU_DIM, (j + 1) * MXU_DIM)
        outs.append(jnp.dot(hi[:, sl], ones, preferred_element_type=F32)
                    + jnp.dot(lo[:, sl], ones, preferred_element_type=F32))
    return jnp.concatenate(outs, axis=-1)


def _rwkv_prep(r_ref, k_ref, v_ref, lo_ref, mu_ref, w0_ref, a0_ref, kk_ref, ka_ref, rk_ref,
               wup_ref, aup_ref, gup_ref, ones_ref, prev_ref):
    tl, c = r_ref.shape
    first_row = lax.broadcasted_iota(jnp.int32, (tl, 1), 0) == 0

    def shift_mix(x_ref, off):
        x = x_ref[...].astype(F32)
        width = x.shape[1]
        prev = jnp.where(first_row, prev_ref[0:1, off:off + width], pltpu.roll(x, 1, 0))
        prev_ref[0:1, off:off + width] = x[tl - 1:tl, :]
        return x + (prev - x) * mu_ref[:, off:off + width]

    r = shift_mix(r_ref, 0)
    k = shift_mix(k_ref, c)
    v = shift_mix(v_ref, 2 * c)
    lo = shift_mix(lo_ref, 3 * c)
    lo_a = lo[:, 0:LANES]
    lo_g = lo[:, LORA_GATE_OFF:LORA_GATE_OFF + LORA_GATE_PAD]
    w = w0_ref[...] + jnp.dot(jnp.tanh(lo_a).astype(BF16), wup_ref[...], preferred_element_type=F32)
    log_decay = -EXP_MINUS_HALF * _sigmoid(w)
    iclr = _sigmoid(a0_ref[...] + jnp.dot(lo_a.astype(BF16), aup_ref[...], preferred_element_type=F32))
    gate = jnp.dot(_sigmoid(lo_g).astype(BF16), gup_ref[...], preferred_element_type=F32)
    ones = ones_ref[...]
    kk = k * kk_ref[...]
    kk = kk / jnp.maximum(jnp.sqrt(_head_sum(kk * kk, ones)), 1e-12)
    k2 = k * (1.0 + (iclr - 1.0) * ka_ref[...])
    bonus = _head_sum(r * k2 * rk_ref[...], ones) * v
    return r, log_decay, k2, v, -kk, kk * iclr, bonus, gate


RWKV_CHUNK = 64


def _nt(a, b):
    return lax.dot_general(a, b, (((1,), (1,)), ((), ())), preferred_element_type=F32)


def _tn(a, b):
    return lax.dot_general(a, b, (((0,), (0,)), ((), ())), preferred_element_type=F32)


def _rwkv_chunk_kernel(r_ref, k_ref, v_ref, lo_ref, mu_ref, w0_ref, a0_ref, kk_ref, ka_ref, rk_ref,
                       wup_ref, aup_ref, gup_ref, ones_ref, rlnw_ref, rlnb_ref,
                       y_ref, state_ref, yacc_ref, prev_ref):
    l, c = r_ref.shape
    hd = R_HEAD_DIM
    n_pair = c // LANES

    @pl.when(pl.program_id(1) == 0)
    def _():
        state_ref[...] = jnp.zeros_like(state_ref)
        prev_ref[...] = jnp.zeros_like(prev_ref)

    r_all, lw, k_all, v_all, a_all, b_all, bonus, gate = _rwkv_prep(
        r_ref, k_ref, v_ref, lo_ref, mu_ref, w0_ref, a0_ref, kk_ref, ka_ref, rk_ref,
        wup_ref, aup_ref, gup_ref, ones_ref, prev_ref)
    row = lax.broadcasted_iota(jnp.int32, (l, l), 0)
    col = lax.broadcasted_iota(jnp.int32, (l, l), 1)
    g = jnp.dot((row >= col).astype(F32), lw, preferred_element_type=F32, precision=HIGHEST)
    g_last = g[l - 1:l, :]
    e_g = jnp.exp(g)
    e_ng = jnp.exp(-g)
    e_last = jnp.exp(g_last)
    at_all = a_all * jnp.exp(g - lw)
    rt_all = r_all * e_g
    bt_all = b_all * e_ng
    kt_all = k_all * e_ng

    wi = lax.broadcasted_iota(jnp.int32, (l, 2 * l), 0)
    wl = lax.broadcasted_iota(jnp.int32, (l, 2 * l), 1)
    wj = jnp.where(wl >= l, wl - l, wl)
    strict = wj < wi
    incl = wj <= wi
    eye_w = (wj == wi).astype(F32)
    first_w = wl < l
    first_c = lax.broadcasted_iota(jnp.int32, (l, LANES), 1) < hd
    bi = lax.broadcasted_iota(jnp.int32, (LANES, LANES), 0)
    bj = lax.broadcasted_iota(jnp.int32, (LANES, LANES), 1)
    same_head = (bi < hd) == (bj < hd)

    def block_rows(x, first):
        return jnp.concatenate([jnp.where(first, x, 0.0), jnp.where(first, 0.0, x)], axis=0)

    pairs = range(n_pair)
    lanes_of = [slice(p * LANES, (p + 1) * LANES) for p in pairs]
    dot = lambda x, y: jnp.dot(x, y, preferred_element_type=F32)
    ht = [state_ref[p] for p in pairs]
    lhs_ar = [jnp.concatenate([at_all[:, s], rt_all[:, s]], axis=0).astype(BF16) for s in lanes_of]
    gram = [_nt(lhs_ar[p], jnp.concatenate([block_rows(bt_all[:, s], first_c),
                                            block_rows(kt_all[:, s], first_c)], axis=0).astype(BF16))
            for p, s in zip(pairs, lanes_of)]
    h_ar = [_nt(lhs_ar[p], ht[p].astype(BF16)) for p in pairs]
    n_w = [jnp.where(strict, gram[p][0:l, 0:2 * l], 0.0) for p in pairs]
    t_w = [eye_w + n_w[p] for p in pairs]
    p_w = [dot(n_w[p].astype(BF16), block_rows(n_w[p], first_w).astype(BF16)) for p in pairs]
    for _ in range(int(np.log2(l)) - 1):
        res = [dot(jnp.concatenate([t_w[p], p_w[p]], axis=0).astype(BF16),
                   block_rows(p_w[p], first_w).astype(BF16)) for p in pairs]
        t_w = [t_w[p] + res[p][0:l] for p in pairs]
        p_w = [res[p][l:2 * l] for p in pairs]
    v_bd = [block_rows(v_all[:, s], first_c).astype(BF16) for s in lanes_of]
    rhs_u = [h_ar[p][0:l] + dot(jnp.where(strict, gram[p][0:l, 2 * l:4 * l], 0.0).astype(BF16), v_bd[p])
             for p in pairs]
    u = [dot(t_w[p].astype(BF16), block_rows(rhs_u[p], first_c).astype(BF16)) for p in pairs]
    for p, s in zip(pairs, lanes_of):
        rbk_w = jnp.where(jnp.concatenate([incl, incl], axis=1), gram[p][l:2 * l, :], 0.0)
        yacc_ref[:, s] = h_ar[p][l:2 * l] + dot(
            rbk_w.astype(BF16), jnp.concatenate([block_rows(u[p], first_c).astype(BF16), v_bd[p]], axis=0))
    for p, s in zip(pairs, lanes_of):
        e_l = e_last[:, s]
        upd = _tn(jnp.concatenate([u[p], v_all[:, s]], axis=0).astype(BF16),
                  jnp.concatenate([bt_all[:, s] * e_l, kt_all[:, s] * e_l], axis=0).astype(BF16))
        state_ref[p] = ht[p] * e_l + jnp.where(same_head, upd, 0.0)

    y = yacc_ref[...]
    ones = ones_ref[...]
    inv = 1.0 / hd
    mu = _head_sum(y, ones) * inv
    yc = y - mu
    var = _head_sum(yc * yc, ones) * inv
    yn = yc * lax.rsqrt(var + R_LN_EPS)
    y_ref[...] = (((yn * rlnw_ref[...] + rlnb_ref[...]) + bonus) * gate).astype(y_ref.dtype)


def rwkv_mixer(proj, lora, cols, mu, w0, a0, k_k, k_a, r_k, w_up, a_up, g_up, rln_w, rln_b):
    bsz, t, _ = proj.shape
    c = w0.shape[1]
    l = RWKV_CHUNK
    ones = jnp.asarray(np.kron(np.eye(MXU_DIM // R_HEAD_DIM), np.ones((R_HEAD_DIM, R_HEAD_DIM))), BF16)

    def col_block(width, off):
        blk = off // width
        return pl.BlockSpec((None, l, width), lambda i, j: (i, j, blk))

    fixed = lambda shape: pl.BlockSpec(shape, lambda i, j: (0, 0))
    return pl.pallas_call(
        _rwkv_chunk_kernel,
        grid=(bsz, t // l),
        in_specs=[col_block(c, cols["r"]), col_block(c, cols["k"]), col_block(c, cols["v"]),
                  col_block(LORA_BLOCK, cols["lora"]), fixed(mu.shape)]
        + [fixed((1, c))] * 5 + [fixed(w_up.shape), fixed(a_up.shape), fixed(g_up.shape), fixed(ones.shape),
                                 fixed((1, c)), fixed((1, c))],
        out_specs=pl.BlockSpec((None, l, c), lambda i, j: (i, j, 0)),
        out_shape=jax.ShapeDtypeStruct((bsz, t, c), BF16),
        scratch_shapes=[pltpu.VMEM((c // LANES, LANES, LANES), F32), pltpu.VMEM((l, c), F32),
                        pltpu.VMEM((SUBLANES, mu.shape[1]), F32)],
        compiler_params=_params("parallel", "arbitrary"),
        name="rwkv_mixer",
    )(proj, proj, proj, lora, mu, w0, a0, k_k, k_a, r_k, w_up, a_up, g_up, ones, rln_w, rln_b)


def _retention_kernel(q_ref, k_ref, v_ref, g_ref, pos_ref, freq_ref, y_ref, state_ref):
    l = q_ref.shape[0]
    dk = RET_QK_HEAD
    dv = RET_V_HEAD
    half = dk // 2

    @pl.when(pl.program_id(1) == 0)
    def _():
        state_ref[...] = jnp.zeros_like(state_ref)

    ang = pos_ref[...].astype(F32) * freq_ref[...]
    cos = jnp.cos(ang)
    sin = jnp.sin(ang)

    def rotate(u):
        ue = u[:, :half].astype(F32)
        uo = u[:, half:].astype(F32)
        return jnp.concatenate([ue * cos - uo * sin, uo * cos + ue * sin], axis=-1)

    row = lax.broadcasted_iota(jnp.int32, (l, l), 0)
    col = lax.broadcasted_iota(jnp.int32, (l, l), 1)
    rel = (row - col).astype(F32)
    causal = row >= col
    idx = lax.broadcasted_iota(jnp.int32, (l, 1), 0).astype(F32)
    for h in range(RET_HEADS):
        log_gamma = float(np.log(np.float32(1.0) - np.float32(2.0) ** np.float32(-5.0 - h)))
        intra = jnp.where(causal, jnp.exp(log_gamma * jnp.where(causal, rel, 0.0)), 0.0)
        q_decay = jnp.exp(log_gamma * (idx + 1.0))
        k_decay = jnp.exp(log_gamma * (l - 1.0 - idx))
        chunk_decay = float(np.exp(np.float32(log_gamma) * np.float32(l)))
        q_h = rotate(q_ref[:, h * dk:(h + 1) * dk]).astype(BF16)
        k_h = rotate(k_ref[:, h * dk:(h + 1) * dk]) * (dk ** -0.5)
        v_h = v_ref[:, h * dv:(h + 1) * dv].astype(BF16)
        s = lax.dot_general(q_h, k_h.astype(BF16), (((1,), (1,)), ((), ())),
                            preferred_element_type=F32) * intra
        st = state_ref[h]
        y = jnp.dot(s.astype(BF16), v_h, preferred_element_type=F32)
        y = y + jnp.dot(q_h, st.astype(BF16), preferred_element_type=F32) * q_decay
        kd = (k_h * k_decay).astype(BF16)
        state_ref[h] = st * chunk_decay + lax.dot_general(
            kd, v_h, (((0,), (0,)), ((), ())), preferred_element_type=F32)
        mu = jnp.mean(y, axis=-1, keepdims=True)
        yc = y - mu
        var = jnp.mean(yc * yc, axis=-1, keepdims=True)
        g_h = g_ref[:, h * dv:(h + 1) * dv].astype(F32)
        y_ref[:, h * dv:(h + 1) * dv] = (g_h * _sigmoid(g_h) * (yc * lax.rsqrt(var + LN_EPS))).astype(y_ref.dtype)


def retention_mixer(proj, positions, inv_freq):
    bsz, t, _ = proj.shape
    hk = RET_HEADS * RET_QK_HEAD
    hv = RET_HEADS * RET_V_HEAD
    l = CHUNK
    return pl.pallas_call(
        _retention_kernel,
        grid=(bsz, t // l),
        in_specs=[pl.BlockSpec((None, l, hk), lambda b, c: (b, c, 0)),
                  pl.BlockSpec((None, l, hk), lambda b, c: (b, c, 1)),
                  pl.BlockSpec((None, l, hv), lambda b, c: (b, c, 1)),
                  pl.BlockSpec((None, l, hv), lambda b, c: (b, c, 2)),
                  pl.BlockSpec((None, l, 1), lambda b, c: (b, c, 0)),
                  pl.BlockSpec((1, RET_QK_HEAD // 2), lambda b, c: (0, 0))],
        out_specs=pl.BlockSpec((None, l, hv), lambda b, c: (b, c, 0)),
        out_shape=jax.ShapeDtypeStruct((bsz, t, hv), BF16),
        scratch_shapes=[pltpu.VMEM((RET_HEADS, RET_QK_HEAD, RET_V_HEAD), F32)],
        compiler_params=_params("parallel", "arbitrary"),
        name="retention_mixer",
    )(proj, proj, proj, proj, positions, inv_freq)


def _even_layer(x, w_in, conv_w, conv_b, dt_bias, a_log, d_skip, m_norm, mu_shift, w0, w_up, a0, a_up,
                g_up, k_k, k_a, r_k, rln_w, rln_b, w_out, ln1_g, ln1_b, wg, wu, wd, ln2_g, ln2_b):
    bsz, t, d = x.shape
    n = bsz * t
    m_inner = m_norm.shape[0]
    m_heads = dt_bias.shape[0]
    c = w0.shape[0]
    r_heads = c // R_HEAD_DIM
    gn = M_GROUPS * M_STATE
    o_xbc = m_inner
    o_dt = o_xbc + m_inner + 2 * gn
    o_rw = o_dt + m_heads
    o_lora = o_rw + 3 * c
    n_lora = R_DECAY_LORA + R_AAA_LORA + R_GATE_LORA
    x2 = x.reshape(n, d)

    zeros = lambda k: jnp.zeros((d, k), w_in.dtype)
    w_main = jnp.concatenate([
        w_in[:, o_rw:o_lora],
        w_in[:, :m_inner],
        w_in[:, o_xbc:o_dt]], axis=1).astype(BF16)
    w_lora = jnp.concatenate([
        w_in[:, o_lora:o_lora + n_lora], zeros(LORA_DT_OFF - n_lora),
        w_in[:, o_dt:o_rw], zeros(LORA_BLOCK - LORA_DT_OFF - m_heads)], axis=1).astype(BF16)
    cols = {"r": 0, "k": c, "v": 2 * c, "z": 3 * c, "xs": 3 * c + m_inner, "B": 3 * c + 2 * m_inner,
            "C": 3 * c + 2 * m_inner + gn, "lora": 0, "dt": LORA_DT_OFF}
    proj = matmul(x2, w_main, tn=LORA_BLOCK, out_dtype=BF16).reshape(bsz, t, -1)
    lora = matmul(x2, w_lora, tn=LORA_BLOCK).reshape(bsz, t, -1)

    pad_lane = lambda u: jnp.pad(u.reshape(1, -1), ((0, 0), (0, LANES - u.shape[0])))
    y_ssd = ssd_mixer(proj, lora, cols, conv_w, conv_b.reshape(1, -1), pad_lane(dt_bias), pad_lane(a_log),
                      jnp.repeat(d_skip, M_HEAD_DIM).reshape(1, -1), m_norm.reshape(1, -1))

    mu_p = jnp.pad(mu_shift, (0, LORA_BLOCK - n_lora)).reshape(1, -1)
    w_up_p = jnp.pad(w_up, ((0, LANES - R_DECAY_LORA), (0, 0))).astype(BF16)
    a_up_p = jnp.pad(a_up, ((R_DECAY_LORA, LANES - R_DECAY_LORA - R_AAA_LORA), (0, 0))).astype(BF16)
    g_up_p = jnp.pad(g_up, ((0, LORA_GATE_PAD - R_GATE_LORA), (0, 0))).astype(BF16)
    row = lambda u: u.reshape(1, -1)
    y_rwkv = rwkv_mixer(proj, lora, cols, mu_p, row(w0), row(a0), row(k_k), row(k_a), row(r_k),
                        w_up_p, a_up_p, g_up_p, row(rln_w), row(rln_b))
    x2 = out_proj_deepnorm(y_ssd.reshape(n, m_inner), y_rwkv.reshape(n, c), w_out.astype(BF16), x2,
                           row(ln1_g), row(ln1_b))
    x2 = ffn_deepnorm(x2, wg.astype(BF16), wu.astype(BF16), wd.astype(BF16), row(ln2_g), row(ln2_b))
    return x2.reshape(bsz, t, d)


def _odd_layer(x, positions, w_in, w_out, ln1_g, ln1_b, router, wg, wu, wd, ln2_g, ln2_b):
    bsz, t, d = x.shape
    n = bsz * t
    ret_qk = RET_HEADS * RET_QK_HEAD
    x2 = x.reshape(n, d)
    head_perm = np.concatenate([np.arange(0, RET_QK_HEAD, 2), np.arange(1, RET_QK_HEAD, 2)])
    qk_perm = (np.arange(2 * RET_HEADS)[:, None] * RET_QK_HEAD + head_perm[None, :]).reshape(-1)
    w_in_p = jnp.concatenate([w_in[:, qk_perm], w_in[:, 2 * ret_qk:]], axis=1).astype(BF16)
    proj = matmul(x2, w_in_p, out_dtype=BF16).reshape(bsz, t, -1)
    inv_freq = (1.0 / (10000.0 ** jnp.linspace(0.0, 1.0, RET_QK_HEAD // 2, dtype=F32))).reshape(1, -1)
    a = retention_mixer(proj, positions.reshape(bsz, t, 1), inv_freq)
    row = lambda u: u.reshape(1, -1)
    x2 = out_proj_deepnorm(a.reshape(n, -1), None, w_out.astype(BF16), x2, row(ln1_g), row(ln1_b))
    x2 = moe_top2_deepnorm(x2, jnp.pad(router, ((0, 0), (0, LANES - router.shape[1]))),
                           wg.astype(BF16), wu.astype(BF16), wd.astype(BF16), row(ln2_g), row(ln2_b))
    return x2.reshape(bsz, t, d)


def kernel(x, positions, ev_w_in, ev_conv_w, ev_conv_b, ev_dt_bias, ev_a_log, ev_d_skip, ev_m_norm, ev_mu_shift, ev_w0, ev_w_up, ev_a0, ev_a_up, ev_g_up, ev_k_k, ev_k_a, ev_r_k, ev_rln_w, ev_rln_b, ev_w_out, ev_ln1_g, ev_ln1_b, ev_ffn_wg, ev_ffn_wu, ev_ffn_wd, ev_ln2_g, ev_ln2_b, od_w_in, od_w_out, od_ln1_g, od_ln1_b, od_router, od_moe_wg, od_moe_wu, od_moe_wd, od_ln2_g, od_ln2_b):
    for layer in range(DEPTH):
        i = layer // 2
        if layer % 2 == 0:
            x = _even_layer(x, ev_w_in[i], ev_conv_w[i], ev_conv_b[i], ev_dt_bias[i], ev_a_log[i], ev_d_skip[i],
                            ev_m_norm[i], ev_mu_shift[i], ev_w0[i], ev_w_up[i], ev_a0[i], ev_a_up[i], ev_g_up[i],
                            ev_k_k[i], ev_k_a[i], ev_r_k[i], ev_rln_w[i], ev_rln_b[i], ev_w_out[i],
                            ev_ln1_g[i], ev_ln1_b[i], ev_ffn_wg[i], ev_ffn_wu[i], ev_ffn_wd[i],
                            ev_ln2_g[i], ev_ln2_b[i])
        else:
            x = _odd_layer(x, positions, od_w_in[i], od_w_out[i], od_ln1_g[i], od_ln1_b[i], od_router[i],
                           od_moe_wg[i], od_moe_wu[i], od_moe_wd[i], od_ln2_g[i], od_ln2_b[i])
    return x
```

```python
import jax
import jax.numpy as jnp
import numpy as np
from jax import lax
from jax.experimental import pallas as pl
from jax.experimental.pallas import tpu as pltpu

F32 = jnp.float32
BF16 = jnp.bfloat16
HIGHEST = lax.Precision.HIGHEST

CHUNK = 128
M_HEAD_DIM = 64
M_GROUPS = 2
M_STATE = 128
R_HEAD_DIM = 64
R_DECAY_LORA = 64
R_AAA_LORA = 64
R_GATE_LORA = 160
R_LN_EPS = 64e-5
RET_HEADS = 4
RET_QK_HEAD = 256
RET_V_HEAD = 512
N_EXPERTS = 8
LN_EPS = 1e-5
DEPTH = 2
ALPHA = (2.0 * DEPTH) ** 0.25

LANES = 128
SUBLANES = 8
MXU_DIM = 256
VMEM_LIMIT_BYTES = 56 * 1024 * 1024

LORA_BLOCK = 512
LORA_GATE_OFF = 128
LORA_GATE_PAD = 256
LORA_DT_OFF = 384


def _params(*sem):
    return pltpu.CompilerParams(dimension_semantics=sem, vmem_limit_bytes=VMEM_LIMIT_BYTES)


def _sigmoid(x):
    return jax.nn.sigmoid(x)


def _matmul_kernel(x_ref, w_ref, o_ref, xb_ref):
    @pl.when(pl.program_id(1) == 0)
    def _():
        xb_ref[...] = x_ref[...].astype(BF16)

    o_ref[...] = jnp.dot(xb_ref[...], w_ref[...], preferred_element_type=F32).astype(o_ref.dtype)


def matmul(x, w, *, tm=1024, tn=1024, out_dtype=F32):
    n, k = x.shape
    m = w.shape[1]
    tm = min(tm, n)
    return pl.pallas_call(
        _matmul_kernel,
        grid=(n // tm, m // tn),
        in_specs=[pl.BlockSpec((tm, k), lambda i, j: (i, 0)),
                  pl.BlockSpec((k, tn), lambda i, j: (0, j))],
        out_specs=pl.BlockSpec((tm, tn), lambda i, j: (i, j)),
        out_shape=jax.ShapeDtypeStruct((n, m), out_dtype),
        scratch_shapes=[pltpu.VMEM((tm, k), BF16)],
        compiler_params=_params("parallel", "arbitrary"),
        name="matmul",
    )(x, w)


def _deepnorm_rows(resid, sub, g, b):
    y = ALPHA * resid + sub
    mu = jnp.mean(y, axis=-1, keepdims=True)
    yc = y - mu
    var = jnp.mean(yc * yc, axis=-1, keepdims=True)
    return yc * lax.rsqrt(var + LN_EPS) * g + b


def _out_proj_kernel(a1_ref, a2_ref, w_ref, x_ref, g_ref, b_ref, o_ref):
    kh = a1_ref.shape[1]
    sub = jnp.dot(a1_ref[...], w_ref[:kh, :], preferred_element_type=F32)
    sub = sub + jnp.dot(a2_ref[...], w_ref[kh:, :], preferred_element_type=F32)
    o_ref[...] = _deepnorm_rows(x_ref[...], sub, g_ref[...], b_ref[...])


def out_proj_deepnorm(a1, a2, w, x, g, b, *, tm=512):
    n, d = x.shape
    kh = w.shape[0] // 2
    tm = min(tm, n)
    row = lambda i: (i, 0)
    fixed = lambda i: (0, 0)
    if a2 is None:
        a2, second = a1, pl.BlockSpec((tm, kh), lambda i: (i, 1))
    else:
        second = pl.BlockSpec((tm, kh), row)
    return pl.pallas_call(
        _out_proj_kernel,
        grid=(n // tm,),
        in_specs=[pl.BlockSpec((tm, kh), row), second,
                  pl.BlockSpec((2 * kh, d), fixed), pl.BlockSpec((tm, d), row),
                  pl.BlockSpec((1, d), fixed), pl.BlockSpec((1, d), fixed)],
        out_specs=pl.BlockSpec((tm, d), row),
        out_shape=jax.ShapeDtypeStruct((n, d), F32),
        compiler_params=_params("parallel"),
        name="out_proj_deepnorm",
    )(a1, a2, w, x, g, b)


TOP_K = 2
R_IDX = 0
R_PROB = TOP_K


def _router_kernel(x_ref, wr_ref, info_ref):
    logits = jnp.dot(x_ref[...], wr_ref[...], preferred_element_type=F32, precision=HIGHEST)
    lane = lax.broadcasted_iota(jnp.int32, logits.shape, 1)
    neg = jnp.float32(-jnp.inf)
    lg = jnp.where(lane < N_EXPERTS, logits, neg)
    m1 = jnp.max(lg, axis=-1, keepdims=True)
    i1 = jnp.min(jnp.where(lg == m1, lane, LANES), axis=-1, keepdims=True)
    lg2 = jnp.where(lane == i1, neg, lg)
    m2 = jnp.max(lg2, axis=-1, keepdims=True)
    i2 = jnp.min(jnp.where(lg2 == m2, lane, LANES), axis=-1, keepdims=True)
    e2 = jnp.exp(m2 - m1)
    p1 = 1.0 / (1.0 + e2)
    p2 = e2 / (1.0 + e2)
    info = jnp.where(lane == R_IDX, i1.astype(F32), 0.0)
    info = jnp.where(lane == R_IDX + 1, i2.astype(F32), info)
    info = jnp.where(lane == R_PROB, p1, info)
    info = jnp.where(lane == R_PROB + 1, p2, info)
    info_ref[...] = info


def router_top2(x, w_router, *, tm=1024):
    n, d = x.shape
    tm = min(tm, n)
    return pl.pallas_call(
        _router_kernel,
        grid=(n // tm,),
        in_specs=[pl.BlockSpec((tm, d), lambda i: (i, 0)), pl.BlockSpec((d, LANES), lambda i: (0, 0))],
        out_specs=pl.BlockSpec((tm, LANES), lambda i: (i, 0)),
        out_shape=jax.ShapeDtypeStruct((n, LANES), F32),
        compiler_params=_params("parallel"),
        name="router_top2",
    )(x, w_router)


def _combine_kernel(x_ref, y0_ref, y1_ref, info_ref, g_ref, b_ref, out_ref):
    sub = y0_ref[...] * info_ref[:, R_PROB:R_PROB + 1] + y1_ref[...] * info_ref[:, R_PROB + 1:R_PROB + 2]
    out_ref[...] = _deepnorm_rows(x_ref[...], sub, g_ref[...], b_ref[...])


def combine_deepnorm(x, y_slots, info, g, b, *, tc=512):
    n, d = x.shape
    tc = min(tc, n)
    nt = n // tc
    row = lambda i: (i, 0)
    fixed = lambda i: (0, 0)
    return pl.pallas_call(
        _combine_kernel,
        grid=(nt,),
        in_specs=[pl.BlockSpec((tc, d), row), pl.BlockSpec((tc, d), row),
                  pl.BlockSpec((tc, d), lambda i: (i + nt, 0)), pl.BlockSpec((tc, LANES), row),
                  pl.BlockSpec((1, d), fixed), pl.BlockSpec((1, d), fixed)],
        out_specs=pl.BlockSpec((tc, d), row),
        out_shape=jax.ShapeDtypeStruct((n, d), F32),
        compiler_params=_params("parallel"),
        name="combine_deepnorm",
    )(x, y_slots, y_slots, info, g, b)


DMA_ISSUE_UNROLL = 8


def _start_row_copies(src_ref, src_row, dst_ref, dst_row, sem, n_rows):
    def issue(r, carry):
        pltpu.make_async_copy(src_ref.at[pl.ds(src_row(r), 1)], dst_ref.at[pl.ds(dst_row(r), 1)], sem).start()
        return carry

    lax.fori_loop(0, n_rows, issue, 0, unroll=DMA_ISSUE_UNROLL)


def _expert_kernel(te_ref, src_ref, src_next_ref, dst_prev_ref, x_hbm, wg_ref, wu_ref, wd_ref, y_hbm,
                   xbuf_ref, ybuf_ref, acc_ref, xb_ref, xsem, ysem):
    i = pl.program_id(0)
    f = pl.program_id(1)
    n_i = pl.num_programs(0)
    n_f = pl.num_programs(1)
    tm = acc_ref.shape[0]
    slot = i % 2

    def wait_tile(src, dst, sem):
        pltpu.make_async_copy(src, dst, sem).wait()

    def swiglu_step(first):
        xb = xb_ref[...]
        hg = jnp.dot(xb, wg_ref[...], preferred_element_type=F32)
        hu = jnp.dot(xb, wu_ref[...], preferred_element_type=F32)
        h = (hg * _sigmoid(hg)) * hu
        out = jnp.dot(h.astype(BF16), wd_ref[...], preferred_element_type=F32)
        acc_ref[...] = out if first else acc_ref[...] + out

    @pl.when(f == 0)
    def _():
        @pl.when(i == 0)
        def _():
            _start_row_copies(x_hbm, lambda r: src_ref[0, r], xbuf_ref.at[0], lambda r: r, xsem.at[0], tm)
            ybuf_ref[...] = jnp.zeros_like(ybuf_ref)

        wait_tile(x_hbm.at[pl.ds(0, tm)], xbuf_ref.at[slot], xsem.at[slot])
        xb_ref[...] = xbuf_ref[slot].astype(BF16)
        swiglu_step(True)
        for r in range(tm):
            pltpu.make_async_copy(ybuf_ref.at[pl.ds(r, 1)], y_hbm.at[pl.ds(dst_prev_ref[0, r], 1)], ysem).start()
            pltpu.make_async_copy(x_hbm.at[pl.ds(src_next_ref[0, r], 1)], xbuf_ref.at[1 - slot, pl.ds(r, 1)],
                                  xsem.at[1 - slot]).start()

    @pl.when(f > 0)
    def _():
        swiglu_step(False)

    @pl.when(f == n_f - 1)
    def _():
        wait_tile(ybuf_ref, y_hbm.at[pl.ds(0, tm)], ysem)
        ybuf_ref[...] = acc_ref[...]

        @pl.when(i == n_i - 1)
        def _():
            wait_tile(x_hbm.at[pl.ds(0, tm)], xbuf_ref.at[1 - slot], xsem.at[1 - slot])


def expert_swiglu(x, src_rows, dst_rows, tile_expert, wg, wu, wd, *, tm, tf=1792):
    d = x.shape[1]
    r = src_rows.shape[0]
    ff = wg.shape[2]
    n_tiles = r // tm
    last = n_tiles - 1
    idx_tile = lambda u: u.reshape(n_tiles, 1, tm)
    smem_tile = lambda index: pl.BlockSpec((None, 1, tm), index, memory_space=pltpu.SMEM)
    tile = lambda i: jnp.minimum(i, last)
    grid_spec = pltpu.PrefetchScalarGridSpec(
        num_scalar_prefetch=1,
        grid=(n_tiles + 1, ff // tf),
        in_specs=[smem_tile(lambda i, f, te: (tile(i), 0, 0)),
                  smem_tile(lambda i, f, te: (tile(i + 1), 0, 0)),
                  smem_tile(lambda i, f, te: (jnp.maximum(i - 1, 0), 0, 0)),
                  pl.BlockSpec(memory_space=pl.ANY),
                  pl.BlockSpec((None, d, tf), lambda i, f, te: (te[tile(i)], 0, f)),
                  pl.BlockSpec((None, d, tf), lambda i, f, te: (te[tile(i)], 0, f)),
                  pl.BlockSpec((None, tf, d), lambda i, f, te: (te[tile(i)], f, 0))],
        out_specs=pl.BlockSpec(memory_space=pl.ANY),
        scratch_shapes=[pltpu.VMEM((2, tm, d), F32), pltpu.VMEM((tm, d), F32), pltpu.VMEM((tm, d), F32),
                        pltpu.VMEM((tm, d), BF16), pltpu.SemaphoreType.DMA((2,)), pltpu.SemaphoreType.DMA(())],
    )
    return pl.pallas_call(
        _expert_kernel,
        grid_spec=grid_spec,
        out_shape=jax.ShapeDtypeStruct((r, d), F32),
        compiler_params=_params("arbitrary", "arbitrary"),
        name="expert_swiglu",
    )(tile_expert, idx_tile(src_rows), idx_tile(src_rows), idx_tile(dst_rows), x, wg, wu, wd)


def moe_top2_deepnorm(x, w_router, wg, wu, wd, g, b, *, tm=512):
    n = x.shape[0]
    ne = wg.shape[0]
    info = router_top2(x, w_router)
    e_flat = info[:, R_IDX:R_IDX + TOP_K].astype(jnp.int32).reshape(-1)

    onehot = (e_flat[:, None] == jnp.arange(ne, dtype=jnp.int32)[None, :]).astype(jnp.int32)
    csum = jnp.cumsum(onehot, axis=0)
    rank = jnp.sum((csum - 1) * onehot, axis=1)
    padded = ((csum[-1] + tm - 1) // tm) * tm
    ends = jnp.cumsum(padded)
    dest = (ends - padded)[e_flat] + rank
    n_rows = n * TOP_K + ne * tm
    n_tiles = n_rows // tm
    assign = jnp.full((n_rows,), -1, jnp.int32).at[dest].set(jnp.arange(n * TOP_K, dtype=jnp.int32))
    token, slot = assign // TOP_K, assign % TOP_K
    src_rows = jnp.where(assign >= 0, token, 0)
    pad_rank = jnp.cumsum((assign < 0).astype(jnp.int32)) - 1
    dst_rows = jnp.where(assign >= 0, slot * n + token, n * TOP_K + pad_rank)
    tile_start = jnp.arange(n_tiles, dtype=jnp.int32) * tm
    tile_expert = jnp.minimum(jnp.sum((tile_start[:, None] >= ends[None, :]).astype(jnp.int32), axis=1), ne - 1)

    y_slots = expert_swiglu(x, src_rows, dst_rows, tile_expert, wg, wu, wd, tm=tm)
    return combine_deepnorm(x, y_slots, info, g, b)


def _ffn_kernel(x_ref, wg_ref, wu_ref, wd_ref, g_ref, b_ref, o_ref, acc_ref, xb_ref):
    f = pl.program_id(1)

    @pl.when(f == 0)
    def _():
        acc_ref[...] = jnp.zeros_like(acc_ref)
        xb_ref[...] = x_ref[...].astype(BF16)

    xb = xb_ref[...]
    hg = jnp.dot(xb, wg_ref[...], preferred_element_type=F32)
    hu = jnp.dot(xb, wu_ref[...], preferred_element_type=F32)
    h = (hg * _sigmoid(hg)) * hu
    acc_ref[...] += jnp.dot(h.astype(BF16), wd_ref[...], preferred_element_type=F32)

    @pl.when(f == pl.num_programs(1) - 1)
    def _():
        o_ref[...] = _deepnorm_rows(x_ref[...], acc_ref[...], g_ref[...], b_ref[...])


def ffn_deepnorm(x, wg, wu, wd, g, b, *, tm=512, tf=1792):
    n, d = x.shape
    ff = wg.shape[1]
    tm = min(tm, n)
    row = lambda i, f: (i, 0)
    fixed = lambda i, f: (0, 0)
    return pl.pallas_call(
        _ffn_kernel,
        grid=(n // tm, ff // tf),
        in_specs=[pl.BlockSpec((tm, d), row),
                  pl.BlockSpec((d, tf), lambda i, f: (0, f)),
                  pl.BlockSpec((d, tf), lambda i, f: (0, f)),
                  pl.BlockSpec((tf, d), lambda i, f: (f, 0)),
                  pl.BlockSpec((1, d), fixed), pl.BlockSpec((1, d), fixed)],
        out_specs=pl.BlockSpec((tm, d), row),
        out_shape=jax.ShapeDtypeStruct((n, d), F32),
        scratch_shapes=[pltpu.VMEM((tm, d), F32), pltpu.VMEM((tm, d), BF16)],
        compiler_params=_params("parallel", "arbitrary"),
        name="ffn_deepnorm",
    )(x, wg, wu, wd, g, b)


CONV_HISTORY = SUBLANES


def _ssd_kernel(xs_ref, b_ref, c_ref, z_ref, dt_ref, cw_ref, cb_ref, dtb_ref, alog_ref, dskip_ref, mnorm_ref,
                y_ref, state_ref, ext_ref, yacc_ref):
    l, hp = xs_ref.shape
    gn = b_ref.shape[1]
    p = M_HEAD_DIM
    ns = M_STATE
    nh = hp // p
    hpg = nh // M_GROUPS
    kc = cw_ref.shape[0]
    hist = CONV_HISTORY

    @pl.when(pl.program_id(1) == 0)
    def _():
        state_ref[...] = jnp.zeros_like(state_ref)
        ext_ref[0:hist, :] = jnp.zeros((hist, ext_ref.shape[1]), F32)

    ext_ref[hist:, 0:hp] = xs_ref[...].astype(F32)
    ext_ref[hist:, hp:hp + gn] = b_ref[...].astype(F32)
    ext_ref[hist:, hp + gn:] = c_ref[...].astype(F32)
    conv = cb_ref[...]
    for i in range(kc):
        conv = conv + cw_ref[i:i + 1, :] * ext_ref[pl.ds(hist - (kc - 1) + i, l), :]
    ext_ref[0:hist, :] = ext_ref[l:l + hist, :]
    xbc = conv * _sigmoid(conv)
    xs = xbc[:, :hp]

    dt_pre = dt_ref[...] + dtb_ref[...]
    dt = jnp.maximum(dt_pre, 0.0) + jnp.log(1.0 + jnp.exp(-jnp.abs(dt_pre)))
    a = -jnp.exp(alog_ref[...])

    row = lax.broadcasted_iota(jnp.int32, (l, l), 0)
    col = lax.broadcasted_iota(jnp.int32, (l, l), 1)
    causal = row >= col
    acs = jnp.dot(causal.astype(F32), dt * a, preferred_element_type=F32, precision=HIGHEST)
    acst = acs.T
    dtt = dt.T
    acs_last = acs[l - 1:l, :]
    w_end = jnp.exp(acs_last - acs) * dt
    exp_acs = jnp.exp(acs)
    exp_last = jnp.exp(acs_last)

    for g in range(M_GROUPS):
        b_g = xbc[:, hp + g * ns:hp + (g + 1) * ns]
        c_g = xbc[:, hp + gn + g * ns:hp + gn + (g + 1) * ns].astype(BF16)
        bt_g = b_g.T.astype(BF16)
        cb = jnp.dot(c_g, bt_g, preferred_element_type=F32)
        for hh in range(hpg):
            h = g * hpg + hh
            x_h = xs[:, h * p:(h + 1) * p]
            seg = acs[:, h:h + 1] - acst[h:h + 1, :]
            decay = jnp.where(causal, jnp.exp(seg), 0.0)
            m = cb * decay * dtt[h:h + 1, :]
            st = state_ref[h]
            y = jnp.dot(m.astype(BF16), x_h.astype(BF16), preferred_element_type=F32)
            y = y + jnp.dot(c_g, st.astype(BF16), preferred_element_type=F32) * exp_acs[:, h:h + 1]
            yacc_ref[:, h * p:(h + 1) * p] = y
            xw = (x_h * w_end[:, h:h + 1]).astype(BF16)
            state_ref[h] = st * exp_last[:, h:h + 1] + jnp.dot(bt_g, xw, preferred_element_type=F32)

    z = z_ref[...].astype(F32)
    y = (yacc_ref[...] + dskip_ref[...] * xs) * (z * _sigmoid(z))
    gw = hp // M_GROUPS
    for g in range(M_GROUPS):
        seg = y[:, g * gw:(g + 1) * gw]
        ms = jnp.mean(seg * seg, axis=-1, keepdims=True)
        y_ref[:, g * gw:(g + 1) * gw] = (seg * lax.rsqrt(ms + LN_EPS)
                                         * mnorm_ref[:, g * gw:(g + 1) * gw]).astype(y_ref.dtype)


def ssd_mixer(proj, lora, cols, conv_w, conv_b, dt_bias, a_log, d_skip, m_norm):
    bsz, t, _ = proj.shape
    hp = m_norm.shape[1]
    gn = M_GROUPS * M_STATE
    conv_dim = hp + 2 * gn
    nh = hp // M_HEAD_DIM
    l = CHUNK

    def col_block(width, off):
        blk = off // width
        return pl.BlockSpec((None, l, width), lambda b, c: (b, c, blk))

    fixed = lambda b, c: (0, 0)
    return pl.pallas_call(
        _ssd_kernel,
        grid=(bsz, t // l),
        in_specs=[col_block(hp, cols["xs"]), col_block(gn, cols["B"]), col_block(gn, cols["C"]),
                  col_block(hp, cols["z"]), col_block(LANES, cols["dt"]),
                  pl.BlockSpec(conv_w.shape, fixed), pl.BlockSpec((1, conv_dim), fixed),
                  pl.BlockSpec((1, LANES), fixed), pl.BlockSpec((1, LANES), fixed),
                  pl.BlockSpec((1, hp), fixed), pl.BlockSpec((1, hp), fixed)],
        out_specs=pl.BlockSpec((None, l, hp), lambda b, c: (b, c, 0)),
        out_shape=jax.ShapeDtypeStruct((bsz, t, hp), BF16),
        scratch_shapes=[pltpu.VMEM((nh, M_STATE, M_HEAD_DIM), F32),
                        pltpu.VMEM((CONV_HISTORY + l, conv_dim), F32),
                        pltpu.VMEM((l, hp), F32)],
        compiler_params=_params("parallel", "arbitrary"),
        name="ssd_mixer",
    )(proj, proj, proj, proj, lora, conv_w, conv_b, dt_bias, a_log, d_skip, m_norm)


EXP_MINUS_HALF = float(np.exp(-0.5))


def _head_sum(x, ones):
    hi = x.astype(BF16)
    lo = (x - hi.astype(F32)).astype(BF16)
    outs = []
    for j in range(x.shape[1] // MXU_DIM):
        sl = slice(j * MXU_DIM, (j + 1) * MXU_DIM)
        outs.append(jnp.dot(hi[:, sl], ones, preferred_element_type=F32)
                    + jnp.dot(lo[:, sl], ones, preferred_element_type=F32))
    return jnp.concatenate(outs, axis=-1)


def _rwkv_prep(r_ref, k_ref, v_ref, lo_ref, mu_ref, w0_ref, a0_ref, kk_ref, ka_ref, rk_ref,
               wup_ref, aup_ref, gup_ref, ones_ref, prev_ref):
    tl, c = r_ref.shape
    first_row = lax.broadcasted_iota(jnp.int32, (tl, 1), 0) == 0

    def shift_mix(x_ref, off):
        x = x_ref[...].astype(F32)
        width = x.shape[1]
        prev = jnp.where(first_row, prev_ref[0:1, off:off + width], pltpu.roll(x, 1, 0))
        prev_ref[0:1, off:off + width] = x[tl - 1:tl, :]
        return x + (prev - x) * mu_ref[:, off:off + width]

    r = shift_mix(r_ref, 0)
    k = shift_mix(k_ref, c)
    v = shift_mix(v_ref, 2 * c)
    lo = shift_mix(lo_ref, 3 * c)
    lo_a = lo[:, 0:LANES]
    lo_g = lo[:, LORA_GATE_OFF:LORA_GATE_OFF + LORA_GATE_PAD]
    w = w0_ref[...] + jnp.dot(jnp.tanh(lo_a).astype(BF16), wup_ref[...], preferred_element_type=F32)
    log_decay = -EXP_MINUS_HALF * _sigmoid(w)
    iclr = _sigmoid(a0_ref[...] + jnp.dot(lo_a.astype(BF16), aup_ref[...], preferred_element_type=F32))
    gate = jnp.dot(_sigmoid(lo_g).astype(BF16), gup_ref[...], preferred_element_type=F32)
    ones = ones_ref[...]
    kk = k * kk_ref[...]
    kk = kk / jnp.maximum(jnp.sqrt(_head_sum(kk * kk, ones)), 1e-12)
    k2 = k * (1.0 + (iclr - 1.0) * ka_ref[...])
    bonus = _head_sum(r * k2 * rk_ref[...], ones) * v
    return r, log_decay, k2, v, -kk, kk * iclr, bonus, gate


RWKV_CHUNK = 64


def _nt(a, b):
    return lax.dot_general(a, b, (((1,), (1,)), ((), ())), preferred_element_type=F32)


def _tn(a, b):
    return lax.dot_general(a, b, (((0,), (0,)), ((), ())), preferred_element_type=F32)


def _rwkv_chunk_kernel(r_ref, k_ref, v_ref, lo_ref, mu_ref, w0_ref, a0_ref, kk_ref, ka_ref, rk_ref,
                       wup_ref, aup_ref, gup_ref, ones_ref, rlnw_ref, rlnb_ref,
                       y_ref, state_ref, yacc_ref, prev_ref):
    l, c = r_ref.shape
    hd = R_HEAD_DIM
    n_pair = c // LANES

    @pl.when(pl.program_id(1) == 0)
    def _():
        state_ref[...] = jnp.zeros_like(state_ref)
        prev_ref[...] = jnp.zeros_like(prev_ref)

    r_all, lw, k_all, v_all, a_all, b_all, bonus, gate = _rwkv_prep(
        r_ref, k_ref, v_ref, lo_ref, mu_ref, w0_ref, a0_ref, kk_ref, ka_ref, rk_ref,
        wup_ref, aup_ref, gup_ref, ones_ref, prev_ref)
    row = lax.broadcasted_iota(jnp.int32, (l, l), 0)
    col = lax.broadcasted_iota(jnp.int32, (l, l), 1)
    g = jnp.dot((row >= col).astype(F32), lw, preferred_element_type=F32, precision=HIGHEST)
    g_last = g[l - 1:l, :]
    e_g = jnp.exp(g)
    e_ng = jnp.exp(-g)
    e_last = jnp.exp(g_last)
    at_all = a_all * jnp.exp(g - lw)
    rt_all = r_all * e_g
    bt_all = b_all * e_ng
    kt_all = k_all * e_ng

    wi = lax.broadcasted_iota(jnp.int32, (l, 2 * l), 0)
    wl = lax.broadcasted_iota(jnp.int32, (l, 2 * l), 1)
    wj = jnp.where(wl >= l, wl - l, wl)
    strict = wj < wi
    incl = wj <= wi
    eye_w = (wj == wi).astype(F32)
    first_w = wl < l
    first_c = lax.broadcasted_iota(jnp.int32, (l, LANES), 1) < hd
    bi = lax.broadcasted_iota(jnp.int32, (LANES, LANES), 0)
    bj = lax.broadcasted_iota(jnp.int32, (LANES, LANES), 1)
    same_head = (bi < hd) == (bj < hd)

    def block_rows(x, first):
        return jnp.concatenate([jnp.where(first, x, 0.0), jnp.where(first, 0.0, x)], axis=0)

    pairs = range(n_pair)
    lanes_of = [slice(p * LANES, (p + 1) * LANES) for p in pairs]
    dot = lambda x, y: jnp.dot(x, y, preferred_element_type=F32)
    ht = [state_ref[p] for p in pairs]
    lhs_ar = [jnp.concatenate([at_all[:, s], rt_all[:, s]], axis=0).astype(BF16) for s in lanes_of]
    gram = [_nt(lhs_ar[p], jnp.concatenate([block_rows(bt_all[:, s], first_c),
                                            block_rows(kt_all[:, s], first_c)], axis=0).astype(BF16))
            for p, s in zip(pairs, lanes_of)]
    h_ar = [_nt(lhs_ar[p], ht[p].astype(BF16)) for p in pairs]
    n_w = [jnp.where(strict, gram[p][0:l, 0:2 * l], 0.0) for p in pairs]
    t_w = [eye_w + n_w[p] for p in pairs]
    p_w = [dot(n_w[p].astype(BF16), block_rows(n_w[p], first_w).astype(BF16)) for p in pairs]
    for _ in range(int(np.log2(l)) - 1):
        res = [dot(jnp.concatenate([t_w[p], p_w[p]], axis=0).astype(BF16),
                   block_rows(p_w[p], first_w).astype(BF16)) for p in pairs]
        t_w = [t_w[p] + res[p][0:l] for p in pairs]
        p_w = [res[p][l:2 * l] for p in pairs]
    v_bd = [block_rows(v_all[:, s], first_c).astype(BF16) for s in lanes_of]
    rhs_u = [h_ar[p][0:l] + dot(jnp.where(strict, gram[p][0:l, 2 * l:4 * l], 0.0).astype(BF16), v_bd[p])
             for p in pairs]
    u = [dot(t_w[p].astype(BF16), block_rows(rhs_u[p], first_c).astype(BF16)) for p in pairs]
    for p, s in zip(pairs, lanes_of):
        rbk_w = jnp.where(jnp.concatenate([incl, incl], axis=1), gram[p][l:2 * l, :], 0.0)
        yacc_ref[:, s] = h_ar[p][l:2 * l] + dot(
            rbk_w.astype(BF16), jnp.concatenate([block_rows(u[p], first_c).astype(BF16), v_bd[p]], axis=0))
    for p, s in zip(pairs, lanes_of):
        e_l = e_last[:, s]
        upd = _tn(jnp.concatenate([u[p], v_all[:, s]], axis=0).astype(BF16),
                  jnp.concatenate([bt_all[:, s] * e_l, kt_all[:, s] * e_l], axis=0).astype(BF16))
        state_ref[p] = ht[p] * e_l + jnp.where(same_head, upd, 0.0)

    y = yacc_ref[...]
    ones = ones_ref[...]
    inv = 1.0 / hd
    mu = _head_sum(y, ones) * inv
    yc = y - mu
    var = _head_sum(yc * yc, ones) * inv
    yn = yc * lax.rsqrt(var + R_LN_EPS)
    y_ref[...] = (((yn * rlnw_ref[...] + rlnb_ref[...]) + bonus) * gate).astype(y_ref.dtype)


def rwkv_mixer(proj, lora, cols, mu, w0, a0, k_k, k_a, r_k, w_up, a_up, g_up, rln_w, rln_b):
    bsz, t, _ = proj.shape
    c = w0.shape[1]
    l = RWKV_CHUNK
    ones = jnp.asarray(np.kron(np.eye(MXU_DIM // R_HEAD_DIM), np.ones((R_HEAD_DIM, R_HEAD_DIM))), BF16)

    def col_block(width, off):
        blk = off // width
        return pl.BlockSpec((None, l, width), lambda i, j: (i, j, blk))

    fixed = lambda shape: pl.BlockSpec(shape, lambda i, j: (0, 0))
    return pl.pallas_call(
        _rwkv_chunk_kernel,
        grid=(bsz, t // l),
        in_specs=[col_block(c, cols["r"]), col_block(c, cols["k"]), col_block(c, cols["v"]),
                  col_block(LORA_BLOCK, cols["lora"]), fixed(mu.shape)]
        + [fixed((1, c))] * 5 + [fixed(w_up.shape), fixed(a_up.shape), fixed(g_up.shape), fixed(ones.shape),
                                 fixed((1, c)), fixed((1, c))],
        out_specs=pl.BlockSpec((None, l, c), lambda i, j: (i, j, 0)),
        out_shape=jax.ShapeDtypeStruct((bsz, t, c), BF16),
        scratch_shapes=[pltpu.VMEM((c // LANES, LANES, LANES), F32), pltpu.VMEM((l, c), F32),
                        pltpu.VMEM((SUBLANES, mu.shape[1]), F32)],
        compiler_params=_params("parallel", "arbitrary"),
        name="rwkv_mixer",
    )(proj, proj, proj, lora, mu, w0, a0, k_k, k_a, r_k, w_up, a_up, g_up, ones, rln_w, rln_b)


def _retention_kernel(q_ref, k_ref, v_ref, g_ref, pos_ref, freq_ref, y_ref, state_ref):
    l = q_ref.shape[0]
    dk = RET_QK_HEAD
    dv = RET_V_HEAD
    half = dk // 2

    @pl.when(pl.program_id(1) == 0)
    def _():
        state_ref[...] = jnp.zeros_like(state_ref)

    ang = pos_ref[...].astype(F32) * freq_ref[...]
    cos = jnp.cos(ang)
    sin = jnp.sin(ang)

    def rotate(u):
        ue = u[:, :half].astype(F32)
        uo = u[:, half:].astype(F32)
        return jnp.concatenate([ue * cos - uo * sin, uo * cos + ue * sin], axis=-1)

    row = lax.broadcasted_iota(jnp.int32, (l, l), 0)
    col = lax.broadcasted_iota(jnp.int32, (l, l), 1)
    rel = (row - col).astype(F32)
    causal = row >= col
    idx = lax.broadcasted_iota(jnp.int32, (l, 1), 0).astype(F32)
    for h in range(RET_HEADS):
        log_gamma = float(np.log(np.float32(1.0) - np.float32(2.0) ** np.float32(-5.0 - h)))
        intra = jnp.where(causal, jnp.exp(log_gamma * jnp.where(causal, rel, 0.0)), 0.0)
        q_decay = jnp.exp(log_gamma * (idx + 1.0))
        k_decay = jnp.exp(log_gamma * (l - 1.0 - idx))
        chunk_decay = float(np.exp(np.float32(log_gamma) * np.float32(l)))
        q_h = rotate(q_ref[:, h * dk:(h + 1) * dk]).astype(BF16)
        k_h = rotate(k_ref[:, h * dk:(h + 1) * dk]) * (dk ** -0.5)
        v_h = v_ref[:, h * dv:(h + 1) * dv].astype(BF16)
        s = lax.dot_general(q_h, k_h.astype(BF16), (((1,), (1,)), ((), ())),
                            preferred_element_type=F32) * intra
        st = state_ref[h]
        y = jnp.dot(s.astype(BF16), v_h, preferred_element_type=F32)
        y = y + jnp.dot(q_h, st.astype(BF16), preferred_element_type=F32) * q_decay
        kd = (k_h * k_decay).astype(BF16)
        state_ref[h] = st * chunk_decay + lax.dot_general(
            kd, v_h, (((0,), (0,)), ((), ())), preferred_element_type=F32)
        mu = jnp.mean(y, axis=-1, keepdims=True)
        yc = y - mu
        var = jnp.mean(yc * yc, axis=-1, keepdims=True)
        g_h = g_ref[:, h * dv:(h + 1) * dv].astype(F32)
        y_ref[:, h * dv:(h + 1) * dv] = (g_h * _sigmoid(g_h) * (yc * lax.rsqrt(var + LN_EPS))).astype(y_ref.dtype)


def retention_mixer(proj, positions, inv_freq):
    bsz, t, _ = proj.shape
    hk = RET_HEADS * RET_QK_HEAD
    hv = RET_HEADS * RET_V_HEAD
    l = CHUNK
    return pl.pallas_call(
        _retention_kernel,
        grid=(bsz, t // l),
        in_specs=[pl.BlockSpec((None, l, hk), lambda b, c: (b, c, 0)),
                  pl.BlockSpec((None, l, hk), lambda b, c: (b, c, 1)),
                  pl.BlockSpec((None, l, hv), lambda b, c: (b, c, 1)),
                  pl.BlockSpec((None, l, hv), lambda b, c: (b, c, 2)),
                  pl.BlockSpec((None, l, 1), lambda b, c: (b, c, 0)),
                  pl.BlockSpec((1, RET_QK_HEAD // 2), lambda b, c: (0, 0))],
        out_specs=pl.BlockSpec((None, l, hv), lambda b, c: (b, c, 0)),
        out_shape=jax.ShapeDtypeStruct((bsz, t, hv), BF16),
        scratch_shapes=[pltpu.VMEM((RET_HEADS, RET_QK_HEAD, RET_V_HEAD), F32)],
        compiler_params=_params("parallel", "arbitrary"),
        name="retention_mixer",
    )(proj, proj, proj, proj, positions, inv_freq)


def _even_layer(x, w_in, conv_w, conv_b, dt_bias, a_log, d_skip, m_norm, mu_shift, w0, w_up, a0, a_up,
                g_up, k_k, k_a, r_k, rln_w, rln_b, w_out, ln1_g, ln1_b, wg, wu, wd, ln2_g, ln2_b):
    bsz, t, d = x.shape
    n = bsz * t
    m_inner = m_norm.shape[0]
    m_heads = dt_bias.shape[0]
    c = w0.shape[0]
    r_heads = c // R_HEAD_DIM
    gn = M_GROUPS * M_STATE
    o_xbc = m_inner
    o_dt = o_xbc + m_inner + 2 * gn
    o_rw = o_dt + m_heads
    o_lora = o_rw + 3 * c
    n_lora = R_DECAY_LORA + R_AAA_LORA + R_GATE_LORA
    x2 = x.reshape(n, d)

    zeros = lambda k: jnp.zeros((d, k), w_in.dtype)
    w_main = jnp.concatenate([
        w_in[:, o_rw:o_lora],
        w_in[:, :m_inner],
        w_in[:, o_xbc:o_dt]], axis=1).astype(BF16)
    w_lora = jnp.concatenate([
        w_in[:, o_lora:o_lora + n_lora], zeros(LORA_DT_OFF - n_lora),
        w_in[:, o_dt:o_rw], zeros(LORA_BLOCK - LORA_DT_OFF - m_heads)], axis=1).astype(BF16)
    cols = {"r": 0, "k": c, "v": 2 * c, "z": 3 * c, "xs": 3 * c + m_inner, "B": 3 * c + 2 * m_inner,
            "C": 3 * c + 2 * m_inner + gn, "lora": 0, "dt": LORA_DT_OFF}
    proj = matmul(x2, w_main, tn=w_main.shape[1] // 4, out_dtype=BF16).reshape(bsz, t, -1)
    lora = matmul(x2, w_lora, tn=LORA_BLOCK).reshape(bsz, t, -1)

    pad_lane = lambda u: jnp.pad(u.reshape(1, -1), ((0, 0), (0, LANES - u.shape[0])))
    y_ssd = ssd_mixer(proj, lora, cols, conv_w, conv_b.reshape(1, -1), pad_lane(dt_bias), pad_lane(a_log),
                      jnp.repeat(d_skip, M_HEAD_DIM).reshape(1, -1), m_norm.reshape(1, -1))

    mu_p = jnp.pad(mu_shift, (0, LORA_BLOCK - n_lora)).reshape(1, -1)
    w_up_p = jnp.pad(w_up, ((0, LANES - R_DECAY_LORA), (0, 0))).astype(BF16)
    a_up_p = jnp.pad(a_up, ((R_DECAY_LORA, LANES - R_DECAY_LORA - R_AAA_LORA), (0, 0))).astype(BF16)
    g_up_p = jnp.pad(g_up, ((0, LORA_GATE_PAD - R_GATE_LORA), (0, 0))).astype(BF16)
    row = lambda u: u.reshape(1, -1)
    y_rwkv = rwkv_mixer(proj, lora, cols, mu_p, row(w0), row(a0), row(k_k), row(k_a), row(r_k),
                        w_up_p, a_up_p, g_up_p, row(rln_w), row(rln_b))
    x2 = out_proj_deepnorm(y_ssd.reshape(n, m_inner), y_rwkv.reshape(n, c), w_out.astype(BF16), x2,
                           row(ln1_g), row(ln1_b))
    x2 = ffn_deepnorm(x2, wg.astype(BF16), wu.astype(BF16), wd.astype(BF16), row(ln2_g), row(ln2_b))
    return x2.reshape(bsz, t, d)


def _odd_layer(x, positions, w_in, w_out, ln1_g, ln1_b, router, wg, wu, wd, ln2_g, ln2_b):
    bsz, t, d = x.shape
    n = bsz * t
    ret_qk = RET_HEADS * RET_QK_HEAD
    x2 = x.reshape(n, d)
    head_perm = np.concatenate([np.arange(0, RET_QK_HEAD, 2), np.arange(1, RET_QK_HEAD, 2)])
    qk_perm = (np.arange(2 * RET_HEADS)[:, None] * RET_QK_HEAD + head_perm[None, :]).reshape(-1)
    w_in_p = jnp.concatenate([w_in[:, qk_perm], w_in[:, 2 * ret_qk:]], axis=1).astype(BF16)
    proj = matmul(x2, w_in_p, out_dtype=BF16).reshape(bsz, t, -1)
    inv_freq = (1.0 / (10000.0 ** jnp.linspace(0.0, 1.0, RET_QK_HEAD // 2, dtype=F32))).reshape(1, -1)
    a = retention_mixer(proj, positions.reshape(bsz, t, 1), inv_freq)
    row = lambda u: u.reshape(1, -1)
    x2 = out_proj_deepnorm(a.reshape(n, -1), None, w_out.astype(BF16), x2, row(ln1_g), row(ln1_b))
    x2 = moe_top2_deepnorm(x2, jnp.pad(router, ((0, 0), (0, LANES - router.shape[1]))),
                           wg.astype(BF16), wu.astype(BF16), wd.astype(BF16), row(ln2_g), row(ln2_b))
    return x2.reshape(bsz, t, d)


def kernel(x, positions, ev_w_in, ev_conv_w, ev_conv_b, ev_dt_bias, ev_a_log, ev_d_skip, ev_m_norm, ev_mu_shift, ev_w0, ev_w_up, ev_a0, ev_a_up, ev_g_up, ev_k_k, ev_k_a, ev_r_k, ev_rln_w, ev_rln_b, ev_w_out, ev_ln1_g, ev_ln1_b, ev_ffn_wg, ev_ffn_wu, ev_ffn_wd, ev_ln2_g, ev_ln2_b, od_w_in, od_w_out, od_ln1_g, od_ln1_b, od_router, od_moe_wg, od_moe_wu, od_moe_wd, od_ln2_g, od_ln2_b):
    for layer in range(DEPTH):
        i = layer // 2
        if layer % 2 == 0:
            x = _even_layer(x, ev_w_in[i], ev_conv_w[i], ev_conv_b[i], ev_dt_bias[i], ev_a_log[i], ev_d_skip[i],
                            ev_m_norm[i], ev_mu_shift[i], ev_w0[i], ev_w_up[i], ev_a0[i], ev_a_up[i], ev_g_up[i],
                            ev_k_k[i], ev_k_a[i], ev_r_k[i], ev_rln_w[i], ev_rln_b[i], ev_w_out[i],
                            ev_ln1_g[i], ev_ln1_b[i], ev_ffn_wg[i], ev_ffn_wu[i], ev_ffn_wd[i],
                            ev_ln2_g[i], ev_ln2_b[i])
        else:
            x = _odd_layer(x, positions, od_w_in[i], od_w_out[i], od_ln1_g[i], od_ln1_b[i], od_router[i],
                           od_moe_wg[i], od_moe_wu[i], od_moe_wd[i], od_ln2_g[i], od_ln2_b[i])
    return x
```

```python
import jax
import jax.numpy as jnp
import numpy as np
from jax import lax
from jax.experimental import pallas as pl
from jax.experimental.pallas import tpu as pltpu

F32 = jnp.float32
BF16 = jnp.bfloat16
HIGHEST = lax.Precision.HIGHEST

CHUNK = 128
M_HEAD_DIM = 64
M_GROUPS = 2
M_STATE = 128
R_HEAD_DIM = 64
R_DECAY_LORA = 64
R_AAA_LORA = 64
R_GATE_LORA = 160
R_LN_EPS = 64e-5
RET_HEADS = 4
RET_QK_HEAD = 256
RET_V_HEAD = 512
N_EXPERTS = 8
LN_EPS = 1e-5
DEPTH = 2
ALPHA = (2.0 * DEPTH) ** 0.25

LANES = 128
SUBLANES = 8
MXU_DIM = 256
VMEM_LIMIT_BYTES = 56 * 1024 * 1024

LORA_BLOCK = 512
LORA_GATE_OFF = 128
LORA_GATE_PAD = 256
LORA_DT_OFF = 384


def _params(*sem):
    return pltpu.CompilerParams(dimension_semantics=sem, vmem_limit_bytes=VMEM_LIMIT_BYTES)


def _sigmoid(x):
    return jax.nn.sigmoid(x)


def _matmul_kernel(x_ref, w_ref, o_ref, xb_ref):
    @pl.when(pl.program_id(1) == 0)
    def _():
        xb_ref[...] = x_ref[...].astype(BF16)

    o_ref[...] = jnp.dot(xb_ref[...], w_ref[...], preferred_element_type=F32).astype(o_ref.dtype)


def matmul(x, w, *, tm=1024, tn=1024, out_dtype=F32):
    n, k = x.shape
    m = w.shape[1]
    tm = min(tm, n)
    return pl.pallas_call(
        _matmul_kernel,
        grid=(n // tm, m // tn),
        in_specs=[pl.BlockSpec((tm, k), lambda i, j: (i, 0)),
                  pl.BlockSpec((k, tn), lambda i, j: (0, j))],
        out_specs=pl.BlockSpec((tm, tn), lambda i, j: (i, j)),
        out_shape=jax.ShapeDtypeStruct((n, m), out_dtype),
        scratch_shapes=[pltpu.VMEM((tm, k), BF16)],
        compiler_params=_params("parallel", "arbitrary"),
        name="matmul",
    )(x, w)


def _deepnorm_rows(resid, sub, g, b):
    y = ALPHA * resid + sub
    mu = jnp.mean(y, axis=-1, keepdims=True)
    yc = y - mu
    var = jnp.mean(yc * yc, axis=-1, keepdims=True)
    return yc * lax.rsqrt(var + LN_EPS) * g + b


def _out_proj_kernel(a1_ref, a2_ref, w_ref, x_ref, g_ref, b_ref, o_ref):
    kh = a1_ref.shape[1]
    sub = jnp.dot(a1_ref[...], w_ref[:kh, :], preferred_element_type=F32)
    sub = sub + jnp.dot(a2_ref[...], w_ref[kh:, :], preferred_element_type=F32)
    o_ref[...] = _deepnorm_rows(x_ref[...], sub, g_ref[...], b_ref[...])


def out_proj_deepnorm(a1, a2, w, x, g, b, *, tm=512):
    n, d = x.shape
    kh = w.shape[0] // 2
    tm = min(tm, n)
    row = lambda i: (i, 0)
    fixed = lambda i: (0, 0)
    if a2 is None:
        a2, second = a1, pl.BlockSpec((tm, kh), lambda i: (i, 1))
    else:
        second = pl.BlockSpec((tm, kh), row)
    return pl.pallas_call(
        _out_proj_kernel,
        grid=(n // tm,),
        in_specs=[pl.BlockSpec((tm, kh), row), second,
                  pl.BlockSpec((2 * kh, d), fixed), pl.BlockSpec((tm, d), row),
                  pl.BlockSpec((1, d), fixed), pl.BlockSpec((1, d), fixed)],
        out_specs=pl.BlockSpec((tm, d), row),
        out_shape=jax.ShapeDtypeStruct((n, d), F32),
        compiler_params=_params("parallel"),
        name="out_proj_deepnorm",
    )(a1, a2, w, x, g, b)


TOP_K = 2
R_IDX = 0
R_PROB = TOP_K


def _router_kernel(x_ref, wr_ref, info_ref):
    logits = jnp.dot(x_ref[...], wr_ref[...], preferred_element_type=F32, precision=HIGHEST)
    lane = lax.broadcasted_iota(jnp.int32, logits.shape, 1)
    neg = jnp.float32(-jnp.inf)
    lg = jnp.where(lane < N_EXPERTS, logits, neg)
    m1 = jnp.max(lg, axis=-1, keepdims=True)
    i1 = jnp.min(jnp.where(lg == m1, lane, LANES), axis=-1, keepdims=True)
    lg2 = jnp.where(lane == i1, neg, lg)
    m2 = jnp.max(lg2, axis=-1, keepdims=True)
    i2 = jnp.min(jnp.where(lg2 == m2, lane, LANES), axis=-1, keepdims=True)
    e2 = jnp.exp(m2 - m1)
    p1 = 1.0 / (1.0 + e2)
    p2 = e2 / (1.0 + e2)
    info = jnp.where(lane == R_IDX, i1.astype(F32), 0.0)
    info = jnp.where(lane == R_IDX + 1, i2.astype(F32), info)
    info = jnp.where(lane == R_PROB, p1, info)
    info = jnp.where(lane == R_PROB + 1, p2, info)
    info_ref[...] = info


def router_top2(x, w_router, *, tm=1024):
    n, d = x.shape
    tm = min(tm, n)
    return pl.pallas_call(
        _router_kernel,
        grid=(n // tm,),
        in_specs=[pl.BlockSpec((tm, d), lambda i: (i, 0)), pl.BlockSpec((d, LANES), lambda i: (0, 0))],
        out_specs=pl.BlockSpec((tm, LANES), lambda i: (i, 0)),
        out_shape=jax.ShapeDtypeStruct((n, LANES), F32),
        compiler_params=_params("parallel"),
        name="router_top2",
    )(x, w_router)


def _combine_kernel(x_ref, y0_ref, y1_ref, info_ref, g_ref, b_ref, out_ref):
    sub = y0_ref[...] * info_ref[:, R_PROB:R_PROB + 1] + y1_ref[...] * info_ref[:, R_PROB + 1:R_PROB + 2]
    out_ref[...] = _deepnorm_rows(x_ref[...], sub, g_ref[...], b_ref[...])


def combine_deepnorm(x, y_slots, info, g, b, *, tc=512):
    n, d = x.shape
    tc = min(tc, n)
    nt = n // tc
    row = lambda i: (i, 0)
    fixed = lambda i: (0, 0)
    return pl.pallas_call(
        _combine_kernel,
        grid=(nt,),
        in_specs=[pl.BlockSpec((tc, d), row), pl.BlockSpec((tc, d), row),
                  pl.BlockSpec((tc, d), lambda i: (i + nt, 0)), pl.BlockSpec((tc, LANES), row),
                  pl.BlockSpec((1, d), fixed), pl.BlockSpec((1, d), fixed)],
        out_specs=pl.BlockSpec((tc, d), row),
        out_shape=jax.ShapeDtypeStruct((n, d), F32),
        compiler_params=_params("parallel"),
        name="combine_deepnorm",
    )(x, y_slots, y_slots, info, g, b)


DMA_ISSUE_UNROLL = 8


def _start_row_copies(src_ref, src_row, dst_ref, dst_row, sem, n_rows):
    def issue(r, carry):
        pltpu.make_async_copy(src_ref.at[pl.ds(src_row(r), 1)], dst_ref.at[pl.ds(dst_row(r), 1)], sem).start()
        return carry

    lax.fori_loop(0, n_rows, issue, 0, unroll=DMA_ISSUE_UNROLL)


def _expert_kernel(te_ref, src_ref, src_next_ref, dst_prev_ref, x_hbm, wg_ref, wu_ref, wd_ref, y_hbm,
                   xbuf_ref, ybuf_ref, acc_ref, xb_ref, xsem, ysem):
    i = pl.program_id(0)
    f = pl.program_id(1)
    n_i = pl.num_programs(0)
    n_f = pl.num_programs(1)
    tm = acc_ref.shape[0]
    slot = i % 2

    def wait_tile(src, dst, sem):
        pltpu.make_async_copy(src, dst, sem).wait()

    def swiglu_step(first):
        xb = xb_ref[...]
        hg = jnp.dot(xb, wg_ref[...], preferred_element_type=F32)
        hu = jnp.dot(xb, wu_ref[...], preferred_element_type=F32)
        h = (hg * _sigmoid(hg)) * hu
        out = jnp.dot(h.astype(BF16), wd_ref[...], preferred_element_type=F32)
        acc_ref[...] = out if first else acc_ref[...] + out

    @pl.when(f == 0)
    def _():
        @pl.when(i == 0)
        def _():
            _start_row_copies(x_hbm, lambda r: src_ref[0, r], xbuf_ref.at[0], lambda r: r, xsem.at[0], tm)
            ybuf_ref[...] = jnp.zeros_like(ybuf_ref)

        wait_tile(x_hbm.at[pl.ds(0, tm)], xbuf_ref.at[slot], xsem.at[slot])
        xb_ref[...] = xbuf_ref[slot].astype(BF16)
        swiglu_step(True)
        for r in range(tm):
            pltpu.make_async_copy(ybuf_ref.at[pl.ds(r, 1)], y_hbm.at[pl.ds(dst_prev_ref[0, r], 1)], ysem).start()
            pltpu.make_async_copy(x_hbm.at[pl.ds(src_next_ref[0, r], 1)], xbuf_ref.at[1 - slot, pl.ds(r, 1)],
                                  xsem.at[1 - slot]).start()

    @pl.when(f > 0)
    def _():
        swiglu_step(False)

    @pl.when(f == n_f - 1)
    def _():
        wait_tile(ybuf_ref, y_hbm.at[pl.ds(0, tm)], ysem)
        ybuf_ref[...] = acc_ref[...]

        @pl.when(i == n_i - 1)
        def _():
            wait_tile(x_hbm.at[pl.ds(0, tm)], xbuf_ref.at[1 - slot], xsem.at[1 - slot])


def expert_swiglu(x, src_rows, dst_rows, tile_expert, wg, wu, wd, *, tm, tf=1792):
    d = x.shape[1]
    r = src_rows.shape[0]
    ff = wg.shape[2]
    n_tiles = r // tm
    last = n_tiles - 1
    idx_tile = lambda u: u.reshape(n_tiles, 1, tm)
    smem_tile = lambda index: pl.BlockSpec((None, 1, tm), index, memory_space=pltpu.SMEM)
    tile = lambda i: jnp.minimum(i, last)
    grid_spec = pltpu.PrefetchScalarGridSpec(
        num_scalar_prefetch=1,
        grid=(n_tiles + 1, ff // tf),
        in_specs=[smem_tile(lambda i, f, te: (tile(i), 0, 0)),
                  smem_tile(lambda i, f, te: (tile(i + 1), 0, 0)),
                  smem_tile(lambda i, f, te: (jnp.maximum(i - 1, 0), 0, 0)),
                  pl.BlockSpec(memory_space=pl.ANY),
                  pl.BlockSpec((None, d, tf), lambda i, f, te: (te[tile(i)], 0, f)),
                  pl.BlockSpec((None, d, tf), lambda i, f, te: (te[tile(i)], 0, f)),
                  pl.BlockSpec((None, tf, d), lambda i, f, te: (te[tile(i)], f, 0))],
        out_specs=pl.BlockSpec(memory_space=pl.ANY),
        scratch_shapes=[pltpu.VMEM((2, tm, d), F32), pltpu.VMEM((tm, d), F32), pltpu.VMEM((tm, d), F32),
                        pltpu.VMEM((tm, d), BF16), pltpu.SemaphoreType.DMA((2,)), pltpu.SemaphoreType.DMA(())],
    )
    return pl.pallas_call(
        _expert_kernel,
        grid_spec=grid_spec,
        out_shape=jax.ShapeDtypeStruct((r, d), F32),
        compiler_params=_params("arbitrary", "arbitrary"),
        name="expert_swiglu",
    )(tile_expert, idx_tile(src_rows), idx_tile(src_rows), idx_tile(dst_rows), x, wg, wu, wd)


def moe_top2_deepnorm(x, w_router, wg, wu, wd, g, b, *, tm=512):
    n = x.shape[0]
    ne = wg.shape[0]
    info = router_top2(x, w_router)
    e_flat = info[:, R_IDX:R_IDX + TOP_K].astype(jnp.int32).reshape(-1)

    onehot = (e_flat[:, None] == jnp.arange(ne, dtype=jnp.int32)[None, :]).astype(jnp.int32)
    csum = jnp.cumsum(onehot, axis=0)
    rank = jnp.sum((csum - 1) * onehot, axis=1)
    padded = ((csum[-1] + tm - 1) // tm) * tm
    ends = jnp.cumsum(padded)
    dest = (ends - padded)[e_flat] + rank
    n_rows = n * TOP_K + ne * tm
    n_tiles = n_rows // tm
    assign = jnp.full((n_rows,), -1, jnp.int32).at[dest].set(
        jnp.arange(n * TOP_K, dtype=jnp.int32), unique_indices=True, mode="promise_in_bounds")
    token, slot = assign // TOP_K, assign % TOP_K
    src_rows = jnp.where(assign >= 0, token, 0)
    pad_rank = jnp.cumsum((assign < 0).astype(jnp.int32)) - 1
    dst_rows = jnp.where(assign >= 0, slot * n + token, n * TOP_K + pad_rank)
    tile_start = jnp.arange(n_tiles, dtype=jnp.int32) * tm
    tile_expert = jnp.minimum(jnp.sum((tile_start[:, None] >= ends[None, :]).astype(jnp.int32), axis=1), ne - 1)

    y_slots = expert_swiglu(x, src_rows, dst_rows, tile_expert, wg, wu, wd, tm=tm)
    return combine_deepnorm(x, y_slots, info, g, b)


def _ffn_kernel(x_ref, wg_ref, wu_ref, wd_ref, g_ref, b_ref, o_ref, acc_ref, xb_ref):
    f = pl.program_id(1)

    @pl.when(f == 0)
    def _():
        acc_ref[...] = jnp.zeros_like(acc_ref)
        xb_ref[...] = x_ref[...].astype(BF16)

    xb = xb_ref[...]
    hg = jnp.dot(xb, wg_ref[...], preferred_element_type=F32)
    hu = jnp.dot(xb, wu_ref[...], preferred_element_type=F32)
    h = (hg * _sigmoid(hg)) * hu
    acc_ref[...] += jnp.dot(h.astype(BF16), wd_ref[...], preferred_element_type=F32)

    @pl.when(f == pl.num_programs(1) - 1)
    def _():
        o_ref[...] = _deepnorm_rows(x_ref[...], acc_ref[...], g_ref[...], b_ref[...])


def ffn_deepnorm(x, wg, wu, wd, g, b, *, tm=512, tf=1792):
    n, d = x.shape
    ff = wg.shape[1]
    tm = min(tm, n)
    row = lambda i, f: (i, 0)
    fixed = lambda i, f: (0, 0)
    return pl.pallas_call(
        _ffn_kernel,
        grid=(n // tm, ff // tf),
        in_specs=[pl.BlockSpec((tm, d), row),
                  pl.BlockSpec((d, tf), lambda i, f: (0, f)),
                  pl.BlockSpec((d, tf), lambda i, f: (0, f)),
                  pl.BlockSpec((tf, d), lambda i, f: (f, 0)),
                  pl.BlockSpec((1, d), fixed), pl.BlockSpec((1, d), fixed)],
        out_specs=pl.BlockSpec((tm, d), row),
        out_shape=jax.ShapeDtypeStruct((n, d), F32),
        scratch_shapes=[pltpu.VMEM((tm, d), F32), pltpu.VMEM((tm, d), BF16)],
        compiler_params=_params("parallel", "arbitrary"),
        name="ffn_deepnorm",
    )(x, wg, wu, wd, g, b)


CONV_HISTORY = SUBLANES


def _ssd_kernel(xs_ref, b_ref, c_ref, z_ref, dt_ref, cw_ref, cb_ref, dtb_ref, alog_ref, dskip_ref, mnorm_ref,
                y_ref, state_ref, ext_ref, yacc_ref):
    l, hp = xs_ref.shape
    gn = b_ref.shape[1]
    p = M_HEAD_DIM
    ns = M_STATE
    nh = hp // p
    hpg = nh // M_GROUPS
    kc = cw_ref.shape[0]
    hist = CONV_HISTORY

    @pl.when(pl.program_id(1) == 0)
    def _():
        state_ref[...] = jnp.zeros_like(state_ref)
        ext_ref[0:hist, :] = jnp.zeros((hist, ext_ref.shape[1]), F32)

    ext_ref[hist:, 0:hp] = xs_ref[...].astype(F32)
    ext_ref[hist:, hp:hp + gn] = b_ref[...].astype(F32)
    ext_ref[hist:, hp + gn:] = c_ref[...].astype(F32)
    conv = cb_ref[...]
    for i in range(kc):
        conv = conv + cw_ref[i:i + 1, :] * ext_ref[pl.ds(hist - (kc - 1) + i, l), :]
    ext_ref[0:hist, :] = ext_ref[l:l + hist, :]
    xbc = conv * _sigmoid(conv)
    xs = xbc[:, :hp]

    dt_pre = dt_ref[...] + dtb_ref[...]
    dt = jnp.maximum(dt_pre, 0.0) + jnp.log(1.0 + jnp.exp(-jnp.abs(dt_pre)))
    a = -jnp.exp(alog_ref[...])

    row = lax.broadcasted_iota(jnp.int32, (l, l), 0)
    col = lax.broadcasted_iota(jnp.int32, (l, l), 1)
    causal = row >= col
    acs = jnp.dot(causal.astype(F32), dt * a, preferred_element_type=F32, precision=HIGHEST)
    acst = acs.T
    dtt = dt.T
    acs_last = acs[l - 1:l, :]
    w_end = jnp.exp(acs_last - acs) * dt
    exp_acs = jnp.exp(acs)
    exp_last = jnp.exp(acs_last)

    for g in range(M_GROUPS):
        b_g = xbc[:, hp + g * ns:hp + (g + 1) * ns]
        c_g = xbc[:, hp + gn + g * ns:hp + gn + (g + 1) * ns].astype(BF16)
        bt_g = b_g.T.astype(BF16)
        cb = jnp.dot(c_g, bt_g, preferred_element_type=F32)
        for hh in range(hpg):
            h = g * hpg + hh
            x_h = xs[:, h * p:(h + 1) * p]
            seg = acs[:, h:h + 1] - acst[h:h + 1, :]
            decay = jnp.where(causal, jnp.exp(seg), 0.0)
            m = cb * decay * dtt[h:h + 1, :]
            st = state_ref[h]
            y = jnp.dot(m.astype(BF16), x_h.astype(BF16), preferred_element_type=F32)
            y = y + jnp.dot(c_g, st.astype(BF16), preferred_element_type=F32) * exp_acs[:, h:h + 1]
            yacc_ref[:, h * p:(h + 1) * p] = y
            xw = (x_h * w_end[:, h:h + 1]).astype(BF16)
            state_ref[h] = st * exp_last[:, h:h + 1] + jnp.dot(bt_g, xw, preferred_element_type=F32)

    z = z_ref[...].astype(F32)
    y = (yacc_ref[...] + dskip_ref[...] * xs) * (z * _sigmoid(z))
    gw = hp // M_GROUPS
    for g in range(M_GROUPS):
        seg = y[:, g * gw:(g + 1) * gw]
        ms = jnp.mean(seg * seg, axis=-1, keepdims=True)
        y_ref[:, g * gw:(g + 1) * gw] = (seg * lax.rsqrt(ms + LN_EPS)
                                         * mnorm_ref[:, g * gw:(g + 1) * gw]).astype(y_ref.dtype)


def ssd_mixer(proj, lora, cols, conv_w, conv_b, dt_bias, a_log, d_skip, m_norm):
    bsz, t, _ = proj.shape
    hp = m_norm.shape[1]
    gn = M_GROUPS * M_STATE
    conv_dim = hp + 2 * gn
    nh = hp // M_HEAD_DIM
    l = CHUNK

    def col_block(width, off):
        blk = off // width
        return pl.BlockSpec((None, l, width), lambda b, c: (b, c, blk))

    fixed = lambda b, c: (0, 0)
    return pl.pallas_call(
        _ssd_kernel,
        grid=(bsz, t // l),
        in_specs=[col_block(hp, cols["xs"]), col_block(gn, cols["B"]), col_block(gn, cols["C"]),
                  col_block(hp, cols["z"]), col_block(LANES, cols["dt"]),
                  pl.BlockSpec(conv_w.shape, fixed), pl.BlockSpec((1, conv_dim), fixed),
                  pl.BlockSpec((1, LANES), fixed), pl.BlockSpec((1, LANES), fixed),
                  pl.BlockSpec((1, hp), fixed), pl.BlockSpec((1, hp), fixed)],
        out_specs=pl.BlockSpec((None, l, hp), lambda b, c: (b, c, 0)),
        out_shape=jax.ShapeDtypeStruct((bsz, t, hp), BF16),
        scratch_shapes=[pltpu.VMEM((nh, M_STATE, M_HEAD_DIM), F32),
                        pltpu.VMEM((CONV_HISTORY + l, conv_dim), F32),
                        pltpu.VMEM((l, hp), F32)],
        compiler_params=_params("parallel", "arbitrary"),
        name="ssd_mixer",
    )(proj, proj, proj, proj, lora, conv_w, conv_b, dt_bias, a_log, d_skip, m_norm)


EXP_MINUS_HALF = float(np.exp(-0.5))


def _head_sum(x, ones):
    hi = x.astype(BF16)
    lo = (x - hi.astype(F32)).astype(BF16)
    outs = []
    for j in range(x.shape[1] // MXU_DIM):
        sl = slice(j * MXU_DIM, (j + 1) * MXU_DIM)
        outs.append(jnp.dot(hi[:, sl], ones, preferred_element_type=F32)
                    + jnp.dot(lo[:, sl], ones, preferred_element_type=F32))
    return jnp.concatenate(outs, axis=-1)


def _rwkv_prep(r_ref, k_ref, v_ref, lo_ref, mu_ref, w0_ref, a0_ref, kk_ref, ka_ref, rk_ref,
               wup_ref, aup_ref, gup_ref, ones_ref, prev_ref):
    tl, c = r_ref.shape
    first_row = lax.broadcasted_iota(jnp.int32, (tl, 1), 0) == 0

    def shift_mix(x_ref, off):
        x = x_ref[...].astype(F32)
        width = x.shape[1]
        prev = jnp.where(first_row, prev_ref[0:1, off:off + width], pltpu.roll(x, 1, 0))
        prev_ref[0:1, off:off + width] = x[tl - 1:tl, :]
        return x + (prev - x) * mu_ref[:, off:off + width]

    r = shift_mix(r_ref, 0)
    k = shift_mix(k_ref, c)
    v = shift_mix(v_ref, 2 * c)
    lo = shift_mix(lo_ref, 3 * c)
    lo_a = lo[:, 0:LANES]
    lo_g = lo[:, LORA_GATE_OFF:LORA_GATE_OFF + LORA_GATE_PAD]
    w = w0_ref[...] + jnp.dot(jnp.tanh(lo_a).astype(BF16), wup_ref[...], preferred_element_type=F32)
    log_decay = -EXP_MINUS_HALF * _sigmoid(w)
    iclr = _sigmoid(a0_ref[...] + jnp.dot(lo_a.astype(BF16), aup_ref[...], preferred_element_type=F32))
    gate = jnp.dot(_sigmoid(lo_g).astype(BF16), gup_ref[...], preferred_element_type=F32)
    ones = ones_ref[...]
    kk = k * kk_ref[...]
    kk = kk / jnp.maximum(jnp.sqrt(_head_sum(kk * kk, ones)), 1e-12)
    k2 = k * (1.0 + (iclr - 1.0) * ka_ref[...])
    bonus = _head_sum(r * k2 * rk_ref[...], ones) * v
    return r, log_decay, k2, v, -kk, kk * iclr, bonus, gate


RWKV_CHUNK = 64


def _nt(a, b):
    return lax.dot_general(a, b, (((1,), (1,)), ((), ())), preferred_element_type=F32)


def _tn(a, b):
    return lax.dot_general(a, b, (((0,), (0,)), ((), ())), preferred_element_type=F32)


def _rwkv_chunk_kernel(r_ref, k_ref, v_ref, lo_ref, mu_ref, w0_ref, a0_ref, kk_ref, ka_ref, rk_ref,
                       wup_ref, aup_ref, gup_ref, ones_ref, rlnw_ref, rlnb_ref,
                       y_ref, state_ref, yacc_ref, prev_ref):
    l, c = r_ref.shape
    hd = R_HEAD_DIM
    n_pair = c // LANES

    @pl.when(pl.program_id(1) == 0)
    def _():
        state_ref[...] = jnp.zeros_like(state_ref)
        prev_ref[...] = jnp.zeros_like(prev_ref)

    r_all, lw, k_all, v_all, a_all, b_all, bonus, gate = _rwkv_prep(
        r_ref, k_ref, v_ref, lo_ref, mu_ref, w0_ref, a0_ref, kk_ref, ka_ref, rk_ref,
        wup_ref, aup_ref, gup_ref, ones_ref, prev_ref)
    row = lax.broadcasted_iota(jnp.int32, (l, l), 0)
    col = lax.broadcasted_iota(jnp.int32, (l, l), 1)
    g = jnp.dot((row >= col).astype(F32), lw, preferred_element_type=F32, precision=HIGHEST)
    g_last = g[l - 1:l, :]
    e_g = jnp.exp(g)
    e_ng = jnp.exp(-g)
    e_last = jnp.exp(g_last)
    at_all = a_all * jnp.exp(g - lw)
    rt_all = r_all * e_g
    bt_all = b_all * e_ng
    kt_all = k_all * e_ng

    wi = lax.broadcasted_iota(jnp.int32, (l, 2 * l), 0)
    wl = lax.broadcasted_iota(jnp.int32, (l, 2 * l), 1)
    wj = jnp.where(wl >= l, wl - l, wl)
    strict = wj < wi
    incl = wj <= wi
    eye_w = (wj == wi).astype(F32)
    first_w = wl < l
    first_c = lax.broadcasted_iota(jnp.int32, (l, LANES), 1) < hd
    bi = lax.broadcasted_iota(jnp.int32, (LANES, LANES), 0)
    bj = lax.broadcasted_iota(jnp.int32, (LANES, LANES), 1)
    same_head = (bi < hd) == (bj < hd)

    def block_rows(x, first):
        return jnp.concatenate([jnp.where(first, x, 0.0), jnp.where(first, 0.0, x)], axis=0)

    pairs = range(n_pair)
    lanes_of = [slice(p * LANES, (p + 1) * LANES) for p in pairs]
    dot = lambda x, y: jnp.dot(x, y, preferred_element_type=F32)
    ht = [state_ref[p] for p in pairs]
    lhs_ar = [jnp.concatenate([at_all[:, s], rt_all[:, s]], axis=0).astype(BF16) for s in lanes_of]
    gram = [_nt(lhs_ar[p], jnp.concatenate([block_rows(bt_all[:, s], first_c),
                                            block_rows(kt_all[:, s], first_c)], axis=0).astype(BF16))
            for p, s in zip(pairs, lanes_of)]
    h_ar = [_nt(lhs_ar[p], ht[p].astype(BF16)) for p in pairs]
    n_w = [jnp.where(strict, gram[p][0:l, 0:2 * l], 0.0) for p in pairs]
    t_w = [eye_w + n_w[p] for p in pairs]
    p_w = [dot(n_w[p].astype(BF16), block_rows(n_w[p], first_w).astype(BF16)) for p in pairs]
    for _ in range(int(np.log2(l)) - 1):
        res = [dot(jnp.concatenate([t_w[p], p_w[p]], axis=0).astype(BF16),
                   block_rows(p_w[p], first_w).astype(BF16)) for p in pairs]
        t_w = [t_w[p] + res[p][0:l] for p in pairs]
        p_w = [res[p][l:2 * l] for p in pairs]
    v_bd = [block_rows(v_all[:, s], first_c).astype(BF16) for s in lanes_of]
    rhs_u = [h_ar[p][0:l] + dot(jnp.where(strict, gram[p][0:l, 2 * l:4 * l], 0.0).astype(BF16), v_bd[p])
             for p in pairs]
    u = [dot(t_w[p].astype(BF16), block_rows(rhs_u[p], first_c).astype(BF16)) for p in pairs]
    for p, s in zip(pairs, lanes_of):
        rbk_w = jnp.where(jnp.concatenate([incl, incl], axis=1), gram[p][l:2 * l, :], 0.0)
        yacc_ref[:, s] = h_ar[p][l:2 * l] + dot(
            rbk_w.astype(BF16), jnp.concatenate([block_rows(u[p], first_c).astype(BF16), v_bd[p]], axis=0))
    for p, s in zip(pairs, lanes_of):
        e_l = e_last[:, s]
        upd = _tn(jnp.concatenate([u[p], v_all[:, s]], axis=0).astype(BF16),
                  jnp.concatenate([bt_all[:, s] * e_l, kt_all[:, s] * e_l], axis=0).astype(BF16))
        state_ref[p] = ht[p] * e_l + jnp.where(same_head, upd, 0.0)

    y = yacc_ref[...]
    ones = ones_ref[...]
    inv = 1.0 / hd
    mu = _head_sum(y, ones) * inv
    yc = y - mu
    var = _head_sum(yc * yc, ones) * inv
    yn = yc * lax.rsqrt(var + R_LN_EPS)
    y_ref[...] = (((yn * rlnw_ref[...] + rlnb_ref[...]) + bonus) * gate).astype(y_ref.dtype)


def rwkv_mixer(proj, lora, cols, mu, w0, a0, k_k, k_a, r_k, w_up, a_up, g_up, rln_w, rln_b):
    bsz, t, _ = proj.shape
    c = w0.shape[1]
    l = RWKV_CHUNK
    ones = jnp.asarray(np.kron(np.eye(MXU_DIM // R_HEAD_DIM), np.ones((R_HEAD_DIM, R_HEAD_DIM))), BF16)

    def col_block(width, off):
        blk = off // width
        return pl.BlockSpec((None, l, width), lambda i, j: (i, j, blk))

    fixed = lambda shape: pl.BlockSpec(shape, lambda i, j: (0, 0))
    return pl.pallas_call(
        _rwkv_chunk_kernel,
        grid=(bsz, t // l),
        in_specs=[col_block(c, cols["r"]), col_block(c, cols["k"]), col_block(c, cols["v"]),
                  col_block(LORA_BLOCK, cols["lora"]), fixed(mu.shape)]
        + [fixed((1, c))] * 5 + [fixed(w_up.shape), fixed(a_up.shape), fixed(g_up.shape), fixed(ones.shape),
                                 fixed((1, c)), fixed((1, c))],
        out_specs=pl.BlockSpec((None, l, c), lambda i, j: (i, j, 0)),
        out_shape=jax.ShapeDtypeStruct((bsz, t, c), BF16),
        scratch_shapes=[pltpu.VMEM((c // LANES, LANES, LANES), F32), pltpu.VMEM((l, c), F32),
                        pltpu.VMEM((SUBLANES, mu.shape[1]), F32)],
        compiler_params=_params("parallel", "arbitrary"),
        name="rwkv_mixer",
    )(proj, proj, proj, lora, mu, w0, a0, k_k, k_a, r_k, w_up, a_up, g_up, ones, rln_w, rln_b)


def _retention_kernel(q_ref, k_ref, v_ref, g_ref, pos_ref, freq_ref, y_ref, state_ref):
    l = q_ref.shape[0]
    dk = RET_QK_HEAD
    dv = RET_V_HEAD
    half = dk // 2

    @pl.when(pl.program_id(1) == 0)
    def _():
        state_ref[...] = jnp.zeros_like(state_ref)

    ang = pos_ref[...].astype(F32) * freq_ref[...]
    cos = jnp.cos(ang)
    sin = jnp.sin(ang)

    def rotate(u):
        ue = u[:, :half].astype(F32)
        uo = u[:, half:].astype(F32)
        return jnp.concatenate([ue * cos - uo * sin, uo * cos + ue * sin], axis=-1)

    row = lax.broadcasted_iota(jnp.int32, (l, l), 0)
    col = lax.broadcasted_iota(jnp.int32, (l, l), 1)
    rel = (row - col).astype(F32)
    causal = row >= col
    idx = lax.broadcasted_iota(jnp.int32, (l, 1), 0).astype(F32)
    for h in range(RET_HEADS):
        log_gamma = float(np.log(np.float32(1.0) - np.float32(2.0) ** np.float32(-5.0 - h)))
        intra = jnp.where(causal, jnp.exp(log_gamma * jnp.where(causal, rel, 0.0)), 0.0)
        q_decay = jnp.exp(log_gamma * (idx + 1.0))
        k_decay = jnp.exp(log_gamma * (l - 1.0 - idx))
        chunk_decay = float(np.exp(np.float32(log_gamma) * np.float32(l)))
        q_h = rotate(q_ref[:, h * dk:(h + 1) * dk]).astype(BF16)
        k_h = rotate(k_ref[:, h * dk:(h + 1) * dk]) * (dk ** -0.5)
        v_h = v_ref[:, h * dv:(h + 1) * dv].astype(BF16)
        s = lax.dot_general(q_h, k_h.astype(BF16), (((1,), (1,)), ((), ())),
                            preferred_element_type=F32) * intra
        st = state_ref[h]
        y = jnp.dot(s.astype(BF16), v_h, preferred_element_type=F32)
        y = y + jnp.dot(q_h, st.astype(BF16), preferred_element_type=F32) * q_decay
        kd = (k_h * k_decay).astype(BF16)
        state_ref[h] = st * chunk_decay + lax.dot_general(
            kd, v_h, (((0,), (0,)), ((), ())), preferred_element_type=F32)
        mu = jnp.mean(y, axis=-1, keepdims=True)
        yc = y - mu
        var = jnp.mean(yc * yc, axis=-1, keepdims=True)
        g_h = g_ref[:, h * dv:(h + 1) * dv].astype(F32)
        y_ref[:, h * dv:(h + 1) * dv] = (g_h * _sigmoid(g_h) * (yc * lax.rsqrt(var + LN_EPS))).astype(y_ref.dtype)


def retention_mixer(proj, positions, inv_freq):
    bsz, t, _ = proj.shape
    hk = RET_HEADS * RET_QK_HEAD
    hv = RET_HEADS * RET_V_HEAD
    l = CHUNK
    return pl.pallas_call(
        _retention_kernel,
        grid=(bsz, t // l),
        in_specs=[pl.BlockSpec((None, l, hk), lambda b, c: (b, c, 0)),
                  pl.BlockSpec((None, l, hk), lambda b, c: (b, c, 1)),
                  pl.BlockSpec((None, l, hv), lambda b, c: (b, c, 1)),
                  pl.BlockSpec((None, l, hv), lambda b, c: (b, c, 2)),
                  pl.BlockSpec((None, l, 1), lambda b, c: (b, c, 0)),
                  pl.BlockSpec((1, RET_QK_HEAD // 2), lambda b, c: (0, 0))],
        out_specs=pl.BlockSpec((None, l, hv), lambda b, c: (b, c, 0)),
        out_shape=jax.ShapeDtypeStruct((bsz, t, hv), BF16),
        scratch_shapes=[pltpu.VMEM((RET_HEADS, RET_QK_HEAD, RET_V_HEAD), F32)],
        compiler_params=_params("parallel", "arbitrary"),
        name="retention_mixer",
    )(proj, proj, proj, proj, positions, inv_freq)


def _even_layer(x, w_in, conv_w, conv_b, dt_bias, a_log, d_skip, m_norm, mu_shift, w0, w_up, a0, a_up,
                g_up, k_k, k_a, r_k, rln_w, rln_b, w_out, ln1_g, ln1_b, wg, wu, wd, ln2_g, ln2_b):
    bsz, t, d = x.shape
    n = bsz * t
    m_inner = m_norm.shape[0]
    m_heads = dt_bias.shape[0]
    c = w0.shape[0]
    r_heads = c // R_HEAD_DIM
    gn = M_GROUPS * M_STATE
    o_xbc = m_inner
    o_dt = o_xbc + m_inner + 2 * gn
    o_rw = o_dt + m_heads
    o_lora = o_rw + 3 * c
    n_lora = R_DECAY_LORA + R_AAA_LORA + R_GATE_LORA
    x2 = x.reshape(n, d)

    zeros = lambda k: jnp.zeros((d, k), w_in.dtype)
    w_main = jnp.concatenate([
        w_in[:, o_rw:o_lora],
        w_in[:, :m_inner],
        w_in[:, o_xbc:o_dt]], axis=1).astype(BF16)
    w_lora = jnp.concatenate([
        w_in[:, o_lora:o_lora + n_lora], zeros(LORA_DT_OFF - n_lora),
        w_in[:, o_dt:o_rw], zeros(LORA_BLOCK - LORA_DT_OFF - m_heads)], axis=1).astype(BF16)
    cols = {"r": 0, "k": c, "v": 2 * c, "z": 3 * c, "xs": 3 * c + m_inner, "B": 3 * c + 2 * m_inner,
            "C": 3 * c + 2 * m_inner + gn, "lora": 0, "dt": LORA_DT_OFF}
    proj = matmul(x2, w_main, tn=w_main.shape[1] // 4, out_dtype=BF16).reshape(bsz, t, -1)
    lora = matmul(x2, w_lora, tn=LORA_BLOCK).reshape(bsz, t, -1)

    pad_lane = lambda u: jnp.pad(u.reshape(1, -1), ((0, 0), (0, LANES - u.shape[0])))
    y_ssd = ssd_mixer(proj, lora, cols, conv_w, conv_b.reshape(1, -1), pad_lane(dt_bias), pad_lane(a_log),
                      jnp.repeat(d_skip, M_HEAD_DIM).reshape(1, -1), m_norm.reshape(1, -1))

    mu_p = jnp.pad(mu_shift, (0, LORA_BLOCK - n_lora)).reshape(1, -1)
    w_up_p = jnp.pad(w_up, ((0, LANES - R_DECAY_LORA), (0, 0))).astype(BF16)
    a_up_p = jnp.pad(a_up, ((R_DECAY_LORA, LANES - R_DECAY_LORA - R_AAA_LORA), (0, 0))).astype(BF16)
    g_up_p = jnp.pad(g_up, ((0, LORA_GATE_PAD - R_GATE_LORA), (0, 0))).astype(BF16)
    row = lambda u: u.reshape(1, -1)
    y_rwkv = rwkv_mixer(proj, lora, cols, mu_p, row(w0), row(a0), row(k_k), row(k_a), row(r_k),
                        w_up_p, a_up_p, g_up_p, row(rln_w), row(rln_b))
    x2 = out_proj_deepnorm(y_ssd.reshape(n, m_inner), y_rwkv.reshape(n, c), w_out.astype(BF16), x2,
                           row(ln1_g), row(ln1_b))
    x2 = ffn_deepnorm(x2, wg.astype(BF16), wu.astype(BF16), wd.astype(BF16), row(ln2_g), row(ln2_b))
    return x2.reshape(bsz, t, d)


def _odd_layer(x, positions, w_in, w_out, ln1_g, ln1_b, router, wg, wu, wd, ln2_g, ln2_b):
    bsz, t, d = x.shape
    n = bsz * t
    ret_qk = RET_HEADS * RET_QK_HEAD
    x2 = x.reshape(n, d)
    head_perm = np.concatenate([np.arange(0, RET_QK_HEAD, 2), np.arange(1, RET_QK_HEAD, 2)])
    qk_perm = (np.arange(2 * RET_HEADS)[:, None] * RET_QK_HEAD + head_perm[None, :]).reshape(-1)
    w_in_p = jnp.concatenate([w_in[:, qk_perm], w_in[:, 2 * ret_qk:]], axis=1).astype(BF16)
    proj = matmul(x2, w_in_p, out_dtype=BF16).reshape(bsz, t, -1)
    inv_freq = (1.0 / (10000.0 ** jnp.linspace(0.0, 1.0, RET_QK_HEAD // 2, dtype=F32))).reshape(1, -1)
    a = retention_mixer(proj, positions.reshape(bsz, t, 1), inv_freq)
    row = lambda u: u.reshape(1, -1)
    x2 = out_proj_deepnorm(a.reshape(n, -1), None, w_out.astype(BF16), x2, row(ln1_g), row(ln1_b))
    x2 = moe_top2_deepnorm(x2, jnp.pad(router, ((0, 0), (0, LANES - router.shape[1]))),
                           wg.astype(BF16), wu.astype(BF16), wd.astype(BF16), row(ln2_g), row(ln2_b))
    return x2.reshape(bsz, t, d)


def kernel(x, positions, ev_w_in, ev_conv_w, ev_conv_b, ev_dt_bias, ev_a_log, ev_d_skip, ev_m_norm, ev_mu_shift, ev_w0, ev_w_up, ev_a0, ev_a_up, ev_g_up, ev_k_k, ev_k_a, ev_r_k, ev_rln_w, ev_rln_b, ev_w_out, ev_ln1_g, ev_ln1_b, ev_ffn_wg, ev_ffn_wu, ev_ffn_wd, ev_ln2_g, ev_ln2_b, od_w_in, od_w_out, od_ln1_g, od_ln1_b, od_router, od_moe_wg, od_moe_wu, od_moe_wd, od_ln2_g, od_ln2_b):
    for layer in range(DEPTH):
        i = layer // 2
        if layer % 2 == 0:
            x = _even_layer(x, ev_w_in[i], ev_conv_w[i], ev_conv_b[i], ev_dt_bias[i], ev_a_log[i], ev_d_skip[i],
                            ev_m_norm[i], ev_mu_shift[i], ev_w0[i], ev_w_up[i], ev_a0[i], ev_a_up[i], ev_g_up[i],
                            ev_k_k[i], ev_k_a[i], ev_r_k[i], ev_rln_w[i], ev_rln_b[i], ev_w_out[i],
                            ev_ln1_g[i], ev_ln1_b[i], ev_ffn_wg[i], ev_ffn_wu[i], ev_ffn_wd[i],
                            ev_ln2_g[i], ev_ln2_b[i])
        else:
            x = _odd_layer(x, positions, od_w_in[i], od_w_out[i], od_ln1_g[i], od_ln1_b[i], od_router[i],
                           od_moe_wg[i], od_moe_wu[i], od_moe_wd[i], od_ln2_g[i], od_ln2_b[i])
    return x
```
